```python
import math
import jax, jax.numpy as jnp
from jax import lax
import numpy as np

D_MODEL = 1024
BATCH = 16
SEQ = 2048
DEPTH = 2

HEAD_DIM = 64
SB_HEADS = 8
DIFF_HEADS = 4
SWA_Q_HEADS = 8
SWA_KV_HEADS = 2
SWA_GROUP = SWA_Q_HEADS // SWA_KV_HEADS
WINDOW = 128
BLOCK = 128
D_FF = 2816
N_BUCKETS = 32
MAX_DISTANCE = 128
N_BIAS_HEADS = DIFF_HEADS + SWA_Q_HEADS
EPS = 1e-6

SB_W = SB_HEADS * HEAD_DIM
DIFF_QK_W = DIFF_HEADS * 2 * HEAD_DIM
DIFF_V_W = DIFF_HEADS * 2 * HEAD_DIM
SWA_Q_W = SWA_Q_HEADS * HEAD_DIM
SWA_KV_W = SWA_KV_HEADS * HEAD_DIM
IN_WIDTHS = (SB_W, SB_W, SB_W, DIFF_QK_W, DIFF_QK_W, DIFF_V_W,
             SWA_Q_W, SWA_KV_W, SWA_KV_W, D_MODEL, D_MODEL, D_MODEL)
IN_W = SB_W * 3 + DIFF_QK_W * 2 + DIFF_V_W + SWA_Q_W + 2 * SWA_KV_W + 3 * D_MODEL

kernel_name = "hybrid_sb_diff_swa_macaron"


def _offsets(widths):
    out, acc = [], 0
    for w in widths[:-1]:
        acc += w
        out.append(acc)
    return out


def rmsnorm(x, g):
    xf = x.astype(jnp.float32)
    y = xf * lax.rsqrt(jnp.mean(xf * xf, axis=-1, keepdims=True) + EPS)
    return (y * g.astype(jnp.float32)).astype(x.dtype)


def swiglu(h, w_in, w_out):
    g, u = jnp.split(h @ w_in, 2, axis=-1)
    return (jax.nn.silu(g) * u) @ w_out


def t5_bucket(dist):
    n = jnp.maximum(dist, 0)
    max_exact = N_BUCKETS // 2
    is_small = n < max_exact
    nf = jnp.maximum(n, 1).astype(jnp.float32)
    large = max_exact + (jnp.log(nf / max_exact) / math.log(MAX_DISTANCE / max_exact)
                         * (N_BUCKETS - max_exact)).astype(jnp.int32)
    large = jnp.minimum(large, N_BUCKETS - 1)
    return jnp.where(is_small, n, large)


def stick_breaking_attention(q, k, v):
    S = q.shape[1]
    scale = HEAD_DIM ** -0.5
    outs = []
    for i in range(S // BLOCK):
        q0 = i * BLOCK
        L = q0 + BLOCK
        z = jnp.einsum('bqhd,bkhd->bhqk', q[:, q0:L], k[:, :L],
                       preferred_element_type=jnp.float32) * scale
        t = q0 + jnp.arange(BLOCK)[:, None]
        s = jnp.arange(L)[None, :]
        mask = s < t
        log_fail = jnp.where(mask, jax.nn.log_sigmoid(-z), 0.0)
        between = lax.cumsum(log_fail, axis=3, reverse=True) - log_fail
        w = jnp.where(mask, jnp.exp(jax.nn.log_sigmoid(z) + between), 0.0)
        outs.append(jnp.einsum('bhqk,bkhd->bqhd', w.astype(v.dtype), v[:, :L]))
    return jnp.concatenate(outs, axis=1)


def differential_attention(q, k, v, bias_table, lam, lam_init, sub_gain):
    B, S = q.shape[0], q.shape[1]
    scale = HEAD_DIM ** -0.5
    outs = []
    for i in range(S // BLOCK):
        q0 = i * BLOCK
        L = q0 + BLOCK
        z = jnp.einsum('bqhcd,bkhcd->bhcqk', q[:, q0:L], k[:, :L],
                       preferred_element_type=jnp.float32) * scale
        dist = (q0 + jnp.arange(BLOCK))[:, None] - jnp.arange(L)[None, :]
        bias = bias_table[t5_bucket(dist)].astype(jnp.float32).transpose(2, 0, 1)
        z = jnp.where(dist >= 0, z + bias[None, :, None], -jnp.inf)
        p = jax.nn.softmax(z, axis=-1)
        w = p[:, :, 0] - lam * p[:, :, 1]
        outs.append(jnp.einsum('bhqk,bkhe->bqhe', w.astype(v.dtype), v[:, :L]))
    o = jnp.concatenate(outs, axis=1)
    o = rmsnorm(o, sub_gain) * (1.0 - lam_init)
    return o.reshape(B, S, DIFF_HEADS * 2 * HEAD_DIM)


def sliding_window_attention(q, k, v, bias_table, sinks):
    B, S = q.shape[0], q.shape[1]
    nb = S // BLOCK
    scale = HEAD_DIM ** -0.5
    qb = q.reshape(B, nb, BLOCK, SWA_KV_HEADS, SWA_GROUP, HEAD_DIM)
    kb = k.reshape(B, nb, BLOCK, SWA_KV_HEADS, HEAD_DIM)
    vb = v.reshape(B, nb, BLOCK, SWA_KV_HEADS, HEAD_DIM)
    pad = ((0, 0), (1, 0), (0, 0), (0, 0), (0, 0))
    kw = jnp.concatenate([jnp.pad(kb, pad)[:, :-1], kb], axis=2)
    vw = jnp.concatenate([jnp.pad(vb, pad)[:, :-1], vb], axis=2)
    z = jnp.einsum('bnqhgd,bnkhd->bnhgqk', qb, kw,
                   preferred_element_type=jnp.float32) * scale
    a = jnp.arange(BLOCK)[:, None]
    c = jnp.arange(2 * BLOCK)[None, :]
    dist = BLOCK + a - c
    band = (dist >= 0) & (dist < WINDOW)
    valid = band[None] & ((jnp.arange(nb)[:, None, None] > 0) | (c >= BLOCK)[None])
    bias = bias_table[t5_bucket(dist)].astype(jnp.float32)
    bias = bias.reshape(BLOCK, 2 * BLOCK, SWA_KV_HEADS, SWA_GROUP).transpose(2, 3, 0, 1)
    z = jnp.where(valid[None, :, None, None], z + bias[None, None], -jnp.inf)
    sink = sinks.astype(jnp.float32).reshape(SWA_KV_HEADS, SWA_GROUP)
    sink_col = jnp.broadcast_to(sink[None, None, :, :, None, None], z.shape[:-1] + (1,))
    p = jax.nn.softmax(jnp.concatenate([z, sink_col], axis=-1), axis=-1)[..., :-1]
    o = jnp.einsum('bnhgqk,bnkhd->bnqhgd', p.astype(v.dtype), vw)
    return o.reshape(B, S, SWA_Q_HEADS * HEAD_DIM)


def hybrid_mixer(h, w_in, q_norm_diff, k_norm_diff, q_norm_swa, k_norm_swa, lam_vecs, lam_init,
                 diff_subln, sinks, rel_bias, w_proj_sb, w_proj_diff, w_proj_swa, w_out):
    B, S, _ = h.shape
    proj = h @ w_in
    (qa, ka, va, qd, kd, vd, qs, ks, vs, ga, gd, gs) = jnp.split(proj, _offsets(IN_WIDTHS), axis=-1)
    hd = (B, S, SB_HEADS, HEAD_DIM)
    o_sb = stick_breaking_attention(qa.reshape(hd), ka.reshape(hd), va.reshape(hd)).reshape(B, S, SB_W)
    qd = rmsnorm(qd.reshape(B, S, DIFF_HEADS, 2, HEAD_DIM), q_norm_diff)
    kd = rmsnorm(kd.reshape(B, S, DIFF_HEADS, 2, HEAD_DIM), k_norm_diff)
    vd = vd.reshape(B, S, DIFF_HEADS, 2 * HEAD_DIM)
    lv = lam_vecs.astype(jnp.float32)
    lam = jnp.exp(jnp.sum(lv[0] * lv[1])) - jnp.exp(jnp.sum(lv[2] * lv[3])) + lam_init
    o_diff = differential_attention(qd, kd, vd, rel_bias[:, :DIFF_HEADS], lam, lam_init, diff_subln)
    qs = rmsnorm(qs.reshape(B, S, SWA_KV_HEADS, SWA_GROUP, HEAD_DIM), q_norm_swa)
    ks = rmsnorm(ks.reshape(B, S, SWA_KV_HEADS, HEAD_DIM), k_norm_swa)
    vs = vs.reshape(B, S, SWA_KV_HEADS, HEAD_DIM)
    o_swa = sliding_window_attention(qs, ks, vs, rel_bias[:, DIFF_HEADS:], sinks)
    merged = (jax.nn.sigmoid(ga) * (o_sb @ w_proj_sb)
              + jax.nn.sigmoid(gd) * (o_diff @ w_proj_diff)
              + jax.nn.sigmoid(gs) * (o_swa @ w_proj_swa))
    return merged @ w_out


def setup_inputs(seed: int = 0) -> dict:
    key = jax.random.key(seed)
    ks = jax.random.split(key, 24)
    f32 = jnp.float32

    def w(k, shape, fan_in):
        return jax.random.normal(k, shape, f32) * (fan_in ** -0.5)

    def gain(k, shape):
        return 1.0 + 0.01 * jax.random.normal(k, shape, f32)

    return {
        "x": jax.random.normal(ks[0], (BATCH, SEQ, D_MODEL), f32),
        "ffn1_norm": gain(ks[1], (DEPTH, D_MODEL)),
        "ffn1_w_in": w(ks[2], (DEPTH, D_MODEL, 2 * D_FF), D_MODEL),
        "ffn1_w_out": w(ks[3], (DEPTH, D_FF, D_MODEL), D_FF),
        "mix_norm": gain(ks[4], (DEPTH, D_MODEL)),
        "w_in": w(ks[5], (DEPTH, D_MODEL, IN_W), D_MODEL),
        "q_norm_diff": gain(ks[6], (DEPTH, HEAD_DIM)),
        "k_norm_diff": gain(ks[7], (DEPTH, HEAD_DIM)),
        "q_norm_swa": gain(ks[8], (DEPTH, HEAD_DIM)),
        "k_norm_swa": gain(ks[9], (DEPTH, HEAD_DIM)),
        "diff_lambda": 0.1 * jax.random.normal(ks[10], (DEPTH, 4, HEAD_DIM), f32),
        "diff_subln": gain(ks[11], (DEPTH, 2 * HEAD_DIM)),
        "swa_sinks": 0.5 * jax.random.normal(ks[12], (DEPTH, SWA_Q_HEADS), f32),
        "rel_bias": 0.5 * jax.random.normal(ks[13], (N_BUCKETS, N_BIAS_HEADS), f32),
        "w_proj_sb": w(ks[14], (DEPTH, SB_W, D_MODEL), SB_W),
        "w_proj_diff": w(ks[15], (DEPTH, DIFF_V_W, D_MODEL), DIFF_V_W),
        "w_proj_swa": w(ks[16], (DEPTH, SWA_Q_W, D_MODEL), SWA_Q_W),
        "w_out": w(ks[17], (DEPTH, D_MODEL, D_MODEL), D_MODEL),
        "ffn2_norm": gain(ks[18], (DEPTH, D_MODEL)),
        "ffn2_w_in": w(ks[19], (DEPTH, D_MODEL, 2 * D_FF), D_MODEL),
        "ffn2_w_out": w(ks[20], (DEPTH, D_FF, D_MODEL), D_FF),
    }


def reference(x, ffn1_norm, ffn1_w_in, ffn1_w_out, mix_norm, w_in, q_norm_diff, k_norm_diff,
              q_norm_swa, k_norm_swa, diff_lambda, diff_subln, swa_sinks, rel_bias,
              w_proj_sb, w_proj_diff, w_proj_swa, w_out, ffn2_norm, ffn2_w_in, ffn2_w_out):
    for l in range(DEPTH):
        lam_init = 0.8 - 0.6 * math.exp(-0.3 * l)
        x = x + 0.5 * swiglu(rmsnorm(x, ffn1_norm[l]), ffn1_w_in[l], ffn1_w_out[l])
        x = x + hybrid_mixer(rmsnorm(x, mix_norm[l]), w_in[l], q_norm_diff[l], k_norm_diff[l],
                             q_norm_swa[l], k_norm_swa[l], diff_lambda[l], lam_init, diff_subln[l],
                             swa_sinks[l], rel_bias, w_proj_sb[l], w_proj_diff[l], w_proj_swa[l],
                             w_out[l])
        x = x + 0.5 * swiglu(rmsnorm(x, ffn2_norm[l]), ffn2_w_in[l], ffn2_w_out[l])
    return x
```

```python
import functools
import math

import jax
import jax.numpy as jnp
from jax import lax
from jax.experimental import pallas as pl
from jax.experimental.pallas import tpu as pltpu

F32 = jnp.float32
BF16 = jnp.bfloat16

HEAD_DIM = 64
SB_HEADS = 8
DIFF_HEADS = 4
SWA_Q_HEADS = 8
SWA_KV_HEADS = 2
SWA_GROUP = SWA_Q_HEADS // SWA_KV_HEADS
WINDOW = 128
N_BUCKETS = 32
MAX_DISTANCE = 128
EPS = 1e-6
SCALE = HEAD_DIM ** -0.5

SB_W = SB_HEADS * HEAD_DIM
DIFF_W = DIFF_HEADS * 2 * HEAD_DIM
SWA_Q_W = SWA_Q_HEADS * HEAD_DIM
SWA_KV_W = SWA_KV_HEADS * HEAD_DIM

V7X_LANES = 128
V7X_MXU_DIM = 256
V7X_VMEM_BYTES = 64 * 1024 * 1024

ATT_TILE = V7X_MXU_DIM
SWA_TILE = WINDOW
ROW_TILE = 512
PAIR_W = 2 * HEAD_DIM
assert PAIR_W == V7X_LANES

P1_W = 2 * SB_W + 3 * 1024
P2_W = 2 * DIFF_W + SWA_Q_W + 2 * SWA_KV_W
P3_W = SB_W + DIFF_W + 2 * SWA_KV_W
P1_CHUNK = 512
P2_CHUNK = 256
P3_CHUNK = 256

_NT = (((1,), (1,)), ((), ()))


def _compiler_params(semantics, vmem_bytes):
    assert vmem_bytes < V7X_VMEM_BYTES
    return pltpu.CompilerParams(dimension_semantics=semantics, vmem_limit_bytes=vmem_bytes)


def _rmsnorm_rows(x, g):
    ms = jnp.mean(x * x, axis=-1, keepdims=True)
    return x * lax.rsqrt(ms + EPS) * g


def _ffn_kernel(x_ref, g_ref, wg_ref, wu_ref, wo_ref, o_ref, h_ref, acc_ref):
    j = pl.program_id(1)

    @pl.when(j == 0)
    def _():
        h_ref[...] = _rmsnorm_rows(x_ref[...], g_ref[...]).astype(BF16)
        acc_ref[...] = jnp.zeros_like(acc_ref)

    h = h_ref[...]
    g = jnp.dot(h, wg_ref[...], preferred_element_type=F32)
    u = jnp.dot(h, wu_ref[...], preferred_element_type=F32)
    a = (g * (1.0 / (1.0 + jnp.exp(-g))) * u).astype(BF16)
    acc_ref[...] += jnp.dot(a, wo_ref[...], preferred_element_type=F32)

    @pl.when(j == pl.num_programs(1) - 1)
    def _():
        o_ref[...] = x_ref[...] + 0.5 * acc_ref[...]


def _ffn(xt, gain, w_in, w_out):
    t, d = xt.shape
    d_ff = w_out.shape[0]
    tf = d_ff // 2
    assert tf % V7X_LANES == 0 and t % ROW_TILE == 0
    nj = d_ff // tf
    w_in = w_in.astype(BF16)
    w_out = w_out.astype(BF16)
    return pl.pallas_call(
        _ffn_kernel,
        name="ffn",
        grid=(t // ROW_TILE, nj),
        in_specs=[
            pl.BlockSpec((ROW_TILE, d), lambda i, j: (i, 0)),
            pl.BlockSpec((1, d), lambda i, j: (0, 0)),
            pl.BlockSpec((d, tf), lambda i, j: (0, j)),
            pl.BlockSpec((d, tf), lambda i, j: (0, nj + j)),
            pl.BlockSpec((tf, d), lambda i, j: (j, 0)),
        ],
        out_specs=pl.BlockSpec((ROW_TILE, d), lambda i, j: (i, 0)),
        out_shape=jax.ShapeDtypeStruct((t, d), F32),
        scratch_shapes=[pltpu.VMEM((ROW_TILE, d), BF16), pltpu.VMEM((ROW_TILE, d), F32)],
        compiler_params=_compiler_params(("parallel", "arbitrary"), 52 * 1024 * 1024),
    )(xt, gain.reshape(1, d), w_in, w_in, w_out)


def _proj_kernel(x_ref, ng_ref, w1_ref, w2_ref, g2_ref, gm_ref, w3_ref,
                 p1_ref, p2_ref, p3_ref):
    h = _rmsnorm_rows(x_ref[...], ng_ref[...]).astype(BF16)
    for c in range(P1_W // P1_CHUNK):
        cols = slice(c * P1_CHUNK, (c + 1) * P1_CHUNK)
        p1_ref[:, cols] = jnp.dot(h, w1_ref[:, cols], preferred_element_type=F32).astype(BF16)
    for c in range(P2_W // P2_CHUNK):
        cols = slice(c * P2_CHUNK, (c + 1) * P2_CHUNK)
        y = jnp.dot(h, w2_ref[:, cols], preferred_element_type=F32)
        ms = jnp.dot((y * y).astype(BF16), gm_ref[...], preferred_element_type=F32)
        p2_ref[:, cols] = (y * lax.rsqrt(ms + EPS) * g2_ref[:, cols]).astype(BF16)
    for c in range(P3_W // P3_CHUNK):
        cols = slice(c * P3_CHUNK, (c + 1) * P3_CHUNK)
        yt = jnp.dot(h, w3_ref[:, cols], preferred_element_type=F32).T
        for r in range(ROW_TILE // ATT_TILE):
            p3_ref[r, cols, :] = yt[:, r * ATT_TILE:(r + 1) * ATT_TILE].astype(BF16)


def _projection(xt, norm_gain, w1, w2, g2, gmat, w3):
    t, d = xt.shape
    const = lambda i: (0, 0)
    return pl.pallas_call(
        _proj_kernel,
        name="mixer_proj",
        grid=(t // ROW_TILE,),
        in_specs=[
            pl.BlockSpec((ROW_TILE, d), lambda i: (i, 0)),
            pl.BlockSpec((1, d), const),
            pl.BlockSpec((d, P1_W), const, pipeline_mode=pl.Buffered(1)),
            pl.BlockSpec((d, P2_W), const, pipeline_mode=pl.Buffered(1)),
            pl.BlockSpec((1, P2_W), const),
            pl.BlockSpec((P2_CHUNK, P2_CHUNK), const),
            pl.BlockSpec((d, P3_W), const, pipeline_mode=pl.Buffered(1)),
        ],
        out_specs=[
            pl.BlockSpec((ROW_TILE, P1_W), lambda i: (i, 0)),
            pl.BlockSpec((ROW_TILE, P2_W), lambda i: (i, 0)),
            pl.BlockSpec((ROW_TILE // ATT_TILE, P3_W, ATT_TILE), lambda i: (i, 0, 0)),
        ],
        out_shape=[
            jax.ShapeDtypeStruct((t, P1_W), BF16),
            jax.ShapeDtypeStruct((t, P2_W), BF16),
            jax.ShapeDtypeStruct((t // ATT_TILE, P3_W, ATT_TILE), BF16),
        ],
        compiler_params=_compiler_params(("parallel",), 52 * 1024 * 1024),
    )(xt, norm_gain.reshape(1, d), w1, w2, g2, gmat, w3)


def _t5_bucket(dist):
    n = jnp.maximum(dist, 0)
    max_exact = N_BUCKETS // 2
    nf = jnp.maximum(n, 1).astype(F32)
    large = max_exact + (jnp.log(nf / max_exact) / math.log(MAX_DISTANCE / max_exact)
                         * (N_BUCKETS - max_exact)).astype(jnp.int32)
    large = jnp.minimum(large, N_BUCKETS - 1)
    return jnp.where(n < max_exact, n, large)


def _bias_lookup(bucket, tab_ref, head):
    val = jnp.zeros(bucket.shape, F32)
    for b in range(N_BUCKETS):
        val = jnp.where(bucket == b, tab_ref[b, head], val)
    return val


def _bias_kernel(tab_ref, dbias_ref, sbias_ref):
    kk = lax.broadcasted_iota(jnp.int32, (ATT_TILE, ATT_TILE), 0)
    qq = lax.broadcasted_iota(jnp.int32, (ATT_TILE, ATT_TILE), 1)
    for v in range(2):
        dist = qq - kk + v * ATT_TILE
        bucket = _t5_bucket(dist)
        for h in range(DIFF_HEADS):
            dbias_ref[h, v] = jnp.where(dist >= 0, _bias_lookup(bucket, tab_ref, h), -jnp.inf)
    kk = lax.broadcasted_iota(jnp.int32, (2 * SWA_TILE, SWA_TILE), 0)
    qq = lax.broadcasted_iota(jnp.int32, (2 * SWA_TILE, SWA_TILE), 1)
    dist = SWA_TILE + qq - kk
    bucket = _t5_bucket(dist)
    for h in range(SWA_Q_HEADS):
        val = _bias_lookup(bucket, tab_ref, DIFF_HEADS + h)
        inband = jnp.where(dist >= 0, jnp.where(dist < WINDOW, val, -jnp.inf), -jnp.inf)
        sbias_ref[h, 0] = inband
        sbias_ref[h, 1] = jnp.where(kk >= SWA_TILE, inband, -jnp.inf)


def _bias_tiles(rel_bias):
    return pl.pallas_call(
        _bias_kernel,
        name="bias_tiles",
        in_specs=[pl.BlockSpec(memory_space=pltpu.SMEM)],
        out_specs=[pl.BlockSpec(memory_space=pltpu.VMEM), pl.BlockSpec(memory_space=pltpu.VMEM)],
        out_shape=[
            jax.ShapeDtypeStruct((DIFF_HEADS, 2, ATT_TILE, ATT_TILE), F32),
            jax.ShapeDtypeStruct((SWA_Q_HEADS, 2, 2 * SWA_TILE, SWA_TILE), F32),
        ],
    )(rel_bias)


def _split_pair(q):
    qf = q.astype(F32)
    lane = lax.broadcasted_iota(jnp.int32, qf.shape, 1)
    zero = jnp.zeros_like(qf)
    return (jnp.where(lane < HEAD_DIM, qf, zero).astype(BF16),
            jnp.where(lane >= HEAD_DIM, qf, zero).astype(BF16))


def _merge_pair_t(lo, hi):
    row = lax.broadcasted_iota(jnp.int32, lo.shape, 0)
    return jnp.where(row < HEAD_DIM, lo, hi)


def _sb_kernel(q_ref, k_ref, vt_ref, mcat_ref, o_ref, acc_ref):
    i = pl.program_id(2)
    q_heads = _split_pair(q_ref[0])
    mcat = mcat_ref[...]
    kk = lax.broadcasted_iota(jnp.int32, (ATT_TILE, ATT_TILE), 0)
    qq = lax.broadcasted_iota(jnp.int32, (ATT_TILE, ATT_TILE), 1)
    strictly_before = kk < qq

    def tile(j, carries, keep):
        k = k_ref[0, pl.ds(pl.multiple_of(j * ATT_TILE, ATT_TILE), ATT_TILE), :]
        vt = vt_ref[j]
        new = []
        for x in range(2):
            z = lax.dot_general(k, q_heads[x], _NT, preferred_element_type=F32)
            log_fail = -(jnp.maximum(z, 0.0) + jnp.log(1.0 + jnp.exp(-jnp.abs(z))))
            if keep is not None:
                log_fail = jnp.where(keep, log_fail, 0.0)
            hi = log_fail.astype(BF16)
            lo = (log_fail - hi.astype(F32)).astype(BF16)
            between = jnp.dot(mcat, jnp.concatenate([hi, lo], axis=0), preferred_element_type=F32)
            w = jnp.exp((z + log_fail) + (between + carries[x]))
            if keep is not None:
                w = jnp.where(keep, w, 0.0)
            acc_ref[x] += jnp.dot(vt, w.astype(BF16), preferred_element_type=F32)
            new.append(carries[x] + jnp.sum(log_fail, axis=0, keepdims=True))
        return tuple(new)

    acc_ref[...] = jnp.zeros_like(acc_ref)
    zero = jnp.zeros((1, ATT_TILE), F32)
    carries = tile(i, (zero, zero), strictly_before)
    lax.fori_loop(0, i, lambda jj, c: tile(i - 1 - jj, c, None), carries)
    o_ref[0] = _merge_pair_t(acc_ref[0], acc_ref[1]).T.astype(BF16)


def _sb_attention(p1, p3, mcat, batch, seq):
    nq = seq // ATT_TILE
    p1 = p1.reshape(batch, seq, P1_W)
    kcol = SB_W // PAIR_W
    return pl.pallas_call(
        _sb_kernel,
        name="sb_attn",
        grid=(batch, SB_HEADS // 2, nq),
        in_specs=[
            pl.BlockSpec((1, ATT_TILE, PAIR_W), lambda b, p, i: (b, i, p)),
            pl.BlockSpec((1, seq, PAIR_W), lambda b, p, i: (b, 0, kcol + p)),
            pl.BlockSpec((nq, PAIR_W, ATT_TILE), lambda b, p, i: (b, p, 0)),
            pl.BlockSpec((ATT_TILE, 2 * ATT_TILE), lambda b, p, i: (0, 0)),
        ],
        out_specs=pl.BlockSpec((1, ATT_TILE, PAIR_W), lambda b, p, i: (b, i, p)),
        out_shape=jax.ShapeDtypeStruct((batch, seq, SB_W), BF16),
        scratch_shapes=[pltpu.VMEM((2, PAIR_W, ATT_TILE), F32)],
        compiler_params=_compiler_params(("parallel", "parallel", "arbitrary"), 32 * 1024 * 1024),
    )(p1, p1, p3, mcat)


def _diff_kernel(tab_ref, q_ref, k_ref, vt_ref, bias_ref, lam_ref, gain_ref, o_ref, acc_ref,
                 *, lam_init):
    h = pl.program_id(1)
    i = pl.program_id(2)
    q_comps = _split_pair(q_ref[0])

    def tile(j, state, bias):
        k = k_ref[0, pl.ds(pl.multiple_of(j * ATT_TILE, ATT_TILE), ATT_TILE), :]
        vt = vt_ref[j]
        new = []
        for c in range(2):
            m_prev, l_prev = state[c]
            s = lax.dot_general(k, q_comps[c], _NT, preferred_element_type=F32) + bias
            m_new = jnp.maximum(m_prev, jnp.max(s, axis=0, keepdims=True))
            alpha = jnp.exp(m_prev - m_new)
            p = jnp.exp(s - m_new)
            l_new = alpha * l_prev + jnp.sum(p, axis=0, keepdims=True)
            acc_ref[c] = alpha * acc_ref[c] + jnp.dot(vt, p.astype(BF16), preferred_element_type=F32)
            new.append((m_new, l_new))
        return tuple(new)

    acc_ref[...] = jnp.zeros_like(acc_ref)
    init = (jnp.full((1, ATT_TILE), -jnp.inf, F32), jnp.zeros((1, ATT_TILE), F32))
    state = tile(i, (init, init), bias_ref[0, 0])

    def with_prev(st):
        return tile(i - 1, st, bias_ref[0, 1])

    state = lax.cond(i >= 1, with_prev, lambda st: st, state)
    far_bias = tab_ref[N_BUCKETS - 1, h]
    state = lax.fori_loop(0, jnp.maximum(i - 1, 0),
                          lambda jj, st: tile(i - 2 - jj, st, far_bias), state)

    lv = lam_ref[...]
    lam = (jnp.exp(jnp.sum(lv[0:1] * lv[1:2], axis=-1, keepdims=True))
           - jnp.exp(jnp.sum(lv[2:3] * lv[3:4], axis=-1, keepdims=True)) + lam_init)
    (_, l1), (_, l2) = state
    o = acc_ref[0] / l1 - lam * (acc_ref[1] / l2)
    ms = jnp.mean(o * o, axis=0, keepdims=True)
    y = o * lax.rsqrt(ms + EPS) * gain_ref[...] * (1.0 - lam_init)
    o_ref[0] = y.T.astype(BF16)


def _diff_attention(rel_bias, p2, p3, dbias, lam_vecs, sub_gain, lam_init, batch, seq):
    nq = seq // ATT_TILE
    p2 = p2.reshape(batch, seq, P2_W)
    kcol = DIFF_W // PAIR_W
    vrow = SB_W // PAIR_W
    return pl.pallas_call(
        functools.partial(_diff_kernel, lam_init=lam_init),
        name="diff_attn",
        grid=(batch, DIFF_HEADS, nq),
        in_specs=[
            pl.BlockSpec(memory_space=pltpu.SMEM),
            pl.BlockSpec((1, ATT_TILE, PAIR_W), lambda b, h, i: (b, i, h)),
            pl.BlockSpec((1, seq, PAIR_W), lambda b, h, i: (b, 0, kcol + h)),
            pl.BlockSpec((nq, PAIR_W, ATT_TILE), lambda b, h, i: (b, vrow + h, 0)),
            pl.BlockSpec((1, 2, ATT_TILE, ATT_TILE), lambda b, h, i: (h, 0, 0, 0)),
            pl.BlockSpec((4, HEAD_DIM), lambda b, h, i: (0, 0)),
            pl.BlockSpec((PAIR_W, 1), lambda b, h, i: (0, 0)),
        ],
        out_specs=pl.BlockSpec((1, ATT_TILE, PAIR_W), lambda b, h, i: (b, i, h)),
        out_shape=jax.ShapeDtypeStruct((batch, seq, DIFF_W), BF16),
        scratch_shapes=[pltpu.VMEM((2, PAIR_W, ATT_TILE), F32)],
        compiler_params=_compiler_params(("parallel", "parallel", "arbitrary"), 32 * 1024 * 1024),
    )(rel_bias, p2, p2, p3, dbias, lam_vecs, sub_gain.reshape(PAIR_W, 1))


def _swa_kernel(sink_ref, q_ref, kp_ref, ko_ref, vtp_ref, vto_ref, bias_ref, o_ref):
    p = pl.program_id(1)
    i = pl.program_id(2)
    q_heads = _split_pair(q_ref[0])
    k = jnp.concatenate([kp_ref[0], ko_ref[0]], axis=0)
    vt = jnp.concatenate([vtp_ref[0], vto_ref[0]], axis=1)
    variant = jnp.where(i == 0, 1, 0)
    outs = []
    for x in range(2):
        sink = sink_ref[2 * p + x]
        s = lax.dot_general(k, q_heads[x], _NT, preferred_element_type=F32) + bias_ref[x, variant]
        m = jnp.maximum(jnp.max(s, axis=0, keepdims=True), sink)
        e = jnp.exp(s - m)
        denom = jnp.sum(e, axis=0, keepdims=True) + jnp.exp(sink - m)
        outs.append(jnp.dot(vt, e.astype(BF16), preferred_element_type=F32) / denom)
    o_ref[0] = _merge_pair_t(outs[0], outs[1]).T.astype(BF16)


def _swa_attention(sinks, p2, p3, sbias, batch, seq):
    nq = seq // SWA_TILE
    p2 = p2.reshape(batch, seq, P2_W)
    qcol = 2 * DIFF_W // PAIR_W
    kcol = (2 * DIFF_W + SWA_Q_W) // PAIR_W
    vrow = (SB_W + DIFF_W) // PAIR_W
    chunks = seq // ATT_TILE
    per_chunk = ATT_TILE // SWA_TILE
    pairs_per_kv = SWA_GROUP // 2

    def prev(i):
        return jnp.maximum(i - 1, 0)

    def vt_map(block):
        def index(b, p, i):
            kb = block(i)
            return (b * chunks + kb // per_chunk, vrow + p // pairs_per_kv, kb % per_chunk)
        return index

    return pl.pallas_call(
        _swa_kernel,
        name="swa_attn",
        grid=(batch, SWA_Q_HEADS // 2, nq),
        in_specs=[
            pl.BlockSpec(memory_space=pltpu.SMEM),
            pl.BlockSpec((1, SWA_TILE, PAIR_W), lambda b, p, i: (b, i, qcol + p)),
            pl.BlockSpec((1, SWA_TILE, PAIR_W), lambda b, p, i: (b, prev(i), kcol + p // pairs_per_kv)),
            pl.BlockSpec((1, SWA_TILE, PAIR_W), lambda b, p, i: (b, i, kcol + p // pairs_per_kv)),
            pl.BlockSpec((1, PAIR_W, SWA_TILE), vt_map(prev)),
            pl.BlockSpec((1, PAIR_W, SWA_TILE), vt_map(lambda i: i)),
            pl.BlockSpec((2, 2, 2 * SWA_TILE, SWA_TILE), lambda b, p, i: (p, 0, 0, 0)),
        ],
        out_specs=pl.BlockSpec((1, SWA_TILE, PAIR_W), lambda b, p, i: (b, i, p)),
        out_shape=jax.ShapeDtypeStruct((batch, seq, SWA_Q_W), BF16),
        compiler_params=_compiler_params(("parallel", "parallel", "arbitrary"), 32 * 1024 * 1024),
    )(sinks, p2, p2, p2, p3, p3, sbias)


def _merge_kernel(x_ref, osb_ref, odf_ref, osw_ref, ga_ref, gd_ref, gs_ref,
                  wsb_ref, wdf_ref, wsw_ref, wo_ref, o_ref):
    def branch(o_ref_, w_ref, g_ref):
        gate = 1.0 / (1.0 + jnp.exp(-g_ref[...].astype(F32)))
        return gate * jnp.dot(o_ref_[...], w_ref[...], preferred_element_type=F32)

    merged = (branch(osb_ref, wsb_ref, ga_ref) + branch(odf_ref, wdf_ref, gd_ref)
              + branch(osw_ref, wsw_ref, gs_ref))
    o_ref[...] = x_ref[...] + jnp.dot(merged.astype(BF16), wo_ref[...], preferred_element_type=F32)


def _merge(xt, o_sb, o_diff, o_swa, p1, w_sb, w_diff, w_swa, w_out):
    t, d = xt.shape
    gcol = 2 * SB_W // d
    row = lambda i: (i, 0)
    const = lambda i: (0, 0)
    return pl.pallas_call(
        _merge_kernel,
        name="merge",
        grid=(t // ROW_TILE,),
        in_specs=[
            pl.BlockSpec((ROW_TILE, d), row),
            pl.BlockSpec((ROW_TILE, SB_W), row),
            pl.BlockSpec((ROW_TILE, DIFF_W), row),
            pl.BlockSpec((ROW_TILE, SWA_Q_W), row),
            pl.BlockSpec((ROW_TILE, d), lambda i: (i, gcol)),
            pl.BlockSpec((ROW_TILE, d), lambda i: (i, gcol + 1)),
            pl.BlockSpec((ROW_TILE, d), lambda i: (i, gcol + 2)),
            pl.BlockSpec((SB_W, d), const),
            pl.BlockSpec((DIFF_W, d), const),
            pl.BlockSpec((SWA_Q_W, d), const),
            pl.BlockSpec((d, d), const),
        ],
        out_specs=pl.BlockSpec((ROW_TILE, d), row),
        out_shape=jax.ShapeDtypeStruct((t, d), F32),
        compiler_params=_compiler_params(("parallel",), 48 * 1024 * 1024),
    )(xt, o_sb.reshape(t, SB_W), o_diff.reshape(t, DIFF_W), o_swa.reshape(t, SWA_Q_W),
      p1, p1, p1, w_sb.astype(BF16), w_diff.astype(BF16), w_swa.astype(BF16), w_out.astype(BF16))


def _projection_weights(w_in, q_norm_diff, k_norm_diff, q_norm_swa, k_norm_swa):
    widths = (SB_W, SB_W, SB_W, DIFF_W, DIFF_W, DIFF_W, SWA_Q_W, SWA_KV_W, SWA_KV_W)
    offs = [0]
    for w in widths:
        offs.append(offs[-1] + w)
    qa, ka, va, qd, kd, vd, qs, ks, vs = (w_in[:, offs[n]:offs[n + 1]] for n in range(len(widths)))
    gates = w_in[:, offs[-1]:]

    def dup_heads(w):
        return jnp.concatenate([w[:, h * HEAD_DIM:(h + 1) * HEAD_DIM]
                                for h in range(SWA_KV_HEADS) for _ in range(2)], axis=1)

    w1 = jnp.concatenate([qa * SCALE, ka, gates], axis=1).astype(BF16)
    w2 = jnp.concatenate([qd, kd, qs, dup_heads(ks)], axis=1).astype(BF16)
    w3 = jnp.concatenate([va, vd, dup_heads(vs)], axis=1).astype(BF16)
    g2 = jnp.concatenate([
        jnp.tile(q_norm_diff * SCALE, DIFF_W // HEAD_DIM), jnp.tile(k_norm_diff, DIFF_W // HEAD_DIM),
        jnp.tile(q_norm_swa * SCALE, SWA_Q_W // HEAD_DIM), jnp.tile(k_norm_swa, 2 * SWA_KV_W // HEAD_DIM),
    ]).reshape(1, P2_W).astype(F32)
    return w1, w2, g2, w3


def _group_mean_matrix():
    g = jnp.arange(P2_CHUNK) // HEAD_DIM
    return jnp.where(g[:, None] == g[None, :], 1.0 / HEAD_DIM, 0.0).astype(BF16)


def _suffix_sum_matrix():
    idx = jnp.arange(ATT_TILE)
    m = (idx[None, :] > idx[:, None]).astype(BF16)
    return jnp.concatenate([m, m], axis=1)


def kernel(x, ffn1_norm, ffn1_w_in, ffn1_w_out, mix_norm, w_in, q_norm_diff, k_norm_diff, q_norm_swa, k_norm_swa, diff_lambda, diff_subln, swa_sinks, rel_bias, w_proj_sb, w_proj_diff, w_proj_swa, w_out, ffn2_norm, ffn2_w_in, ffn2_w_out):
    batch, seq, d = x.shape
    depth = ffn1_norm.shape[0]
    assert seq % ATT_TILE == 0 and (batch * seq) % ROW_TILE == 0
    xt = x.reshape(batch * seq, d)
    dbias, sbias = _bias_tiles(rel_bias)
    gmat = _group_mean_matrix()
    mcat = _suffix_sum_matrix()
    for l in range(depth):
        lam_init = 0.8 - 0.6 * math.exp(-0.3 * l)
        xt = _ffn(xt, ffn1_norm[l], ffn1_w_in[l], ffn1_w_out[l])
        w1, w2, g2, w3 = _projection_weights(w_in[l], q_norm_diff[l], k_norm_diff[l],
                                             q_norm_swa[l], k_norm_swa[l])
        p1, p2, p3 = _projection(xt, mix_norm[l], w1, w2, g2, gmat, w3)
        o_sb = _sb_attention(p1, p3, mcat, batch, seq)
        o_diff = _diff_attention(rel_bias, p2, p3, dbias, diff_lambda[l], diff_subln[l], lam_init,
                                 batch, seq)
        o_swa = _swa_attention(swa_sinks[l], p2, p3, sbias, batch, seq)
        xt = _merge(xt, o_sb, o_diff, o_swa, p1, w_proj_sb[l], w_proj_diff[l], w_proj_swa[l], w_out[l])
        xt = _ffn(xt, ffn2_norm[l], ffn2_w_in[l], ffn2_w_out[l])
    return xt.reshape(batch, seq, d)
```

```python
import functools
import math

import jax
import jax.numpy as jnp
from jax import lax
from jax.experimental import pallas as pl
from jax.experimental.pallas import tpu as pltpu

F32 = jnp.float32
BF16 = jnp.bfloat16

HEAD_DIM = 64
SB_HEADS = 8
DIFF_HEADS = 4
SWA_Q_HEADS = 8
SWA_KV_HEADS = 2
SWA_GROUP = SWA_Q_HEADS // SWA_KV_HEADS
WINDOW = 128
N_BUCKETS = 32
MAX_DISTANCE = 128
EPS = 1e-6
SCALE = HEAD_DIM ** -0.5
LOG2E = math.log2(math.e)

SB_W = SB_HEADS * HEAD_DIM
DIFF_W = DIFF_HEADS * 2 * HEAD_DIM
SWA_Q_W = SWA_Q_HEADS * HEAD_DIM
SWA_KV_W = SWA_KV_HEADS * HEAD_DIM

V7X_LANES = 128
V7X_MXU_DIM = 256
V7X_VMEM_BYTES = 64 * 1024 * 1024

ATT_TILE = V7X_MXU_DIM
ROW_TILE = 512
PAIR_W = 2 * HEAD_DIM
assert PAIR_W == V7X_LANES and WINDOW <= ATT_TILE

P1_W = 2 * SB_W + 3 * 1024
P2_W = 2 * DIFF_W + SWA_Q_W + 2 * SWA_KV_W
P3_W = SB_W + DIFF_W + SWA_KV_W
P1_CHUNK = 512
P2_CHUNK = 256
P3_CHUNKS = ((0, 256), (256, 256), (512, 256), (768, 256), (1024, 128))
assert sum(w for _, w in P3_CHUNKS) == P3_W

_NT = (((1,), (1,)), ((), ()))


def _compiler_params(semantics, vmem_bytes):
    assert vmem_bytes < V7X_VMEM_BYTES
    return pltpu.CompilerParams(dimension_semantics=semantics, vmem_limit_bytes=vmem_bytes)


def _rmsnorm_rows(x, g):
    ms = jnp.mean(x * x, axis=-1, keepdims=True)
    return x * lax.rsqrt(ms + EPS) * g


def _ffn_kernel(x_ref, g_ref, wg_ref, wu_ref, wo_ref, o_ref, h_ref, acc_ref):
    j = pl.program_id(1)

    @pl.when(j == 0)
    def _():
        h_ref[...] = _rmsnorm_rows(x_ref[...], g_ref[...]).astype(BF16)
        acc_ref[...] = jnp.zeros_like(acc_ref)

    h = h_ref[...]
    g = jnp.dot(h, wg_ref[...], preferred_element_type=F32)
    u = jnp.dot(h, wu_ref[...], preferred_element_type=F32)
    a = (g * (1.0 / (1.0 + jnp.exp(-g))) * u).astype(BF16)
    acc_ref[...] += jnp.dot(a, wo_ref[...], preferred_element_type=F32)

    @pl.when(j == pl.num_programs(1) - 1)
    def _():
        o_ref[...] = x_ref[...] + 0.5 * acc_ref[...]


def _ffn(xt, gain, w_in, w_out):
    t, d = xt.shape
    d_ff = w_out.shape[0]
    tf = d_ff // 2
    assert tf % V7X_LANES == 0 and t % ROW_TILE == 0
    nj = d_ff // tf
    w_in = w_in.astype(BF16)
    w_out = w_out.astype(BF16)
    return pl.pallas_call(
        _ffn_kernel,
        name="ffn",
        grid=(t // ROW_TILE, nj),
        in_specs=[
            pl.BlockSpec((ROW_TILE, d), lambda i, j: (i, 0)),
            pl.BlockSpec((1, d), lambda i, j: (0, 0)),
            pl.BlockSpec((d, tf), lambda i, j: (0, j)),
            pl.BlockSpec((d, tf), lambda i, j: (0, nj + j)),
            pl.BlockSpec((tf, d), lambda i, j: (j, 0)),
        ],
        out_specs=pl.BlockSpec((ROW_TILE, d), lambda i, j: (i, 0)),
        out_shape=jax.ShapeDtypeStruct((t, d), F32),
        scratch_shapes=[pltpu.VMEM((ROW_TILE, d), BF16), pltpu.VMEM((ROW_TILE, d), F32)],
        compiler_params=_compiler_params(("parallel", "arbitrary"), 52 * 1024 * 1024),
    )(xt, gain.reshape(1, d), w_in, w_in, w_out)


def _proj_kernel(x_ref, ng_ref, w1_ref, w2_ref, g2_ref, gm_ref, w3_ref,
                 p1_ref, p2_ref, p3_ref):
    h = _rmsnorm_rows(x_ref[...], ng_ref[...]).astype(BF16)
    for c in range(P1_W // P1_CHUNK):
        cols = slice(c * P1_CHUNK, (c + 1) * P1_CHUNK)
        p1_ref[:, cols] = jnp.dot(h, w1_ref[:, cols], preferred_element_type=F32).astype(BF16)
    for c in range(P2_W // P2_CHUNK):
        cols = slice(c * P2_CHUNK, (c + 1) * P2_CHUNK)
        y = jnp.dot(h, w2_ref[:, cols], preferred_element_type=F32)
        ms = jnp.dot((y * y).astype(BF16), gm_ref[...], preferred_element_type=F32)
        p2_ref[:, cols] = (y * lax.rsqrt(ms + EPS) * g2_ref[:, cols]).astype(BF16)
    for start, width in P3_CHUNKS:
        cols = slice(start, start + width)
        yt = jnp.dot(h, w3_ref[:, cols], preferred_element_type=F32).T
        for r in range(ROW_TILE // ATT_TILE):
            p3_ref[r, cols, :] = yt[:, r * ATT_TILE:(r + 1) * ATT_TILE].astype(BF16)


def _projection(xt, norm_gain, w1, w2, g2, gmat, w3):
    t, d = xt.shape
    const = lambda i: (0, 0)
    return pl.pallas_call(
        _proj_kernel,
        name="mixer_proj",
        grid=(t // ROW_TILE,),
        in_specs=[
            pl.BlockSpec((ROW_TILE, d), lambda i: (i, 0)),
            pl.BlockSpec((1, d), const),
            pl.BlockSpec((d, P1_W), const, pipeline_mode=pl.Buffered(1)),
            pl.BlockSpec((d, P2_W), const, pipeline_mode=pl.Buffered(1)),
            pl.BlockSpec((1, P2_W), const),
            pl.BlockSpec((P2_CHUNK, P2_CHUNK), const),
            pl.BlockSpec((d, P3_W), const, pipeline_mode=pl.Buffered(1)),
        ],
        out_specs=[
            pl.BlockSpec((ROW_TILE, P1_W), lambda i: (i, 0)),
            pl.BlockSpec((ROW_TILE, P2_W), lambda i: (i, 0)),
            pl.BlockSpec((ROW_TILE // ATT_TILE, P3_W, ATT_TILE), lambda i: (i, 0, 0)),
        ],
        out_shape=[
            jax.ShapeDtypeStruct((t, P1_W), BF16),
            jax.ShapeDtypeStruct((t, P2_W), BF16),
            jax.ShapeDtypeStruct((t // ATT_TILE, P3_W, ATT_TILE), BF16),
        ],
        compiler_params=_compiler_params(("parallel",), 52 * 1024 * 1024),
    )(xt, norm_gain.reshape(1, d), w1, w2, g2, gmat, w3)


def _t5_bucket(dist):
    n = jnp.maximum(dist, 0)
    max_exact = N_BUCKETS // 2
    nf = jnp.maximum(n, 1).astype(F32)
    large = max_exact + (jnp.log(nf / max_exact) / math.log(MAX_DISTANCE / max_exact)
                         * (N_BUCKETS - max_exact)).astype(jnp.int32)
    large = jnp.minimum(large, N_BUCKETS - 1)
    return jnp.where(n < max_exact, n, large)


def _bias_lookup(bucket, tab_ref, head):
    val = jnp.zeros(bucket.shape, F32)
    for b in range(N_BUCKETS):
        val = jnp.where(bucket == b, tab_ref[b, head], val)
    return val


def _bias_kernel(tab_ref, dbias_ref, sbias_ref):
    kk = lax.broadcasted_iota(jnp.int32, (ATT_TILE, ATT_TILE), 0)
    qq = lax.broadcasted_iota(jnp.int32, (ATT_TILE, ATT_TILE), 1)
    for v in range(2):
        dist = qq - kk + v * ATT_TILE
        bucket = _t5_bucket(dist)
        for h in range(DIFF_HEADS):
            val = (_bias_lookup(bucket, tab_ref, h) - tab_ref[N_BUCKETS - 1, h]) * LOG2E
            dbias_ref[h, v] = jnp.where(dist >= 0, val, -jnp.inf)
        for h in range(SWA_Q_HEADS):
            val = _bias_lookup(bucket, tab_ref, DIFF_HEADS + h) * LOG2E
            sbias_ref[h, v] = jnp.where(dist >= 0, jnp.where(dist < WINDOW, val, -jnp.inf), -jnp.inf)


def _bias_tiles(rel_bias):
    return pl.pallas_call(
        _bias_kernel,
        name="bias_tiles",
        in_specs=[pl.BlockSpec(memory_space=pltpu.SMEM)],
        out_specs=[pl.BlockSpec(memory_space=pltpu.VMEM), pl.BlockSpec(memory_space=pltpu.VMEM)],
        out_shape=[
            jax.ShapeDtypeStruct((DIFF_HEADS, 2, ATT_TILE, ATT_TILE), F32),
            jax.ShapeDtypeStruct((SWA_Q_HEADS, 2, ATT_TILE, ATT_TILE), F32),
        ],
    )(rel_bias)


def _split_pairs(q, n_pairs):
    heads = []
    for p in range(n_pairs):
        qf = q[:, p * PAIR_W:(p + 1) * PAIR_W].astype(F32)
        lane = lax.broadcasted_iota(jnp.int32, qf.shape, 1)
        zero = jnp.zeros_like(qf)
        heads.append(jnp.where(lane < HEAD_DIM, qf, zero).astype(BF16))
        heads.append(jnp.where(lane >= HEAD_DIM, qf, zero).astype(BF16))
    return heads


def _neg_abs(x):
    bits = lax.bitcast_convert_type(x, jnp.uint32) | jnp.uint32(0x80000000)
    return lax.bitcast_convert_type(bits, F32)


def _interleave(chains):
    results = [None] * len(chains)
    live = []
    started = 0
    while started < len(chains) or live:
        if started < len(chains):
            live.append(started)
            started += 1
        for n in reversed(list(live)):
            try:
                next(chains[n])
            except StopIteration as done:
                results[n] = done.value
                live.remove(n)
    return results


def _key_tile(ref, j):
    return ref[0, pl.ds(pl.multiple_of(j * ATT_TILE, ATT_TILE), ATT_TILE), :]


def _sb_kernel(q_ref, k_ref, vt_ref, mtri_ref, o_ref, acc_ref):
    i = pl.program_id(1)
    q_heads = _split_pairs(q_ref[0], SB_HEADS // 2)
    mtri = mtri_ref[...]
    kk = lax.broadcasted_iota(jnp.int32, (ATT_TILE, ATT_TILE), 0)
    qq = lax.broadcasted_iota(jnp.int32, (ATT_TILE, ATT_TILE), 1)
    strictly_before = kk < qq

    def tile(j, carries, keep):
        k = _key_tile(k_ref, j)
        vt = vt_ref[j]

        def chain(h):
            k_pair = k[:, (h // 2) * PAIR_W:(h // 2 + 1) * PAIR_W]
            rows = slice(h * HEAD_DIM, (h + 1) * HEAD_DIM)
            z = lax.dot_general(k_pair, q_heads[h], _NT, preferred_element_type=F32)
            yield
            sp = jnp.maximum(z, 0.0) + jnp.log(1.0 + jnp.exp2(_neg_abs(z))) * LOG2E
            if keep is not None:
                sp = jnp.where(keep, sp, 0.0)
            carry = carries[h] - jnp.sum(sp, axis=0, keepdims=True)
            sp_b = sp.astype(BF16)
            yield
            between = jnp.dot(mtri, sp_b, preferred_element_type=F32)
            yield
            w = jnp.exp2((z - sp) + (between + carries[h]))
            if keep is not None:
                w = jnp.where(keep, w, 0.0)
            w = w.astype(BF16)
            yield
            acc_ref[rows, :] += jnp.dot(vt[rows], w, preferred_element_type=F32)
            return carry

        return tuple(_interleave([chain(h) for h in range(SB_HEADS)]))

    acc_ref[...] = jnp.zeros_like(acc_ref)
    zero = jnp.zeros((1, ATT_TILE), F32)
    carries = tile(i, (zero,) * SB_HEADS, strictly_before)
    lax.fori_loop(0, i, lambda jj, c: tile(i - 1 - jj, c, None), carries)
    o_ref[0] = acc_ref[...].T.astype(BF16)


def _sb_attention(p1, p3, mtri, batch, seq):
    nq = seq // ATT_TILE
    p1 = p1.reshape(batch, seq, P1_W)
    return pl.pallas_call(
        _sb_kernel,
        name="sb_attn",
        grid=(batch, nq),
        in_specs=[
            pl.BlockSpec((1, ATT_TILE, SB_W), lambda b, i: (b, i, 0)),
            pl.BlockSpec((1, seq, SB_W), lambda b, i: (b, 0, 1)),
            pl.BlockSpec((nq, SB_W, ATT_TILE), lambda b, i: (b, 0, 0)),
            pl.BlockSpec((ATT_TILE, ATT_TILE), lambda b, i: (0, 0)),
        ],
        out_specs=pl.BlockSpec((1, ATT_TILE, SB_W), lambda b, i: (b, i, 0)),
        out_shape=jax.ShapeDtypeStruct((batch, seq, SB_W), BF16),
        scratch_shapes=[pltpu.VMEM((SB_W, ATT_TILE), F32)],
        compiler_params=_compiler_params(("parallel", "arbitrary"), 40 * 1024 * 1024),
    )(p1, p1, p3, mtri)


def _diff_kernel(q_ref, k_ref, vt_ref, bias_ref, lam_ref, gain_ref, o_ref, acc_ref, *, lam_init):
    i = pl.program_id(1)
    q_comps = _split_pairs(q_ref[0], DIFF_HEADS)

    def tile(j, state, variant):
        k = _key_tile(k_ref, j)
        vt = vt_ref[j]

        def chain(n):
            h = n // 2
            m_prev, l_prev = state[n]
            s = lax.dot_general(k[:, h * PAIR_W:(h + 1) * PAIR_W], q_comps[n], _NT,
                                preferred_element_type=F32)
            yield
            if variant is not None:
                s = s + bias_ref[h, variant]
            m_new = jnp.maximum(m_prev, jnp.max(s, axis=0, keepdims=True))
            alpha = jnp.exp2(m_prev - m_new)
            p = jnp.exp2(s - m_new)
            l_new = alpha * l_prev + jnp.sum(p, axis=0, keepdims=True)
            p = p.astype(BF16)
            yield
            acc_ref[n] = alpha * acc_ref[n] + jnp.dot(vt[h * PAIR_W:(h + 1) * PAIR_W], p,
                                                      preferred_element_type=F32)
            return m_new, l_new

        return tuple(_interleave([chain(n) for n in range(2 * DIFF_HEADS)]))

    acc_ref[...] = jnp.zeros_like(acc_ref)
    init = (jnp.full((1, ATT_TILE), -jnp.inf, F32), jnp.zeros((1, ATT_TILE), F32))
    state = lax.fori_loop(0, jnp.minimum(i + 1, 2), lambda jj, st: tile(i - jj, st, jj),
                          (init,) * (2 * DIFF_HEADS))
    state = lax.fori_loop(2, i + 1, lambda jj, st: tile(i - jj, st, None), state)

    lv = lam_ref[...]
    lam = (jnp.exp(jnp.sum(lv[0:1] * lv[1:2], axis=-1, keepdims=True))
           - jnp.exp(jnp.sum(lv[2:3] * lv[3:4], axis=-1, keepdims=True)) + lam_init)
    outs = []
    for h in range(DIFF_HEADS):
        (_, l1), (_, l2) = state[2 * h], state[2 * h + 1]
        o = acc_ref[2 * h] / l1 - lam * (acc_ref[2 * h + 1] / l2)
        ms = jnp.mean(o * o, axis=0, keepdims=True)
        outs.append(o * lax.rsqrt(ms + EPS) * gain_ref[...] * (1.0 - lam_init))
    o_ref[0] = jnp.concatenate(outs, axis=0).T.astype(BF16)


def _diff_attention(p2, p3, dbias, lam_vecs, sub_gain, lam_init, batch, seq):
    nq = seq // ATT_TILE
    p2 = p2.reshape(batch, seq, P2_W)
    return pl.pallas_call(
        functools.partial(_diff_kernel, lam_init=lam_init),
        name="diff_attn",
        grid=(batch, nq),
        in_specs=[
            pl.BlockSpec((1, ATT_TILE, DIFF_W), lambda b, i: (b, i, 0)),
            pl.BlockSpec((1, seq, DIFF_W), lambda b, i: (b, 0, 1)),
            pl.BlockSpec((nq, DIFF_W, ATT_TILE), lambda b, i: (b, SB_W // DIFF_W, 0)),
            pl.BlockSpec((DIFF_HEADS, 2, ATT_TILE, ATT_TILE), lambda b, i: (0, 0, 0, 0)),
            pl.BlockSpec((4, HEAD_DIM), lambda b, i: (0, 0)),
            pl.BlockSpec((PAIR_W, 1), lambda b, i: (0, 0)),
        ],
        out_specs=pl.BlockSpec((1, ATT_TILE, DIFF_W), lambda b, i: (b, i, 0)),
        out_shape=jax.ShapeDtypeStruct((batch, seq, DIFF_W), BF16),
        scratch_shapes=[pltpu.VMEM((2 * DIFF_HEADS, PAIR_W, ATT_TILE), F32)],
        compiler_params=_compiler_params(("parallel", "arbitrary"), 40 * 1024 * 1024),
    )(p2, p2, p3, dbias, lam_vecs, sub_gain.reshape(PAIR_W, 1))


def _swa_kernel(sink_ref, q_ref, k_ref, vt_ref, bias_ref, o_ref):
    i = pl.program_id(1)
    prev = jnp.maximum(i - 1, 0)
    prev_mask = jnp.where(i == 0, -jnp.inf, 0.0).astype(F32)
    q_heads = _split_pairs(q_ref[0], SWA_Q_HEADS // 2)
    k_win = jnp.concatenate([_key_tile(k_ref, prev), _key_tile(k_ref, i)], axis=0)
    vt_win = jnp.concatenate([vt_ref[prev], vt_ref[i]], axis=1)

    def chain(h):
        kv = h // SWA_GROUP
        k_pair = k_win[:, kv * PAIR_W:(kv + 1) * PAIR_W]
        s = lax.dot_general(k_pair, q_heads[h], _NT, preferred_element_type=F32)
        yield
        sink = sink_ref[h] * LOG2E
        s = s + jnp.concatenate([bias_ref[h, 1] + prev_mask, bias_ref[h, 0]], axis=0)
        m = jnp.maximum(jnp.max(s, axis=0, keepdims=True), sink)
        e = jnp.exp2(s - m)
        denom = jnp.sum(e, axis=0, keepdims=True) + jnp.exp2(sink - m)
        e = e.astype(BF16)
        yield
        vt_h = vt_win[kv * HEAD_DIM:(kv + 1) * HEAD_DIM]
        return jnp.dot(vt_h, e, preferred_element_type=F32) / denom

    outs = _interleave([chain(h) for h in range(SWA_Q_HEADS)])
    o_ref[0] = jnp.concatenate(outs, axis=0).T.astype(BF16)


def _swa_attention(sinks, p2, p3, sbias, batch, seq):
    nq = seq // ATT_TILE
    p2 = p2.reshape(batch, seq, P2_W)
    kdup_w = 2 * SWA_KV_W
    return pl.pallas_call(
        _swa_kernel,
        name="swa_attn",
        grid=(batch, nq),
        in_specs=[
            pl.BlockSpec(memory_space=pltpu.SMEM),
            pl.BlockSpec((1, ATT_TILE, SWA_Q_W), lambda b, i: (b, i, 2 * DIFF_W // SWA_Q_W)),
            pl.BlockSpec((1, seq, kdup_w), lambda b, i: (b, 0, (2 * DIFF_W + SWA_Q_W) // kdup_w)),
            pl.BlockSpec((nq, SWA_KV_W, ATT_TILE), lambda b, i: (b, (SB_W + DIFF_W) // SWA_KV_W, 0)),
            pl.BlockSpec((SWA_Q_HEADS, 2, ATT_TILE, ATT_TILE), lambda b, i: (0, 0, 0, 0)),
        ],
        out_specs=pl.BlockSpec((1, ATT_TILE, SWA_Q_W), lambda b, i: (b, i, 0)),
        out_shape=jax.ShapeDtypeStruct((batch, seq, SWA_Q_W), BF16),
        compiler_params=_compiler_params(("parallel", "arbitrary"), 40 * 1024 * 1024),
    )(sinks, p2, p2, p3, sbias)


def _merge_kernel(x_ref, osb_ref, odf_ref, osw_ref, ga_ref, gd_ref, gs_ref,
                  wsb_ref, wdf_ref, wsw_ref, wo_ref, o_ref):
    def branch(o_ref_, w_ref, g_ref):
        gate = 1.0 / (1.0 + jnp.exp(-g_ref[...].astype(F32)))
        return gate * jnp.dot(o_ref_[...], w_ref[...], preferred_element_type=F32)

    merged = (branch(osb_ref, wsb_ref, ga_ref) + branch(odf_ref, wdf_ref, gd_ref)
              + branch(osw_ref, wsw_ref, gs_ref))
    o_ref[...] = x_ref[...] + jnp.dot(merged.astype(BF16), wo_ref[...], preferred_element_type=F32)


def _merge(xt, o_sb, o_diff, o_swa, p1, w_sb, w_diff, w_swa, w_out):
    t, d = xt.shape
    gcol = 2 * SB_W // d
    row = lambda i: (i, 0)
    const = lambda i: (0, 0)
    return pl.pallas_call(
        _merge_kernel,
        name="merge",
        grid=(t // ROW_TILE,),
        in_specs=[
            pl.BlockSpec((ROW_TILE, d), row),
            pl.BlockSpec((ROW_TILE, SB_W), row),
            pl.BlockSpec((ROW_TILE, DIFF_W), row),
            pl.BlockSpec((ROW_TILE, SWA_Q_W), row),
            pl.BlockSpec((ROW_TILE, d), lambda i: (i, gcol)),
            pl.BlockSpec((ROW_TILE, d), lambda i: (i, gcol + 1)),
            pl.BlockSpec((ROW_TILE, d), lambda i: (i, gcol + 2)),
            pl.BlockSpec((SB_W, d), const),
            pl.BlockSpec((DIFF_W, d), const),
            pl.BlockSpec((SWA_Q_W, d), const),
            pl.BlockSpec((d, d), const),
        ],
        out_specs=pl.BlockSpec((ROW_TILE, d), row),
        out_shape=jax.ShapeDtypeStruct((t, d), F32),
        compiler_params=_compiler_params(("parallel",), 48 * 1024 * 1024),
    )(xt, o_sb.reshape(t, SB_W), o_diff.reshape(t, DIFF_W), o_swa.reshape(t, SWA_Q_W),
      p1, p1, p1, w_sb.astype(BF16), w_diff.astype(BF16), w_swa.astype(BF16), w_out.astype(BF16))


def _projection_weights(w_in, q_norm_diff, k_norm_diff, q_norm_swa, k_norm_swa):
    widths = (SB_W, SB_W, SB_W, DIFF_W, DIFF_W, DIFF_W, SWA_Q_W, SWA_KV_W, SWA_KV_W)
    offs = [0]
    for w in widths:
        offs.append(offs[-1] + w)
    qa, ka, va, qd, kd, vd, qs, ks, vs = (w_in[:, offs[n]:offs[n + 1]] for n in range(len(widths)))
    gates = w_in[:, offs[-1]:]
    qscale = SCALE * LOG2E

    def dup_heads(w):
        return jnp.concatenate([w[:, h * HEAD_DIM:(h + 1) * HEAD_DIM]
                                for h in range(SWA_KV_HEADS) for _ in range(2)], axis=1)

    w1 = jnp.concatenate([qa * qscale, ka, gates], axis=1).astype(BF16)
    w2 = jnp.concatenate([qd, kd, qs, dup_heads(ks)], axis=1).astype(BF16)
    w3 = jnp.concatenate([va, vd, vs], axis=1).astype(BF16)
    g2 = jnp.concatenate([
        jnp.tile(q_norm_diff * qscale, DIFF_W // HEAD_DIM), jnp.tile(k_norm_diff, DIFF_W // HEAD_DIM),
        jnp.tile(q_norm_swa * qscale, SWA_Q_W // HEAD_DIM), jnp.tile(k_norm_swa, 2 * SWA_KV_W // HEAD_DIM),
    ]).reshape(1, P2_W).astype(F32)
    return w1, w2, g2, w3


def _group_mean_matrix():
    g = jnp.arange(P2_CHUNK) // HEAD_DIM
    return jnp.where(g[:, None] == g[None, :], 1.0 / HEAD_DIM, 0.0).astype(BF16)


def _neg_suffix_matrix():
    idx = jnp.arange(ATT_TILE)
    return jnp.where(idx[None, :] > idx[:, None], -1.0, 0.0).astype(BF16)


def kernel(x, ffn1_norm, ffn1_w_in, ffn1_w_out, mix_norm, w_in, q_norm_diff, k_norm_diff, q_norm_swa, k_norm_swa, diff_lambda, diff_subln, swa_sinks, rel_bias, w_proj_sb, w_proj_diff, w_proj_swa, w_out, ffn2_norm, ffn2_w_in, ffn2_w_out):
    batch, seq, d = x.shape
    depth = ffn1_norm.shape[0]
    assert seq % ATT_TILE == 0 and (batch * seq) % ROW_TILE == 0
    xt = x.reshape(batch * seq, d)
    dbias, sbias = _bias_tiles(rel_bias)
    gmat = _group_mean_matrix()
    mtri = _neg_suffix_matrix()
    for l in range(depth):
        lam_init = 0.8 - 0.6 * math.exp(-0.3 * l)
        xt = _ffn(xt, ffn1_norm[l], ffn1_w_in[l], ffn1_w_out[l])
        w1, w2, g2, w3 = _projection_weights(w_in[l], q_norm_diff[l], k_norm_diff[l],
                                             q_norm_swa[l], k_norm_swa[l])
        p1, p2, p3 = _projection(xt, mix_norm[l], w1, w2, g2, gmat, w3)
        o_sb = _sb_attention(p1, p3, mtri, batch, seq)
        o_diff = _diff_attention(p2, p3, dbias, diff_lambda[l], diff_subln[l], lam_init, batch, seq)
        o_swa = _swa_attention(swa_sinks[l], p2, p3, sbias, batch, seq)
        xt = _merge(xt, o_sb, o_diff, o_swa, p1, w_proj_sb[l], w_proj_diff[l], w_proj_swa[l], w_out[l])
        xt = _ffn(xt, ffn2_norm[l], ffn2_w_in[l], ffn2_w_out[l])
    return xt.reshape(batch, seq, d)
```

```python
import functools
import math

import jax
import jax.numpy as jnp
from jax import lax
from jax.experimental import pallas as pl
from jax.experimental.pallas import tpu as pltpu

F32 = jnp.float32
BF16 = jnp.bfloat16

HEAD_DIM = 64
SB_HEADS = 8
DIFF_HEADS = 4
SWA_Q_HEADS = 8
SWA_KV_HEADS = 2
SWA_GROUP = SWA_Q_HEADS // SWA_KV_HEADS
WINDOW = 128
N_BUCKETS = 32
MAX_DISTANCE = 128
EPS = 1e-6
SCALE = HEAD_DIM ** -0.5
LOG2E = math.log2(math.e)
SB_DEAD_LOG2 = -160.0
SWA_LEAD = 4

SB_W = SB_HEADS * HEAD_DIM
DIFF_W = DIFF_HEADS * 2 * HEAD_DIM
SWA_Q_W = SWA_Q_HEADS * HEAD_DIM
SWA_KV_W = SWA_KV_HEADS * HEAD_DIM

V7X_LANES = 128
V7X_MXU_DIM = 256
V7X_VMEM_BYTES = 64 * 1024 * 1024

ATT_TILE = V7X_MXU_DIM
ROW_TILE = 512
PAIR_W = 2 * HEAD_DIM
assert PAIR_W == V7X_LANES and WINDOW <= ATT_TILE

P1_W = 2 * SB_W + 3 * 1024
P2_W = 2 * DIFF_W + SWA_Q_W + 2 * SWA_KV_W
P3_W = SB_W + DIFF_W + SWA_KV_W
P1_CHUNK = 512
P2_CHUNK = 256
P3_CHUNKS = ((0, 256), (256, 256), (512, 256), (768, 256), (1024, 128))
assert sum(w for _, w in P3_CHUNKS) == P3_W

_NT = (((1,), (1,)), ((), ()))


def _compiler_params(semantics, vmem_bytes):
    assert vmem_bytes < V7X_VMEM_BYTES
    return pltpu.CompilerParams(dimension_semantics=semantics, vmem_limit_bytes=vmem_bytes)


def _rmsnorm_rows(x, g):
    ms = jnp.mean(x * x, axis=-1, keepdims=True)
    return x * lax.rsqrt(ms + EPS) * g


def _ffn_kernel(x_ref, g_ref, wg_ref, wu_ref, wo_ref, o_ref):
    x = x_ref[...]
    h = _rmsnorm_rows(x, g_ref[...]).astype(BF16)
    g = jnp.dot(h, wg_ref[...], preferred_element_type=F32)
    u = jnp.dot(h, wu_ref[...], preferred_element_type=F32)
    a = (g * (1.0 / (1.0 + jnp.exp(-g))) * u).astype(BF16)
    o_ref[...] = x + 0.5 * jnp.dot(a, wo_ref[...], preferred_element_type=F32)


def _ffn(xt, gain, w_in, w_out):
    t, d = xt.shape
    d_ff = w_out.shape[0]
    assert d_ff % V7X_MXU_DIM == 0 and t % ROW_TILE == 0
    w_in = w_in.astype(BF16)
    w_out = w_out.astype(BF16)
    resident = pl.Buffered(1)
    return pl.pallas_call(
        _ffn_kernel,
        name="ffn",
        grid=(t // ROW_TILE,),
        in_specs=[
            pl.BlockSpec((ROW_TILE, d), lambda i: (i, 0)),
            pl.BlockSpec((1, d), lambda i: (0, 0)),
            pl.BlockSpec((d, d_ff), lambda i: (0, 0), pipeline_mode=resident),
            pl.BlockSpec((d, d_ff), lambda i: (0, 1), pipeline_mode=resident),
            pl.BlockSpec((d_ff, d), lambda i: (0, 0), pipeline_mode=resident),
        ],
        out_specs=pl.BlockSpec((ROW_TILE, d), lambda i: (i, 0)),
        out_shape=jax.ShapeDtypeStruct((t, d), F32),
        compiler_params=_compiler_params(("parallel",), 56 * 1024 * 1024),
    )(xt, gain.reshape(1, d), w_in, w_in, w_out)


def _proj_kernel(x_ref, ng_ref, w1_ref, w2_ref, g2_ref, gm_ref, w3_ref,
                 p1_ref, p2_ref, p3_ref):
    h = _rmsnorm_rows(x_ref[...], ng_ref[...]).astype(BF16)
    for c in range(P1_W // P1_CHUNK):
        cols = slice(c * P1_CHUNK, (c + 1) * P1_CHUNK)
        p1_ref[:, cols] = jnp.dot(h, w1_ref[:, cols], preferred_element_type=F32).astype(BF16)
    y_all = jnp.dot(h, w2_ref[...], preferred_element_type=F32)
    for c in range(P2_W // P2_CHUNK):
        cols = slice(c * P2_CHUNK, (c + 1) * P2_CHUNK)
        y = y_all[:, cols]
        ms = jnp.dot((y * y).astype(BF16), gm_ref[...], preferred_element_type=F32)
        p2_ref[:, cols] = (y * lax.rsqrt(ms + EPS) * g2_ref[:, cols]).astype(BF16)
    for start, width in P3_CHUNKS:
        cols = slice(start, start + width)
        yt = jnp.dot(h, w3_ref[:, cols], preferred_element_type=F32).T
        for r in range(ROW_TILE // ATT_TILE):
            p3_ref[r, cols, :] = yt[:, r * ATT_TILE:(r + 1) * ATT_TILE].astype(BF16)


def _projection(xt, norm_gain, w1, w2, g2, gmat, w3):
    t, d = xt.shape
    const = lambda i: (0, 0)
    return pl.pallas_call(
        _proj_kernel,
        name="mixer_proj",
        grid=(t // ROW_TILE,),
        in_specs=[
            pl.BlockSpec((ROW_TILE, d), lambda i: (i, 0)),
            pl.BlockSpec((1, d), const),
            pl.BlockSpec((d, P1_W), const, pipeline_mode=pl.Buffered(1)),
            pl.BlockSpec((d, P2_W), const, pipeline_mode=pl.Buffered(1)),
            pl.BlockSpec((1, P2_W), const),
            pl.BlockSpec((P2_CHUNK, P2_CHUNK), const),
            pl.BlockSpec((d, P3_W), const, pipeline_mode=pl.Buffered(1)),
        ],
        out_specs=[
            pl.BlockSpec((ROW_TILE, P1_W), lambda i: (i, 0)),
            pl.BlockSpec((ROW_TILE, P2_W), lambda i: (i, 0)),
            pl.BlockSpec((ROW_TILE // ATT_TILE, P3_W, ATT_TILE), lambda i: (i, 0, 0)),
        ],
        out_shape=[
            jax.ShapeDtypeStruct((t, P1_W), BF16),
            jax.ShapeDtypeStruct((t, P2_W), BF16),
            jax.ShapeDtypeStruct((t // ATT_TILE, P3_W, ATT_TILE), BF16),
        ],
        compiler_params=_compiler_params(("parallel",), 52 * 1024 * 1024),
    )(xt, norm_gain.reshape(1, d), w1, w2, g2, gmat, w3)


def _t5_bucket(dist):
    n = jnp.maximum(dist, 0)
    max_exact = N_BUCKETS // 2
    nf = jnp.maximum(n, 1).astype(F32)
    large = max_exact + (jnp.log(nf / max_exact) / math.log(MAX_DISTANCE / max_exact)
                         * (N_BUCKETS - max_exact)).astype(jnp.int32)
    large = jnp.minimum(large, N_BUCKETS - 1)
    return jnp.where(n < max_exact, n, large)


def _bias_lookup(bucket, tab_ref, head):
    val = jnp.zeros(bucket.shape, F32)
    for b in range(N_BUCKETS):
        val = jnp.where(bucket == b, tab_ref[b, head], val)
    return val


def _bias_kernel(tab_ref, dbias_ref, sbias_ref):
    kk = lax.broadcasted_iota(jnp.int32, (ATT_TILE, ATT_TILE), 0)
    qq = lax.broadcasted_iota(jnp.int32, (ATT_TILE, ATT_TILE), 1)
    for v in range(2):
        dist = qq - kk + v * ATT_TILE
        bucket = _t5_bucket(dist)
        for h in range(DIFF_HEADS):
            val = (_bias_lookup(bucket, tab_ref, h) - tab_ref[N_BUCKETS - 1, h]) * LOG2E
            dbias_ref[h, v] = jnp.where(dist >= 0, val, -jnp.inf)
    kk = lax.broadcasted_iota(jnp.int32, (2 * WINDOW, WINDOW), 0)
    qq = lax.broadcasted_iota(jnp.int32, (2 * WINDOW, WINDOW), 1)
    for v in range(2):
        dist = qq - kk + (1 - v) * WINDOW
        bucket = _t5_bucket(dist)
        for h in range(SWA_Q_HEADS):
            val = _bias_lookup(bucket, tab_ref, DIFF_HEADS + h) * LOG2E
            sbias_ref[h, v] = jnp.where(dist >= 0, jnp.where(dist < WINDOW, val, -jnp.inf), -jnp.inf)


def _bias_tiles(rel_bias):
    return pl.pallas_call(
        _bias_kernel,
        name="bias_tiles",
        in_specs=[pl.BlockSpec(memory_space=pltpu.SMEM)],
        out_specs=[pl.BlockSpec(memory_space=pltpu.VMEM), pl.BlockSpec(memory_space=pltpu.VMEM)],
        out_shape=[
            jax.ShapeDtypeStruct((DIFF_HEADS, 2, ATT_TILE, ATT_TILE), F32),
            jax.ShapeDtypeStruct((SWA_Q_HEADS, 2, 2 * WINDOW, WINDOW), F32),
        ],
    )(rel_bias)


def _split_pairs(q, n_pairs):
    heads = []
    for p in range(n_pairs):
        qf = q[:, p * PAIR_W:(p + 1) * PAIR_W].astype(F32)
        lane = lax.broadcasted_iota(jnp.int32, qf.shape, 1)
        zero = jnp.zeros_like(qf)
        heads.append(jnp.where(lane < HEAD_DIM, qf, zero).astype(BF16))
        heads.append(jnp.where(lane >= HEAD_DIM, qf, zero).astype(BF16))
    return heads


def _neg_abs(x):
    bits = lax.bitcast_convert_type(x, jnp.uint32) | jnp.uint32(0x80000000)
    return lax.bitcast_convert_type(bits, F32)


def _interleave(chains):
    results = [None] * len(chains)
    live = []
    started = 0
    while started < len(chains) or live:
        if started < len(chains):
            live.append(started)
            started += 1
        for n in reversed(list(live)):
            try:
                next(chains[n])
            except StopIteration as done:
                results[n] = done.value
                live.remove(n)
    return results


def _key_tile(ref, j):
    return ref[0, pl.ds(pl.multiple_of(j * ATT_TILE, ATT_TILE), ATT_TILE), :]


def _sb_kernel(q_ref, k_ref, vt_ref, mtri_ref, o_ref, acc_ref):
    i = pl.program_id(1)
    q_heads = _split_pairs(q_ref[0], SB_HEADS // 2)
    mtri = mtri_ref[...]
    kk = lax.broadcasted_iota(jnp.int32, (ATT_TILE, ATT_TILE), 0)
    qq = lax.broadcasted_iota(jnp.int32, (ATT_TILE, ATT_TILE), 1)
    strictly_before = kk < qq

    def run(tiles, carries):
        carry = list(carries)

        def chain(k, vt, keep, h):
            k_pair = k[:, (h // 2) * PAIR_W:(h // 2 + 1) * PAIR_W]
            rows = slice(h * HEAD_DIM, (h + 1) * HEAD_DIM)
            z = lax.dot_general(k_pair, q_heads[h], _NT, preferred_element_type=F32)
            yield
            sp = jnp.maximum(z, 0.0) + jnp.log(1.0 + jnp.exp2(_neg_abs(z))) * LOG2E
            if keep is not None:
                sp = jnp.where(keep, sp, 0.0)
            after = carry[h]
            carry[h] = after - jnp.sum(sp, axis=0, keepdims=True)
            sp_b = sp.astype(BF16)
            yield
            between = jnp.dot(mtri, sp_b, preferred_element_type=F32)
            yield
            w = jnp.exp2((z - sp) + (between + after))
            if keep is not None:
                w = jnp.where(keep, w, 0.0)
            w = w.astype(BF16)
            yield
            acc_ref[rows, :] += jnp.dot(vt[rows], w, preferred_element_type=F32)

        chains = []
        for j, keep in tiles:
            k = _key_tile(k_ref, j)
            vt = vt_ref[j]
            chains.extend(chain(k, vt, keep, h) for h in range(SB_HEADS))
        _interleave(chains)
        return tuple(carry)

    acc_ref[...] = jnp.zeros_like(acc_ref)
    zeros = (jnp.zeros((1, ATT_TILE), F32),) * SB_HEADS
    carries = lax.cond(
        i >= 1,
        lambda: run([(i, strictly_before), (i - 1, None)], zeros),
        lambda: run([(i, strictly_before)], zeros))

    def live(c):
        j, carries = c
        return jnp.logical_and(j >= 0, jnp.max(jnp.concatenate(carries, axis=0)) > SB_DEAD_LOG2)

    lax.while_loop(live, lambda c: (c[0] - 1, run([(c[0], None)], c[1])), (i - 2, carries))
    o_ref[0] = acc_ref[...].T.astype(BF16)


def _sb_attention(p1, p3, mtri, batch, seq):
    nq = seq // ATT_TILE
    p1 = p1.reshape(batch, seq, P1_W)
    return pl.pallas_call(
        _sb_kernel,
        name="sb_attn",
        grid=(batch, nq),
        in_specs=[
            pl.BlockSpec((1, ATT_TILE, SB_W), lambda b, i: (b, i, 0)),
            pl.BlockSpec((1, seq, SB_W), lambda b, i: (b, 0, 1)),
            pl.BlockSpec((nq, SB_W, ATT_TILE), lambda b, i: (b, 0, 0)),
            pl.BlockSpec((ATT_TILE, ATT_TILE), lambda b, i: (0, 0)),
        ],
        out_specs=pl.BlockSpec((1, ATT_TILE, SB_W), lambda b, i: (b, i, 0)),
        out_shape=jax.ShapeDtypeStruct((batch, seq, SB_W), BF16),
        scratch_shapes=[pltpu.VMEM((SB_W, ATT_TILE), F32)],
        compiler_params=_compiler_params(("parallel", "arbitrary"), 40 * 1024 * 1024),
    )(p1, p1, p3, mtri)


def _diff_kernel(q_ref, k_ref, vt_ref, bias_ref, lam_ref, gain_ref, o_ref, acc_ref, *, lam_init):
    i = pl.program_id(1)
    q_comps = _split_pairs(q_ref[0], DIFF_HEADS)

    def run(tiles, state):
        m_run = [s[0] for s in state]
        l_run = [s[1] for s in state]

        def chain(k, vt, variant, n):
            h = n // 2
            s = lax.dot_general(k[:, h * PAIR_W:(h + 1) * PAIR_W], q_comps[n], _NT,
                                preferred_element_type=F32)
            yield
            yield
            if variant is not None:
                s = s + bias_ref[h, variant]
            m_prev = m_run[n]
            m_new = jnp.maximum(m_prev, jnp.max(s, axis=0, keepdims=True))
            alpha = jnp.exp2(m_prev - m_new)
            m_run[n] = m_new
            yield
            p = jnp.exp2(s - m_new)
            l_run[n] = alpha * l_run[n] + jnp.sum(p, axis=0, keepdims=True)
            p = p.astype(BF16)
            yield
            pv = jnp.dot(vt[h * PAIR_W:(h + 1) * PAIR_W], p, preferred_element_type=F32)
            yield
            acc_ref[n] = alpha * acc_ref[n] + pv

        chains = []
        for j, variant in tiles:
            k = _key_tile(k_ref, j)
            vt = vt_ref[j]
            chains.extend(chain(k, vt, variant, n) for n in range(2 * DIFF_HEADS))
        _interleave(chains)
        return tuple(zip(m_run, l_run))

    acc_ref[...] = jnp.zeros_like(acc_ref)
    init = ((jnp.full((1, ATT_TILE), -jnp.inf, F32), jnp.zeros((1, ATT_TILE), F32)),) * (2 * DIFF_HEADS)
    state = lax.cond(i >= 1,
                     lambda: run([(i, 0), (i - 1, 1)], init),
                     lambda: run([(i, 0)], init))
    n_far = jnp.maximum(i - 1, 0)
    state = lax.fori_loop(
        0, lax.shift_right_logical(n_far, 1),
        lambda t, st: run([(i - 2 - 2 * t, None), (i - 3 - 2 * t, None)], st), state)
    state = lax.fori_loop(0, n_far & 1, lambda t, st: run([(0, None)], st), state)

    lv = lam_ref[...]
    lam = (jnp.exp(jnp.sum(lv[0:1] * lv[1:2], axis=-1, keepdims=True))
           - jnp.exp(jnp.sum(lv[2:3] * lv[3:4], axis=-1, keepdims=True)) + lam_init)
    outs = []
    for h in range(DIFF_HEADS):
        (_, l1), (_, l2) = state[2 * h], state[2 * h + 1]
        o = acc_ref[2 * h] / l1 - lam * (acc_ref[2 * h + 1] / l2)
        ms = jnp.mean(o * o, axis=0, keepdims=True)
        outs.append(o * lax.rsqrt(ms + EPS) * gain_ref[...] * (1.0 - lam_init))
    o_ref[0] = jnp.concatenate(outs, axis=0).T.astype(BF16)


def _diff_attention(p2, p3, dbias, lam_vecs, sub_gain, lam_init, batch, seq):
    nq = seq // ATT_TILE
    p2 = p2.reshape(batch, seq, P2_W)
    return pl.pallas_call(
        functools.partial(_diff_kernel, lam_init=lam_init),
        name="diff_attn",
        grid=(batch, nq),
        in_specs=[
            pl.BlockSpec((1, ATT_TILE, DIFF_W), lambda b, i: (b, i, 0)),
            pl.BlockSpec((1, seq, DIFF_W), lambda b, i: (b, 0, 1)),
            pl.BlockSpec((nq, DIFF_W, ATT_TILE), lambda b, i: (b, SB_W // DIFF_W, 0)),
            pl.BlockSpec((DIFF_HEADS, 2, ATT_TILE, ATT_TILE), lambda b, i: (0, 0, 0, 0)),
            pl.BlockSpec((4, HEAD_DIM), lambda b, i: (0, 0)),
            pl.BlockSpec((PAIR_W, 1), lambda b, i: (0, 0)),
        ],
        out_specs=pl.BlockSpec((1, ATT_TILE, DIFF_W), lambda b, i: (b, i, 0)),
        out_shape=jax.ShapeDtypeStruct((batch, seq, DIFF_W), BF16),
        scratch_shapes=[pltpu.VMEM((2 * DIFF_HEADS, PAIR_W, ATT_TILE), F32)],
        compiler_params=_compiler_params(("parallel", "arbitrary"), 40 * 1024 * 1024),
    )(p2, p2, p3, dbias, lam_vecs, sub_gain.reshape(PAIR_W, 1))


def _swa_kernel(sink_ref, q_ref, k_ref, vt_ref, bias_ref, o_ref):
    i = pl.program_id(1)
    first = i == 0
    prev = jnp.maximum(i - 1, 0)
    q = q_ref[0]
    k_hi = _key_tile(k_ref, i)
    vt_hi = vt_ref[i]
    lo_start = pl.multiple_of(jnp.maximum(i * ATT_TILE - WINDOW, 0), WINDOW)
    k_lo = k_ref[0, pl.ds(lo_start, 2 * WINDOW), :]
    vt_lo = jnp.where(first, vt_hi, jnp.concatenate([vt_ref[prev][:, WINDOW:], vt_hi[:, :WINDOW]], axis=1))
    lo_variant = jnp.where(first, 1, 0)
    halves = ((_split_pairs(q[:WINDOW], SWA_Q_HEADS // 2), k_lo, vt_lo, lo_variant),
              (_split_pairs(q[WINDOW:], SWA_Q_HEADS // 2), k_hi, vt_hi, 0))

    def chain(q_heads, k_win, vt_win, variant, h):
        kv = h // SWA_GROUP
        k_pair = k_win[:, kv * PAIR_W:(kv + 1) * PAIR_W]
        s = lax.dot_general(k_pair, q_heads[h], _NT, preferred_element_type=F32)
        for _ in range(SWA_LEAD):
            yield
        sink = sink_ref[h] * LOG2E
        s = s + bias_ref[h, variant]
        m = jnp.maximum(jnp.max(s, axis=0, keepdims=True), sink)
        e = jnp.exp2(s - m)
        denom = jnp.sum(e, axis=0, keepdims=True) + jnp.exp2(sink - m)
        e = e.astype(BF16)
        yield
        pv = jnp.dot(vt_win[kv * HEAD_DIM:(kv + 1) * HEAD_DIM], e, preferred_element_type=F32)
        for _ in range(SWA_LEAD):
            yield
        return pv / denom

    outs = _interleave([chain(*half, h) for half in halves for h in range(SWA_Q_HEADS)])
    out_t = jnp.concatenate([jnp.concatenate(outs[:SWA_Q_HEADS], axis=0),
                             jnp.concatenate(outs[SWA_Q_HEADS:], axis=0)], axis=1)
    o_ref[0] = out_t.T.astype(BF16)


def _swa_attention(sinks, p2, p3, sbias, batch, seq):
    nq = seq // ATT_TILE
    p2 = p2.reshape(batch, seq, P2_W)
    kdup_w = 2 * SWA_KV_W
    return pl.pallas_call(
        _swa_kernel,
        name="swa_attn",
        grid=(batch, nq),
        in_specs=[
            pl.BlockSpec(memory_space=pltpu.SMEM),
            pl.BlockSpec((1, ATT_TILE, SWA_Q_W), lambda b, i: (b, i, 2 * DIFF_W // SWA_Q_W)),
            pl.BlockSpec((1, seq, kdup_w), lambda b, i: (b, 0, (2 * DIFF_W + SWA_Q_W) // kdup_w)),
            pl.BlockSpec((nq, SWA_KV_W, ATT_TILE), lambda b, i: (b, (SB_W + DIFF_W) // SWA_KV_W, 0)),
            pl.BlockSpec((SWA_Q_HEADS, 2, 2 * WINDOW, WINDOW), lambda b, i: (0, 0, 0, 0)),
        ],
        out_specs=pl.BlockSpec((1, ATT_TILE, SWA_Q_W), lambda b, i: (b, i, 0)),
        out_shape=jax.ShapeDtypeStruct((batch, seq, SWA_Q_W), BF16),
        compiler_params=_compiler_params(("parallel", "arbitrary"), 40 * 1024 * 1024),
    )(sinks, p2, p2, p3, sbias)


def _merge_kernel(x_ref, osb_ref, odf_ref, osw_ref, ga_ref, gd_ref, gs_ref,
                  wsb_ref, wdf_ref, wsw_ref, wo_ref, o_ref):
    def branch(o_ref_, w_ref, g_ref):
        gate = 1.0 / (1.0 + jnp.exp(-g_ref[...].astype(F32)))
        return gate * jnp.dot(o_ref_[...], w_ref[...], preferred_element_type=F32)

    merged = (branch(osb_ref, wsb_ref, ga_ref) + branch(odf_ref, wdf_ref, gd_ref)
              + branch(osw_ref, wsw_ref, gs_ref))
    o_ref[...] = x_ref[...] + jnp.dot(merged.astype(BF16), wo_ref[...], preferred_element_type=F32)


def _merge(xt, o_sb, o_diff, o_swa, p1, w_sb, w_diff, w_swa, w_out):
    t, d = xt.shape
    gcol = 2 * SB_W // d
    row = lambda i: (i, 0)
    const = lambda i: (0, 0)
    return pl.pallas_call(
        _merge_kernel,
        name="merge",
        grid=(t // ROW_TILE,),
        in_specs=[
            pl.BlockSpec((ROW_TILE, d), row),
            pl.BlockSpec((ROW_TILE, SB_W), row),
            pl.BlockSpec((ROW_TILE, DIFF_W), row),
            pl.BlockSpec((ROW_TILE, SWA_Q_W), row),
            pl.BlockSpec((ROW_TILE, d), lambda i: (i, gcol)),
            pl.BlockSpec((ROW_TILE, d), lambda i: (i, gcol + 1)),
            pl.BlockSpec((ROW_TILE, d), lambda i: (i, gcol + 2)),
            pl.BlockSpec((SB_W, d), const),
            pl.BlockSpec((DIFF_W, d), const),
            pl.BlockSpec((SWA_Q_W, d), const),
            pl.BlockSpec((d, d), const),
        ],
        out_specs=pl.BlockSpec((ROW_TILE, d), row),
        out_shape=jax.ShapeDtypeStruct((t, d), F32),
        compiler_params=_compiler_params(("parallel",), 48 * 1024 * 1024),
    )(xt, o_sb.reshape(t, SB_W), o_diff.reshape(t, DIFF_W), o_swa.reshape(t, SWA_Q_W),
      p1, p1, p1, w_sb.astype(BF16), w_diff.astype(BF16), w_swa.astype(BF16), w_out.astype(BF16))


def _projection_weights(w_in, q_norm_diff, k_norm_diff, q_norm_swa, k_norm_swa):
    widths = (SB_W, SB_W, SB_W, DIFF_W, DIFF_W, DIFF_W, SWA_Q_W, SWA_KV_W, SWA_KV_W)
    offs = [0]
    for w in widths:
        offs.append(offs[-1] + w)
    qa, ka, va, qd, kd, vd, qs, ks, vs = (w_in[:, offs[n]:offs[n + 1]] for n in range(len(widths)))
    gates = w_in[:, offs[-1]:]
    qscale = SCALE * LOG2E

    def dup_heads(w):
        return jnp.concatenate([w[:, h * HEAD_DIM:(h + 1) * HEAD_DIM]
                                for h in range(SWA_KV_HEADS) for _ in range(2)], axis=1)

    w1 = jnp.concatenate([qa * qscale, ka, gates], axis=1).astype(BF16)
    w2 = jnp.concatenate([qd, kd, qs, dup_heads(ks)], axis=1).astype(BF16)
    w3 = jnp.concatenate([va, vd, vs], axis=1).astype(BF16)
    g2 = jnp.concatenate([
        jnp.tile(q_norm_diff * qscale, DIFF_W // HEAD_DIM), jnp.tile(k_norm_diff, DIFF_W // HEAD_DIM),
        jnp.tile(q_norm_swa * qscale, SWA_Q_W // HEAD_DIM), jnp.tile(k_norm_swa, 2 * SWA_KV_W // HEAD_DIM),
    ]).reshape(1, P2_W).astype(F32)
    return w1, w2, g2, w3


def _group_mean_matrix():
    g = jnp.arange(P2_CHUNK) // HEAD_DIM
    return jnp.where(g[:, None] == g[None, :], 1.0 / HEAD_DIM, 0.0).astype(BF16)


def _neg_suffix_matrix():
    idx = jnp.arange(ATT_TILE)
    return jnp.where(idx[None, :] > idx[:, None], -1.0, 0.0).astype(BF16)


def kernel(x, ffn1_norm, ffn1_w_in, ffn1_w_out, mix_norm, w_in, q_norm_diff, k_norm_diff, q_norm_swa, k_norm_swa, diff_lambda, diff_subln, swa_sinks, rel_bias, w_proj_sb, w_proj_diff, w_proj_swa, w_out, ffn2_norm, ffn2_w_in, ffn2_w_out):
    batch, seq, d = x.shape
    depth = ffn1_norm.shape[0]
    assert seq % ATT_TILE == 0 and (batch * seq) % ROW_TILE == 0
    xt = x.reshape(batch * seq, d)
    dbias, sbias = _bias_tiles(rel_bias)
    gmat = _group_mean_matrix()
    mtri = _neg_suffix_matrix()
    for l in range(depth):
        lam_init = 0.8 - 0.6 * math.exp(-0.3 * l)
        xt = _ffn(xt, ffn1_norm[l], ffn1_w_in[l], ffn1_w_out[l])
        w1, w2, g2, w3 = _projection_weights(w_in[l], q_norm_diff[l], k_norm_diff[l],
                                             q_norm_swa[l], k_norm_swa[l])
        p1, p2, p3 = _projection(xt, mix_norm[l], w1, w2, g2, gmat, w3)
        o_sb = _sb_attention(p1, p3, mtri, batch, seq)
        o_diff = _diff_attention(p2, p3, dbias, diff_lambda[l], diff_subln[l], lam_init, batch, seq)
        o_swa = _swa_attention(swa_sinks[l], p2, p3, sbias, batch, seq)
        xt = _merge(xt, o_sb, o_diff, o_swa, p1, w_proj_sb[l], w_proj_diff[l], w_proj_swa[l], w_out[l])
        xt = _ffn(xt, ffn2_norm[l], ffn2_w_in[l], ffn2_w_out[l])
    return xt.reshape(batch, seq, d)
```

```python
import functools
import math

import jax
import jax.numpy as jnp
from jax import lax
from jax.experimental import pallas as pl
from jax.experimental.pallas import tpu as pltpu

F32 = jnp.float32
BF16 = jnp.bfloat16

HEAD_DIM = 64
SB_HEADS = 8
DIFF_HEADS = 4
SWA_Q_HEADS = 8
SWA_KV_HEADS = 2
SWA_GROUP = SWA_Q_HEADS // SWA_KV_HEADS
WINDOW = 128
N_BUCKETS = 32
MAX_DISTANCE = 128
EPS = 1e-6
SCALE = HEAD_DIM ** -0.5
LOG2E = math.log2(math.e)
SB_DEAD_LOG2 = -160.0
SWA_LEAD = 4

SB_W = SB_HEADS * HEAD_DIM
DIFF_W = DIFF_HEADS * 2 * HEAD_DIM
SWA_Q_W = SWA_Q_HEADS * HEAD_DIM
SWA_KV_W = SWA_KV_HEADS * HEAD_DIM

V7X_LANES = 128
V7X_MXU_DIM = 256
V7X_VMEM_BYTES = 64 * 1024 * 1024

ATT_TILE = V7X_MXU_DIM
ROW_TILE = 512
PAIR_W = 2 * HEAD_DIM
assert PAIR_W == V7X_LANES and WINDOW <= ATT_TILE

P1_W = 2 * SB_W + 3 * 1024
P2_W = 2 * DIFF_W + SWA_Q_W + 2 * SWA_KV_W
P3_W = SB_W + DIFF_W + SWA_KV_W
P1_CHUNK = 512
P2_CHUNK = 256
P3_CHUNKS = ((0, 256), (256, 256), (512, 256), (768, 256), (1024, 128))
assert sum(w for _, w in P3_CHUNKS) == P3_W

_NT = (((1,), (1,)), ((), ()))


def _compiler_params(semantics, vmem_bytes):
    assert vmem_bytes < V7X_VMEM_BYTES
    return pltpu.CompilerParams(dimension_semantics=semantics, vmem_limit_bytes=vmem_bytes)


def _rmsnorm_rows(x, g):
    ms = jnp.mean(x * x, axis=-1, keepdims=True)
    return x * lax.rsqrt(ms + EPS) * g


def _ffn_kernel(x_ref, g_ref, wg_ref, wu_ref, wo_ref, o_ref):
    x = x_ref[...]
    h = _rmsnorm_rows(x, g_ref[...]).astype(BF16)
    g = jnp.dot(h, wg_ref[...], preferred_element_type=F32)
    u = jnp.dot(h, wu_ref[...], preferred_element_type=F32)
    a = (g * (1.0 / (1.0 + jnp.exp(-g))) * u).astype(BF16)
    o_ref[...] = x + 0.5 * jnp.dot(a, wo_ref[...], preferred_element_type=F32)


def _ffn(xt, gain, w_in, w_out):
    t, d = xt.shape
    d_ff = w_out.shape[0]
    assert d_ff % V7X_MXU_DIM == 0 and t % ROW_TILE == 0
    w_in = w_in.astype(BF16)
    w_out = w_out.astype(BF16)
    resident = pl.Buffered(1)
    return pl.pallas_call(
        _ffn_kernel,
        name="ffn",
        grid=(t // ROW_TILE,),
        in_specs=[
            pl.BlockSpec((ROW_TILE, d), lambda i: (i, 0)),
            pl.BlockSpec((1, d), lambda i: (0, 0)),
            pl.BlockSpec((d, d_ff), lambda i: (0, 0), pipeline_mode=resident),
            pl.BlockSpec((d, d_ff), lambda i: (0, 1), pipeline_mode=resident),
            pl.BlockSpec((d_ff, d), lambda i: (0, 0), pipeline_mode=resident),
        ],
        out_specs=pl.BlockSpec((ROW_TILE, d), lambda i: (i, 0)),
        out_shape=jax.ShapeDtypeStruct((t, d), F32),
        compiler_params=_compiler_params(("parallel",), 56 * 1024 * 1024),
    )(xt, gain.reshape(1, d), w_in, w_in, w_out)


def _proj_kernel(x_ref, ng_ref, w1_ref, w2_ref, g2_ref, gm_ref, w3_ref,
                 p1_ref, p2_ref, p3_ref):
    h = _rmsnorm_rows(x_ref[...], ng_ref[...]).astype(BF16)
    for c in range(P1_W // P1_CHUNK):
        cols = slice(c * P1_CHUNK, (c + 1) * P1_CHUNK)
        p1_ref[:, cols] = jnp.dot(h, w1_ref[:, cols], preferred_element_type=F32).astype(BF16)
    y_all = jnp.dot(h, w2_ref[...], preferred_element_type=F32)
    for c in range(P2_W // P2_CHUNK):
        cols = slice(c * P2_CHUNK, (c + 1) * P2_CHUNK)
        y = y_all[:, cols]
        ms = jnp.dot((y * y).astype(BF16), gm_ref[...], preferred_element_type=F32)
        p2_ref[:, cols] = (y * lax.rsqrt(ms + EPS) * g2_ref[:, cols]).astype(BF16)
    for start, width in P3_CHUNKS:
        cols = slice(start, start + width)
        yt = jnp.dot(h, w3_ref[:, cols], preferred_element_type=F32).T
        for r in range(ROW_TILE // ATT_TILE):
            p3_ref[r, cols, :] = yt[:, r * ATT_TILE:(r + 1) * ATT_TILE].astype(BF16)


def _projection(xt, norm_gain, w1, w2, g2, gmat, w3):
    t, d = xt.shape
    const = lambda i: (0, 0)
    return pl.pallas_call(
        _proj_kernel,
        name="mixer_proj",
        grid=(t // ROW_TILE,),
        in_specs=[
            pl.BlockSpec((ROW_TILE, d), lambda i: (i, 0)),
            pl.BlockSpec((1, d), const),
            pl.BlockSpec((d, P1_W), const, pipeline_mode=pl.Buffered(1)),
            pl.BlockSpec((d, P2_W), const, pipeline_mode=pl.Buffered(1)),
            pl.BlockSpec((1, P2_W), const),
            pl.BlockSpec((P2_CHUNK, P2_CHUNK), const),
            pl.BlockSpec((d, P3_W), const, pipeline_mode=pl.Buffered(1)),
        ],
        out_specs=[
            pl.BlockSpec((ROW_TILE, P1_W), lambda i: (i, 0)),
            pl.BlockSpec((ROW_TILE, P2_W), lambda i: (i, 0)),
            pl.BlockSpec((ROW_TILE // ATT_TILE, P3_W, ATT_TILE), lambda i: (i, 0, 0)),
        ],
        out_shape=[
            jax.ShapeDtypeStruct((t, P1_W), BF16),
            jax.ShapeDtypeStruct((t, P2_W), BF16),
            jax.ShapeDtypeStruct((t // ATT_TILE, P3_W, ATT_TILE), BF16),
        ],
        compiler_params=_compiler_params(("parallel",), 52 * 1024 * 1024),
    )(xt, norm_gain.reshape(1, d), w1, w2, g2, gmat, w3)


def _t5_bucket(dist):
    n = jnp.maximum(dist, 0)
    max_exact = N_BUCKETS // 2
    nf = jnp.maximum(n, 1).astype(F32)
    large = max_exact + (jnp.log(nf / max_exact) / math.log(MAX_DISTANCE / max_exact)
                         * (N_BUCKETS - max_exact)).astype(jnp.int32)
    large = jnp.minimum(large, N_BUCKETS - 1)
    return jnp.where(n < max_exact, n, large)


def _bias_lookup(bucket, tab_ref, head):
    val = jnp.zeros(bucket.shape, F32)
    for b in range(N_BUCKETS):
        val = jnp.where(bucket == b, tab_ref[b, head], val)
    return val


def _bias_kernel(tab_ref, dbias_ref, sbias_ref):
    kk = lax.broadcasted_iota(jnp.int32, (ATT_TILE, ATT_TILE), 0)
    qq = lax.broadcasted_iota(jnp.int32, (ATT_TILE, ATT_TILE), 1)
    for v in range(2):
        dist = qq - kk + v * ATT_TILE
        bucket = _t5_bucket(dist)
        for h in range(DIFF_HEADS):
            val = (_bias_lookup(bucket, tab_ref, h) - tab_ref[N_BUCKETS - 1, h]) * LOG2E
            dbias_ref[h, v] = jnp.where(dist >= 0, val, -jnp.inf)
    kk = lax.broadcasted_iota(jnp.int32, (2 * WINDOW, WINDOW), 0)
    qq = lax.broadcasted_iota(jnp.int32, (2 * WINDOW, WINDOW), 1)
    for v in range(2):
        dist = qq - kk + (1 - v) * WINDOW
        bucket = _t5_bucket(dist)
        for h in range(SWA_Q_HEADS):
            val = _bias_lookup(bucket, tab_ref, DIFF_HEADS + h) * LOG2E
            sbias_ref[h, v] = jnp.where(dist >= 0, jnp.where(dist < WINDOW, val, -jnp.inf), -jnp.inf)


def _bias_tiles(rel_bias):
    return pl.pallas_call(
        _bias_kernel,
        name="bias_tiles",
        in_specs=[pl.BlockSpec(memory_space=pltpu.SMEM)],
        out_specs=[pl.BlockSpec(memory_space=pltpu.VMEM), pl.BlockSpec(memory_space=pltpu.VMEM)],
        out_shape=[
            jax.ShapeDtypeStruct((DIFF_HEADS, 2, ATT_TILE, ATT_TILE), F32),
            jax.ShapeDtypeStruct((SWA_Q_HEADS, 2, 2 * WINDOW, WINDOW), F32),
        ],
    )(rel_bias)


def _split_pairs(q, n_pairs):
    heads = []
    for p in range(n_pairs):
        qf = q[:, p * PAIR_W:(p + 1) * PAIR_W].astype(F32)
        lane = lax.broadcasted_iota(jnp.int32, qf.shape, 1)
        zero = jnp.zeros_like(qf)
        heads.append(jnp.where(lane < HEAD_DIM, qf, zero).astype(BF16))
        heads.append(jnp.where(lane >= HEAD_DIM, qf, zero).astype(BF16))
    return heads


def _neg_abs(x):
    bits = lax.bitcast_convert_type(x, jnp.uint32) | jnp.uint32(0x80000000)
    return lax.bitcast_convert_type(bits, F32)


def _interleave(chains):
    results = [None] * len(chains)
    live = []
    started = 0
    while started < len(chains) or live:
        if started < len(chains):
            live.append(started)
            started += 1
        for n in reversed(list(live)):
            try:
                next(chains[n])
            except StopIteration as done:
                results[n] = done.value
                live.remove(n)
    return results


def _key_tile(ref, j):
    return ref[0, pl.ds(pl.multiple_of(j * ATT_TILE, ATT_TILE), ATT_TILE), :]


def _sb_kernel(q_ref, k_ref, vt_ref, mtri_ref, o_ref, acc_ref, z_ref, btw_ref):
    i = pl.program_id(1)
    q_heads = _split_pairs(q_ref[0], SB_HEADS // 2)
    mtri = mtri_ref[...]
    kk = lax.broadcasted_iota(jnp.int32, (ATT_TILE, ATT_TILE), 0)
    qq = lax.broadcasted_iota(jnp.int32, (ATT_TILE, ATT_TILE), 1)
    strictly_before = kk < qq

    def run(tiles, carries):
        carry = list(carries)

        def chain(k, vt, keep, h, slot):
            k_pair = k[:, (h // 2) * PAIR_W:(h // 2 + 1) * PAIR_W]
            rows = slice(h * HEAD_DIM, (h + 1) * HEAD_DIM)
            z_ref[slot] = lax.dot_general(k_pair, q_heads[h], _NT, preferred_element_type=F32)
            yield
            yield
            z = z_ref[slot]
            sp = jnp.maximum(z, 0.0) + jnp.log(1.0 + jnp.exp2(_neg_abs(z))) * LOG2E
            if keep is not None:
                sp = jnp.where(keep, sp, 0.0)
            after = carry[h]
            carry[h] = after - jnp.sum(sp, axis=0, keepdims=True)
            sp_b = sp.astype(BF16)
            yield
            btw_ref[slot] = jnp.dot(mtri, sp_b, preferred_element_type=F32)
            yield
            yield
            w = jnp.exp2((z - sp) + (btw_ref[slot] + after))
            if keep is not None:
                w = jnp.where(keep, w, 0.0)
            w = w.astype(BF16)
            yield
            acc_ref[rows, :] += jnp.dot(vt[rows], w, preferred_element_type=F32)

        chains = []
        for t, (j, keep) in enumerate(tiles):
            k = _key_tile(k_ref, j)
            vt = vt_ref[j]
            chains.extend(chain(k, vt, keep, h, SB_HEADS * t + h) for h in range(SB_HEADS))
        _interleave(chains)
        return tuple(carry)

    acc_ref[...] = jnp.zeros_like(acc_ref)
    zeros = (jnp.zeros((1, ATT_TILE), F32),) * SB_HEADS
    carries = lax.cond(
        i >= 1,
        lambda: run([(i, strictly_before), (i - 1, None)], zeros),
        lambda: run([(i, strictly_before)], zeros))

    def live(c):
        j, carries = c
        return jnp.logical_and(j >= 0, jnp.max(jnp.concatenate(carries, axis=0)) > SB_DEAD_LOG2)

    lax.while_loop(live, lambda c: (c[0] - 1, run([(c[0], None)], c[1])), (i - 2, carries))
    o_ref[0] = acc_ref[...].T.astype(BF16)


def _sb_attention(p1, p3, mtri, batch, seq):
    nq = seq // ATT_TILE
    p1 = p1.reshape(batch, seq, P1_W)
    return pl.pallas_call(
        _sb_kernel,
        name="sb_attn",
        grid=(batch, nq),
        in_specs=[
            pl.BlockSpec((1, ATT_TILE, SB_W), lambda b, i: (b, i, 0)),
            pl.BlockSpec((1, seq, SB_W), lambda b, i: (b, 0, 1)),
            pl.BlockSpec((nq, SB_W, ATT_TILE), lambda b, i: (b, 0, 0)),
            pl.BlockSpec((ATT_TILE, ATT_TILE), lambda b, i: (0, 0)),
        ],
        out_specs=pl.BlockSpec((1, ATT_TILE, SB_W), lambda b, i: (b, i, 0)),
        out_shape=jax.ShapeDtypeStruct((batch, seq, SB_W), BF16),
        scratch_shapes=[pltpu.VMEM((SB_W, ATT_TILE), F32),
                        pltpu.VMEM((2 * SB_HEADS, ATT_TILE, ATT_TILE), F32),
                        pltpu.VMEM((2 * SB_HEADS, ATT_TILE, ATT_TILE), F32)],
        compiler_params=_compiler_params(("parallel", "arbitrary"), 40 * 1024 * 1024),
    )(p1, p1, p3, mtri)


def _diff_kernel(q_ref, k_ref, vt_ref, bias_ref, lam_ref, gain_ref, o_ref, acc_ref, s_ref, pv_ref,
                 *, lam_init):
    i = pl.program_id(1)
    q_comps = _split_pairs(q_ref[0], DIFF_HEADS)

    def run(tiles, state):
        m_run = [s[0] for s in state]
        l_run = [s[1] for s in state]

        def chain(k, vt, variant, n, slot):
            h = n // 2
            s_ref[slot] = lax.dot_general(k[:, h * PAIR_W:(h + 1) * PAIR_W], q_comps[n], _NT,
                                          preferred_element_type=F32)
            yield
            yield
            s = s_ref[slot]
            if variant is not None:
                s = s + bias_ref[h, variant]
            m_prev = m_run[n]
            m_new = jnp.maximum(m_prev, jnp.max(s, axis=0, keepdims=True))
            alpha = jnp.exp2(m_prev - m_new)
            m_run[n] = m_new
            yield
            p = jnp.exp2(s - m_new)
            l_run[n] = alpha * l_run[n] + jnp.sum(p, axis=0, keepdims=True)
            p = p.astype(BF16)
            yield
            pv_ref[slot] = jnp.dot(vt[h * PAIR_W:(h + 1) * PAIR_W], p, preferred_element_type=F32)
            yield
            yield
            acc_ref[n] = alpha * acc_ref[n] + pv_ref[slot]

        chains = []
        for t, (j, variant) in enumerate(tiles):
            k = _key_tile(k_ref, j)
            vt = vt_ref[j]
            chains.extend(chain(k, vt, variant, n, 2 * DIFF_HEADS * t + n)
                          for n in range(2 * DIFF_HEADS))
        _interleave(chains)
        return tuple(zip(m_run, l_run))

    acc_ref[...] = jnp.zeros_like(acc_ref)
    init = ((jnp.full((1, ATT_TILE), -jnp.inf, F32), jnp.zeros((1, ATT_TILE), F32)),) * (2 * DIFF_HEADS)
    state = lax.cond(i >= 1,
                     lambda: run([(i, 0), (i - 1, 1)], init),
                     lambda: run([(i, 0)], init))
    n_far = jnp.maximum(i - 1, 0)
    state = lax.fori_loop(
        0, lax.shift_right_logical(n_far, 1),
        lambda t, st: run([(i - 2 - 2 * t, None), (i - 3 - 2 * t, None)], st), state)
    state = lax.fori_loop(0, n_far & 1, lambda t, st: run([(0, None)], st), state)

    lv = lam_ref[...]
    lam = (jnp.exp(jnp.sum(lv[0:1] * lv[1:2], axis=-1, keepdims=True))
           - jnp.exp(jnp.sum(lv[2:3] * lv[3:4], axis=-1, keepdims=True)) + lam_init)
    outs = []
    for h in range(DIFF_HEADS):
        (_, l1), (_, l2) = state[2 * h], state[2 * h + 1]
        o = acc_ref[2 * h] / l1 - lam * (acc_ref[2 * h + 1] / l2)
        ms = jnp.mean(o * o, axis=0, keepdims=True)
        outs.append(o * lax.rsqrt(ms + EPS) * gain_ref[...] * (1.0 - lam_init))
    o_ref[0] = jnp.concatenate(outs, axis=0).T.astype(BF16)


def _diff_attention(p2, p3, dbias, lam_vecs, sub_gain, lam_init, batch, seq):
    nq = seq // ATT_TILE
    p2 = p2.reshape(batch, seq, P2_W)
    return pl.pallas_call(
        functools.partial(_diff_kernel, lam_init=lam_init),
        name="diff_attn",
        grid=(batch, nq),
        in_specs=[
            pl.BlockSpec((1, ATT_TILE, DIFF_W), lambda b, i: (b, i, 0)),
            pl.BlockSpec((1, seq, DIFF_W), lambda b, i: (b, 0, 1)),
            pl.BlockSpec((nq, DIFF_W, ATT_TILE), lambda b, i: (b, SB_W // DIFF_W, 0)),
            pl.BlockSpec((DIFF_HEADS, 2, ATT_TILE, ATT_TILE), lambda b, i: (0, 0, 0, 0)),
            pl.BlockSpec((4, HEAD_DIM), lambda b, i: (0, 0)),
            pl.BlockSpec((PAIR_W, 1), lambda b, i: (0, 0)),
        ],
        out_specs=pl.BlockSpec((1, ATT_TILE, DIFF_W), lambda b, i: (b, i, 0)),
        out_shape=jax.ShapeDtypeStruct((batch, seq, DIFF_W), BF16),
        scratch_shapes=[pltpu.VMEM((2 * DIFF_HEADS, PAIR_W, ATT_TILE), F32),
                        pltpu.VMEM((4 * DIFF_HEADS, ATT_TILE, ATT_TILE), F32),
                        pltpu.VMEM((4 * DIFF_HEADS, PAIR_W, ATT_TILE), F32)],
        compiler_params=_compiler_params(("parallel", "arbitrary"), 40 * 1024 * 1024),
    )(p2, p2, p3, dbias, lam_vecs, sub_gain.reshape(PAIR_W, 1))


def _swa_kernel(sink_ref, q_ref, k_ref, vt_ref, bias_ref, o_ref, s_ref, pv_ref):
    i = pl.program_id(1)
    first = i == 0
    prev = jnp.maximum(i - 1, 0)
    q = q_ref[0]
    k_hi = _key_tile(k_ref, i)
    vt_hi = vt_ref[i]
    lo_start = pl.multiple_of(jnp.maximum(i * ATT_TILE - WINDOW, 0), WINDOW)
    k_lo = k_ref[0, pl.ds(lo_start, 2 * WINDOW), :]
    vt_lo = jnp.where(first, vt_hi, jnp.concatenate([vt_ref[prev][:, WINDOW:], vt_hi[:, :WINDOW]], axis=1))
    lo_variant = jnp.where(first, 1, 0)
    halves = ((_split_pairs(q[:WINDOW], SWA_Q_HEADS // 2), k_lo, vt_lo, lo_variant),
              (_split_pairs(q[WINDOW:], SWA_Q_HEADS // 2), k_hi, vt_hi, 0))

    def chain(q_heads, k_win, vt_win, variant, h, slot):
        kv = h // SWA_GROUP
        k_pair = k_win[:, kv * PAIR_W:(kv + 1) * PAIR_W]
        s_ref[slot] = lax.dot_general(k_pair, q_heads[h], _NT, preferred_element_type=F32)
        for _ in range(SWA_LEAD):
            yield
        sink = sink_ref[h] * LOG2E
        s = s_ref[slot] + bias_ref[h, variant]
        m = jnp.maximum(jnp.max(s, axis=0, keepdims=True), sink)
        e = jnp.exp2(s - m)
        denom = jnp.sum(e, axis=0, keepdims=True) + jnp.exp2(sink - m)
        e = e.astype(BF16)
        yield
        pv_ref[slot] = jnp.dot(vt_win[kv * HEAD_DIM:(kv + 1) * HEAD_DIM], e, preferred_element_type=F32)
        for _ in range(SWA_LEAD):
            yield
        return pv_ref[slot] / denom

    outs = _interleave([chain(*half, h, a * SWA_Q_HEADS + h)
                        for a, half in enumerate(halves) for h in range(SWA_Q_HEADS)])
    out_t = jnp.concatenate([jnp.concatenate(outs[:SWA_Q_HEADS], axis=0),
                             jnp.concatenate(outs[SWA_Q_HEADS:], axis=0)], axis=1)
    o_ref[0] = out_t.T.astype(BF16)


def _swa_attention(sinks, p2, p3, sbias, batch, seq):
    nq = seq // ATT_TILE
    p2 = p2.reshape(batch, seq, P2_W)
    kdup_w = 2 * SWA_KV_W
    return pl.pallas_call(
        _swa_kernel,
        name="swa_attn",
        grid=(batch, nq),
        in_specs=[
            pl.BlockSpec(memory_space=pltpu.SMEM),
            pl.BlockSpec((1, ATT_TILE, SWA_Q_W), lambda b, i: (b, i, 2 * DIFF_W // SWA_Q_W)),
            pl.BlockSpec((1, seq, kdup_w), lambda b, i: (b, 0, (2 * DIFF_W + SWA_Q_W) // kdup_w)),
            pl.BlockSpec((nq, SWA_KV_W, ATT_TILE), lambda b, i: (b, (SB_W + DIFF_W) // SWA_KV_W, 0)),
            pl.BlockSpec((SWA_Q_HEADS, 2, 2 * WINDOW, WINDOW), lambda b, i: (0, 0, 0, 0)),
        ],
        out_specs=pl.BlockSpec((1, ATT_TILE, SWA_Q_W), lambda b, i: (b, i, 0)),
        out_shape=jax.ShapeDtypeStruct((batch, seq, SWA_Q_W), BF16),
        scratch_shapes=[pltpu.VMEM((2 * SWA_Q_HEADS, 2 * WINDOW, WINDOW), F32),
                        pltpu.VMEM((2 * SWA_Q_HEADS, HEAD_DIM, WINDOW), F32)],
        compiler_params=_compiler_params(("parallel", "arbitrary"), 40 * 1024 * 1024),
    )(sinks, p2, p2, p3, sbias)


def _merge_kernel(x_ref, osb_ref, odf_ref, osw_ref, ga_ref, gd_ref, gs_ref,
                  wsb_ref, wdf_ref, wsw_ref, wo_ref, o_ref):
    def branch(o_ref_, w_ref, g_ref):
        gate = 1.0 / (1.0 + jnp.exp(-g_ref[...].astype(F32)))
        return gate * jnp.dot(o_ref_[...], w_ref[...], preferred_element_type=F32)

    merged = (branch(osb_ref, wsb_ref, ga_ref) + branch(odf_ref, wdf_ref, gd_ref)
              + branch(osw_ref, wsw_ref, gs_ref))
    o_ref[...] = x_ref[...] + jnp.dot(merged.astype(BF16), wo_ref[...], preferred_element_type=F32)


def _merge(xt, o_sb, o_diff, o_swa, p1, w_sb, w_diff, w_swa, w_out):
    t, d = xt.shape
    gcol = 2 * SB_W // d
    row = lambda i: (i, 0)
    const = lambda i: (0, 0)
    return pl.pallas_call(
        _merge_kernel,
        name="merge",
        grid=(t // ROW_TILE,),
        in_specs=[
            pl.BlockSpec((ROW_TILE, d), row),
            pl.BlockSpec((ROW_TILE, SB_W), row),
            pl.BlockSpec((ROW_TILE, DIFF_W), row),
            pl.BlockSpec((ROW_TILE, SWA_Q_W), row),
            pl.BlockSpec((ROW_TILE, d), lambda i: (i, gcol)),
            pl.BlockSpec((ROW_TILE, d), lambda i: (i, gcol + 1)),
            pl.BlockSpec((ROW_TILE, d), lambda i: (i, gcol + 2)),
            pl.BlockSpec((SB_W, d), const),
            pl.BlockSpec((DIFF_W, d), const),
            pl.BlockSpec((SWA_Q_W, d), const),
            pl.BlockSpec((d, d), const),
        ],
        out_specs=pl.BlockSpec((ROW_TILE, d), row),
        out_shape=jax.ShapeDtypeStruct((t, d), F32),
        compiler_params=_compiler_params(("parallel",), 48 * 1024 * 1024),
    )(xt, o_sb.reshape(t, SB_W), o_diff.reshape(t, DIFF_W), o_swa.reshape(t, SWA_Q_W),
      p1, p1, p1, w_sb.astype(BF16), w_diff.astype(BF16), w_swa.astype(BF16), w_out.astype(BF16))


def _projection_weights(w_in, q_norm_diff, k_norm_diff, q_norm_swa, k_norm_swa):
    widths = (SB_W, SB_W, SB_W, DIFF_W, DIFF_W, DIFF_W, SWA_Q_W, SWA_KV_W, SWA_KV_W)
    offs = [0]
    for w in widths:
        offs.append(offs[-1] + w)
    qa, ka, va, qd, kd, vd, qs, ks, vs = (w_in[:, offs[n]:offs[n + 1]] for n in range(len(widths)))
    gates = w_in[:, offs[-1]:]
    qscale = SCALE * LOG2E

    def dup_heads(w):
        return jnp.concatenate([w[:, h * HEAD_DIM:(h + 1) * HEAD_DIM]
                                for h in range(SWA_KV_HEADS) for _ in range(2)], axis=1)

    w1 = jnp.concatenate([qa * qscale, ka, gates], axis=1).astype(BF16)
    w2 = jnp.concatenate([qd, kd, qs, dup_heads(ks)], axis=1).astype(BF16)
    w3 = jnp.concatenate([va, vd, vs], axis=1).astype(BF16)
    g2 = jnp.concatenate([
        jnp.tile(q_norm_diff * qscale, DIFF_W // HEAD_DIM), jnp.tile(k_norm_diff, DIFF_W // HEAD_DIM),
        jnp.tile(q_norm_swa * qscale, SWA_Q_W // HEAD_DIM), jnp.tile(k_norm_swa, 2 * SWA_KV_W // HEAD_DIM),
    ]).reshape(1, P2_W).astype(F32)
    return w1, w2, g2, w3


def _group_mean_matrix():
    g = jnp.arange(P2_CHUNK) // HEAD_DIM
    return jnp.where(g[:, None] == g[None, :], 1.0 / HEAD_DIM, 0.0).astype(BF16)


def _neg_suffix_matrix():
    idx = jnp.arange(ATT_TILE)
    return jnp.where(idx[None, :] > idx[:, None], -1.0, 0.0).astype(BF16)


def kernel(x, ffn1_norm, ffn1_w_in, ffn1_w_out, mix_norm, w_in, q_norm_diff, k_norm_diff, q_norm_swa, k_norm_swa, diff_lambda, diff_subln, swa_sinks, rel_bias, w_proj_sb, w_proj_diff, w_proj_swa, w_out, ffn2_norm, ffn2_w_in, ffn2_w_out):
    batch, seq, d = x.shape
    depth = ffn1_norm.shape[0]
    assert seq % ATT_TILE == 0 and (batch * seq) % ROW_TILE == 0
    xt = x.reshape(batch * seq, d)
    dbias, sbias = _bias_tiles(rel_bias)
    gmat = _group_mean_matrix()
    mtri = _neg_suffix_matrix()
    for l in range(depth):
        lam_init = 0.8 - 0.6 * math.exp(-0.3 * l)
        xt = _ffn(xt, ffn1_norm[l], ffn1_w_in[l], ffn1_w_out[l])
        w1, w2, g2, w3 = _projection_weights(w_in[l], q_norm_diff[l], k_norm_diff[l],
                                             q_norm_swa[l], k_norm_swa[l])
        p1, p2, p3 = _projection(xt, mix_norm[l], w1, w2, g2, gmat, w3)
        o_sb = _sb_attention(p1, p3, mtri, batch, seq)
        o_diff = _diff_attention(p2, p3, dbias, diff_lambda[l], diff_subln[l], lam_init, batch, seq)
        o_swa = _swa_attention(swa_sinks[l], p2, p3, sbias, batch, seq)
        xt = _merge(xt, o_sb, o_diff, o_swa, p1, w_proj_sb[l], w_proj_diff[l], w_proj_swa[l], w_out[l])
        xt = _ffn(xt, ffn2_norm[l], ffn2_w_in[l], ffn2_w_out[l])
    return xt.reshape(batch, seq, d)
```

```python
import functools
import math

import jax
import jax.numpy as jnp
from jax import lax
from jax.experimental import pallas as pl
from jax.experimental.pallas import tpu as pltpu

F32 = jnp.float32
BF16 = jnp.bfloat16

D_MODEL = 1024
HEAD_DIM = 64
SB_HEADS = 8
DIFF_HEADS = 4
SWA_Q_HEADS = 8
SWA_KV_HEADS = 2
SWA_GROUP = SWA_Q_HEADS // SWA_KV_HEADS
WINDOW = 128
N_BUCKETS = 32
MAX_DISTANCE = 128
EPS = 1e-6
SCALE = HEAD_DIM ** -0.5
LOG2E = math.log2(math.e)
SB_DEAD_LOG2 = -160.0
MXU_LEAD_MIXED = 2
MXU_LEAD_LOOP = 4

SB_W = SB_HEADS * HEAD_DIM
DIFF_W = DIFF_HEADS * 2 * HEAD_DIM
SWA_Q_W = SWA_Q_HEADS * HEAD_DIM
SWA_KV_W = SWA_KV_HEADS * HEAD_DIM
ATT_OUT_W = SB_W + DIFF_W + SWA_Q_W

V7X_LANES = 128
V7X_MXU_DIM = 256
V7X_VMEM_BYTES = 64 * 1024 * 1024

ATT_TILE = V7X_MXU_DIM
ROW_TILE = 512
PAIR_W = 2 * HEAD_DIM
assert PAIR_W == V7X_LANES and 2 * WINDOW == ATT_TILE

P1_W = 2 * SB_W + 3 * D_MODEL
P2_W = 2 * DIFF_W + SWA_Q_W + 2 * SWA_KV_W
P3_W = SB_W + DIFF_W + SWA_KV_W
P1_CHUNK = 512
P2_CHUNK = 256
P3_CHUNKS = ((0, 256), (256, 256), (512, 256), (768, 256), (1024, 128))
assert sum(w for _, w in P3_CHUNKS) == P3_W

_NT = (((1,), (1,)), ((), ()))


def _compiler_params(semantics, vmem_bytes):
    assert vmem_bytes < V7X_VMEM_BYTES
    return pltpu.CompilerParams(dimension_semantics=semantics, vmem_limit_bytes=vmem_bytes)


def _rmsnorm_rows(x, g):
    ms = jnp.mean(x * x, axis=-1, keepdims=True)
    return x * lax.rsqrt(ms + EPS) * g


def _ffn_kernel(x_ref, g_ref, wg_ref, wu_ref, wo_ref, o_ref):
    x = x_ref[...]
    h = _rmsnorm_rows(x, g_ref[...]).astype(BF16)
    g = jnp.dot(h, wg_ref[...], preferred_element_type=F32)
    u = jnp.dot(h, wu_ref[...], preferred_element_type=F32)
    a = (g * (1.0 / (1.0 + jnp.exp(-g))) * u).astype(BF16)
    o_ref[...] = x + 0.5 * jnp.dot(a, wo_ref[...], preferred_element_type=F32)


def _ffn(xt, gain, w_in, w_out):
    t, d = xt.shape
    d_ff = w_out.shape[0]
    assert d_ff % V7X_MXU_DIM == 0 and t % ROW_TILE == 0
    w_in = w_in.astype(BF16)
    w_out = w_out.astype(BF16)
    resident = pl.Buffered(1)
    return pl.pallas_call(
        _ffn_kernel,
        name="ffn",
        grid=(t // ROW_TILE,),
        in_specs=[
            pl.BlockSpec((ROW_TILE, d), lambda i: (i, 0)),
            pl.BlockSpec((1, d), lambda i: (0, 0)),
            pl.BlockSpec((d, d_ff), lambda i: (0, 0), pipeline_mode=resident),
            pl.BlockSpec((d, d_ff), lambda i: (0, 1), pipeline_mode=resident),
            pl.BlockSpec((d_ff, d), lambda i: (0, 0), pipeline_mode=resident),
        ],
        out_specs=pl.BlockSpec((ROW_TILE, d), lambda i: (i, 0)),
        out_shape=jax.ShapeDtypeStruct((t, d), F32),
        compiler_params=_compiler_params(("parallel",), 56 * 1024 * 1024),
    )(xt, gain.reshape(1, d), w_in, w_in, w_out)


def _proj_kernel(x_ref, ng_ref, w1_ref, w2_ref, g2_ref, gm_ref, w3_ref,
                 p1_ref, p2_ref, p3_ref):
    h = _rmsnorm_rows(x_ref[...], ng_ref[...]).astype(BF16)
    for c in range(P1_W // P1_CHUNK):
        cols = slice(c * P1_CHUNK, (c + 1) * P1_CHUNK)
        p1_ref[:, cols] = jnp.dot(h, w1_ref[:, cols], preferred_element_type=F32).astype(BF16)
    y_all = jnp.dot(h, w2_ref[...], preferred_element_type=F32)
    for c in range(P2_W // P2_CHUNK):
        cols = slice(c * P2_CHUNK, (c + 1) * P2_CHUNK)
        y = y_all[:, cols]
        ms = jnp.dot((y * y).astype(BF16), gm_ref[...], preferred_element_type=F32)
        p2_ref[:, cols] = (y * lax.rsqrt(ms + EPS) * g2_ref[:, cols]).astype(BF16)
    for start, width in P3_CHUNKS:
        cols = slice(start, start + width)
        yt = jnp.dot(h, w3_ref[:, cols], preferred_element_type=F32).T
        for r in range(ROW_TILE // ATT_TILE):
            p3_ref[r, cols, :] = yt[:, r * ATT_TILE:(r + 1) * ATT_TILE].astype(BF16)


def _projection(xt, norm_gain, w1, w2, g2, gmat, w3):
    t, d = xt.shape
    const = lambda i: (0, 0)
    return pl.pallas_call(
        _proj_kernel,
        name="mixer_proj",
        grid=(t // ROW_TILE,),
        in_specs=[
            pl.BlockSpec((ROW_TILE, d), lambda i: (i, 0)),
            pl.BlockSpec((1, d), const),
            pl.BlockSpec((d, P1_W), const, pipeline_mode=pl.Buffered(1)),
            pl.BlockSpec((d, P2_W), const, pipeline_mode=pl.Buffered(1)),
            pl.BlockSpec((1, P2_W), const),
            pl.BlockSpec((P2_CHUNK, P2_CHUNK), const),
            pl.BlockSpec((d, P3_W), const, pipeline_mode=pl.Buffered(1)),
        ],
        out_specs=[
            pl.BlockSpec((ROW_TILE, P1_W), lambda i: (i, 0)),
            pl.BlockSpec((ROW_TILE, P2_W), lambda i: (i, 0)),
            pl.BlockSpec((ROW_TILE // ATT_TILE, P3_W, ATT_TILE), lambda i: (i, 0, 0)),
        ],
        out_shape=[
            jax.ShapeDtypeStruct((t, P1_W), BF16),
            jax.ShapeDtypeStruct((t, P2_W), BF16),
            jax.ShapeDtypeStruct((t // ATT_TILE, P3_W, ATT_TILE), BF16),
        ],
        compiler_params=_compiler_params(("parallel",), 52 * 1024 * 1024),
    )(xt, norm_gain.reshape(1, d), w1, w2, g2, gmat, w3)


def _t5_bucket(dist):
    n = jnp.maximum(dist, 0)
    max_exact = N_BUCKETS // 2
    nf = jnp.maximum(n, 1).astype(F32)
    large = max_exact + (jnp.log(nf / max_exact) / math.log(MAX_DISTANCE / max_exact)
                         * (N_BUCKETS - max_exact)).astype(jnp.int32)
    large = jnp.minimum(large, N_BUCKETS - 1)
    return jnp.where(n < max_exact, n, large)


def _bias_lookup(bucket, tab_ref, head):
    val = jnp.zeros(bucket.shape, F32)
    for b in range(N_BUCKETS):
        val = jnp.where(bucket == b, tab_ref[b, head], val)
    return val


def _bias_kernel(tab_ref, dbias_ref, sbias_ref):
    kk = lax.broadcasted_iota(jnp.int32, (ATT_TILE, ATT_TILE), 0)
    qq = lax.broadcasted_iota(jnp.int32, (ATT_TILE, ATT_TILE), 1)
    for v in range(2):
        dist = qq - kk + v * ATT_TILE
        bucket = _t5_bucket(dist)
        for h in range(DIFF_HEADS):
            val = (_bias_lookup(bucket, tab_ref, h) - tab_ref[N_BUCKETS - 1, h]) * LOG2E
            dbias_ref[h, v] = jnp.where(dist >= 0, val, -jnp.inf)
    kk = lax.broadcasted_iota(jnp.int32, (2 * WINDOW, WINDOW), 0)
    qq = lax.broadcasted_iota(jnp.int32, (2 * WINDOW, WINDOW), 1)
    for v in range(2):
        dist = qq - kk + (1 - v) * WINDOW
        bucket = _t5_bucket(dist)
        for h in range(SWA_Q_HEADS):
            val = _bias_lookup(bucket, tab_ref, DIFF_HEADS + h) * LOG2E
            sbias_ref[h, v] = jnp.where(dist >= 0, jnp.where(dist < WINDOW, val, -jnp.inf), -jnp.inf)


def _bias_tiles(rel_bias):
    return pl.pallas_call(
        _bias_kernel,
        name="bias_tiles",
        in_specs=[pl.BlockSpec(memory_space=pltpu.SMEM)],
        out_specs=[pl.BlockSpec(memory_space=pltpu.VMEM), pl.BlockSpec(memory_space=pltpu.VMEM)],
        out_shape=[
            jax.ShapeDtypeStruct((DIFF_HEADS, 2, ATT_TILE, ATT_TILE), F32),
            jax.ShapeDtypeStruct((SWA_Q_HEADS, 2, 2 * WINDOW, WINDOW), F32),
        ],
    )(rel_bias)


def _split_pairs(q, n_pairs):
    lane = lax.broadcasted_iota(jnp.int32, (1, PAIR_W), 1)
    low = jnp.where(lane < HEAD_DIM, 1.0, 0.0).astype(BF16)
    high = jnp.where(lane >= HEAD_DIM, 1.0, 0.0).astype(BF16)
    heads = []
    for p in range(n_pairs):
        pair = q[:, p * PAIR_W:(p + 1) * PAIR_W]
        heads.append(pair * low)
        heads.append(pair * high)
    return heads


def _neg_abs(x):
    bits = lax.bitcast_convert_type(x, jnp.uint32) | jnp.uint32(0x80000000)
    return lax.bitcast_convert_type(bits, F32)


def _lead(rounds):
    for _ in range(rounds):
        yield


def _interleave(chains):
    live = []
    started = 0
    while started < len(chains) or live:
        if started < len(chains):
            live.append(started)
            started += 1
        for n in reversed(list(live)):
            try:
                next(chains[n])
            except StopIteration:
                live.remove(n)


def _round_robin(groups):
    merged = []
    for n in range(max(len(g) for g in groups)):
        merged.extend(g[n] for g in groups if n < len(g))
    return merged


def _key_tile(ref, j):
    return ref[0, pl.ds(pl.multiple_of(j * ATT_TILE, ATT_TILE), ATT_TILE), :]


def _attn_kernel(sink_ref, qa_ref, ka_ref, qd_ref, kd_ref, qs_ref, ks_ref,
                 vta_ref, vtd_ref, vts_ref, mtri_ref, dbias_ref, sbias_ref, lam_ref, gain_ref,
                 o_ref, acc_a, z_ref, btw_ref, acc_d, sd_ref, pvd_ref, ss_ref, pvs_ref,
                 *, lam_init):
    i = pl.program_id(1)
    first = i == 0
    prev = jnp.maximum(i - 1, 0)

    qa_heads = _split_pairs(qa_ref[0], SB_HEADS // 2)
    mtri = mtri_ref[...]
    kk = lax.broadcasted_iota(jnp.int32, (ATT_TILE, ATT_TILE), 0)
    qq = lax.broadcasted_iota(jnp.int32, (ATT_TILE, ATT_TILE), 1)
    strictly_before = kk < qq

    def sb_chains(tiles, carry, lead):
        def chain(k, vt, keep, h, slot):
            k_pair = k[:, (h // 2) * PAIR_W:(h // 2 + 1) * PAIR_W]
            rows = slice(h * HEAD_DIM, (h + 1) * HEAD_DIM)
            z_ref[slot] = lax.dot_general(k_pair, qa_heads[h], _NT, preferred_element_type=F32)
            yield from _lead(lead)
            z = z_ref[slot]
            sp = jnp.maximum(z, 0.0) + jnp.log(1.0 + jnp.exp2(_neg_abs(z))) * LOG2E
            if keep is not None:
                sp = jnp.where(keep, sp, 0.0)
            after = carry[h]
            carry[h] = after - jnp.sum(sp, axis=0, keepdims=True)
            sp_b = sp.astype(BF16)
            yield
            btw_ref[slot] = jnp.dot(mtri, sp_b, preferred_element_type=F32)
            yield from _lead(lead)
            w = jnp.exp2((z - sp) + (btw_ref[slot] + after))
            if keep is not None:
                w = jnp.where(keep, w, 0.0)
            w = w.astype(BF16)
            yield
            acc_a[rows, :] += jnp.dot(vt[rows], w, preferred_element_type=F32)

        chains = []
        for t, (j, keep) in enumerate(tiles):
            k = _key_tile(ka_ref, j)
            vt = vta_ref[j]
            chains.extend(chain(k, vt, keep, h, SB_HEADS * t + h) for h in range(SB_HEADS))
        return chains

    qd_comps = _split_pairs(qd_ref[0], DIFF_HEADS)

    def diff_chains(tiles, m_run, l_run, lead):
        def chain(k, vt, variant, n, slot):
            h = n // 2
            sd_ref[slot] = lax.dot_general(k[:, h * PAIR_W:(h + 1) * PAIR_W], qd_comps[n], _NT,
                                           preferred_element_type=F32)
            yield from _lead(lead)
            s = sd_ref[slot]
            if variant is not None:
                s = s + dbias_ref[h, variant]
            m_prev = m_run[n]
            m_new = jnp.maximum(m_prev, jnp.max(s, axis=0, keepdims=True))
            alpha = jnp.exp2(m_prev - m_new)
            m_run[n] = m_new
            yield
            p = jnp.exp2(s - m_new)
            l_run[n] = alpha * l_run[n] + jnp.sum(p, axis=0, keepdims=True)
            p = p.astype(BF16)
            yield
            pvd_ref[slot] = jnp.dot(vt[h * PAIR_W:(h + 1) * PAIR_W], p, preferred_element_type=F32)
            yield from _lead(lead)
            acc_d[n] = alpha * acc_d[n] + pvd_ref[slot]

        chains = []
        for t, (j, variant) in enumerate(tiles):
            k = _key_tile(kd_ref, j)
            vt = vtd_ref[j]
            chains.extend(chain(k, vt, variant, n, 2 * DIFF_HEADS * t + n)
                          for n in range(2 * DIFF_HEADS))
        return chains

    def swa_chains(outs, lead):
        qs = qs_ref[0]
        k_hi = _key_tile(ks_ref, i)
        vt_hi = vts_ref[i]
        lo_start = pl.multiple_of(jnp.maximum(i * ATT_TILE - WINDOW, 0), WINDOW)
        k_lo = ks_ref[0, pl.ds(lo_start, 2 * WINDOW), :]
        vt_lo = jnp.where(first, vt_hi,
                          jnp.concatenate([vts_ref[prev][:, WINDOW:], vt_hi[:, :WINDOW]], axis=1))
        lo_variant = jnp.where(first, 1, 0)
        halves = ((_split_pairs(qs[:WINDOW], SWA_Q_HEADS // 2), k_lo, vt_lo, lo_variant),
                  (_split_pairs(qs[WINDOW:], SWA_Q_HEADS // 2), k_hi, vt_hi, 0))

        def chain(q_heads, k_win, vt_win, variant, h, slot):
            kv = h // SWA_GROUP
            k_pair = k_win[:, kv * PAIR_W:(kv + 1) * PAIR_W]
            ss_ref[slot] = lax.dot_general(k_pair, q_heads[h], _NT, preferred_element_type=F32)
            yield from _lead(lead)
            sink = sink_ref[h] * LOG2E
            s = ss_ref[slot] + sbias_ref[h, variant]
            m = jnp.maximum(jnp.max(s, axis=0, keepdims=True), sink)
            e = jnp.exp2(s - m)
            denom = jnp.sum(e, axis=0, keepdims=True) + jnp.exp2(sink - m)
            e = e.astype(BF16)
            yield
            pvs_ref[slot] = jnp.dot(vt_win[kv * HEAD_DIM:(kv + 1) * HEAD_DIM], e,
                                    preferred_element_type=F32)
            yield from _lead(lead)
            outs[slot] = pvs_ref[slot] / denom

        return [chain(*half, h, a * SWA_Q_HEADS + h)
                for a, half in enumerate(halves) for h in range(SWA_Q_HEADS)]

    def head_tiles(sb_tiles, diff_tiles):
        carry = [jnp.zeros((1, ATT_TILE), F32)] * SB_HEADS
        m_run = [jnp.full((1, ATT_TILE), -jnp.inf, F32)] * (2 * DIFF_HEADS)
        l_run = [jnp.zeros((1, ATT_TILE), F32)] * (2 * DIFF_HEADS)
        swa = [None] * (2 * SWA_Q_HEADS)
        _interleave(_round_robin([sb_chains(sb_tiles, carry, MXU_LEAD_MIXED),
                                  diff_chains(diff_tiles, m_run, l_run, MXU_LEAD_MIXED),
                                  swa_chains(swa, MXU_LEAD_MIXED)]))
        return tuple(carry), tuple(zip(m_run, l_run)), tuple(swa)

    acc_a[...] = jnp.zeros_like(acc_a)
    acc_d[...] = jnp.zeros_like(acc_d)
    carries, state, swa = lax.cond(
        i >= 1,
        lambda: head_tiles([(i, strictly_before), (i - 1, None)], [(i, 0), (i - 1, 1)]),
        lambda: head_tiles([(i, strictly_before)], [(i, 0)]))

    def far_tiles(tiles, st):
        m_run = [s[0] for s in st]
        l_run = [s[1] for s in st]
        _interleave(diff_chains(tiles, m_run, l_run, MXU_LEAD_LOOP))
        return tuple(zip(m_run, l_run))

    n_far = jnp.maximum(i - 1, 0)
    state = lax.fori_loop(
        0, lax.shift_right_logical(n_far, 1),
        lambda t, st: far_tiles([(i - 2 - 2 * t, None), (i - 3 - 2 * t, None)], st), state)
    state = lax.fori_loop(0, n_far & 1, lambda t, st: far_tiles([(0, None)], st), state)

    def live(c):
        j, carry = c
        return jnp.logical_and(j >= 0, jnp.max(jnp.concatenate(carry, axis=0)) > SB_DEAD_LOG2)

    def earlier_tile(c):
        j, carry = c
        carry = list(carry)
        _interleave(sb_chains([(j, None)], carry, MXU_LEAD_LOOP))
        return j - 1, tuple(carry)

    lax.while_loop(live, earlier_tile, (i - 2, carries))

    o_ref[0, :, 0:SB_W] = acc_a[...].T.astype(BF16)

    lv = lam_ref[...]
    lam = (jnp.exp(jnp.sum(lv[0:1] * lv[1:2], axis=-1, keepdims=True))
           - jnp.exp(jnp.sum(lv[2:3] * lv[3:4], axis=-1, keepdims=True)) + lam_init)
    outs = []
    for h in range(DIFF_HEADS):
        (_, l1), (_, l2) = state[2 * h], state[2 * h + 1]
        o = acc_d[2 * h] / l1 - lam * (acc_d[2 * h + 1] / l2)
        ms = jnp.mean(o * o, axis=0, keepdims=True)
        outs.append(o * lax.rsqrt(ms + EPS) * gain_ref[...] * (1.0 - lam_init))
    o_ref[0, :, SB_W:SB_W + DIFF_W] = jnp.concatenate(outs, axis=0).T.astype(BF16)

    swa_t = jnp.concatenate([jnp.concatenate(swa[:SWA_Q_HEADS], axis=0),
                             jnp.concatenate(swa[SWA_Q_HEADS:], axis=0)], axis=1)
    o_ref[0, :, SB_W + DIFF_W:ATT_OUT_W] = swa_t.T.astype(BF16)


def _attention(sinks, p1, p2, p3, mtri, dbias, sbias, lam_vecs, sub_gain, lam_init, batch, seq):
    nq = seq // ATT_TILE
    p1 = p1.reshape(batch, seq, P1_W)
    p2 = p2.reshape(batch, seq, P2_W)
    kdup_w = 2 * SWA_KV_W
    q_tile = lambda col: pl.BlockSpec((1, ATT_TILE, SB_W), lambda b, i: (b, i, col))
    keys = lambda col: pl.BlockSpec((1, seq, SB_W), lambda b, i: (b, 0, col))
    values_t = lambda rows, blk: pl.BlockSpec((nq, rows, ATT_TILE), lambda b, i: (b, blk, 0))
    whole = lambda shape: pl.BlockSpec(shape, lambda b, i: (0,) * len(shape),
                                       pipeline_mode=pl.Buffered(1))
    assert SB_W == DIFF_W == SWA_Q_W
    return pl.pallas_call(
        functools.partial(_attn_kernel, lam_init=lam_init),
        name="attention",
        grid=(batch, nq),
        in_specs=[
            pl.BlockSpec(memory_space=pltpu.SMEM),
            q_tile(0), keys(1),
            q_tile(0), keys(1),
            q_tile(2),
            pl.BlockSpec((1, seq, kdup_w), lambda b, i: (b, 0, (2 * DIFF_W + SWA_Q_W) // kdup_w)),
            values_t(SB_W, 0), values_t(DIFF_W, 1),
            values_t(SWA_KV_W, (SB_W + DIFF_W) // SWA_KV_W),
            whole((ATT_TILE, ATT_TILE)),
            whole((DIFF_HEADS, 2, ATT_TILE, ATT_TILE)),
            whole((SWA_Q_HEADS, 2, 2 * WINDOW, WINDOW)),
            whole((4, HEAD_DIM)),
            whole((PAIR_W, 1)),
        ],
        out_specs=pl.BlockSpec((1, ATT_TILE, ATT_OUT_W), lambda b, i: (b, i, 0)),
        out_shape=jax.ShapeDtypeStruct((batch, seq, ATT_OUT_W), BF16),
        scratch_shapes=[
            pltpu.VMEM((SB_W, ATT_TILE), F32),
            pltpu.VMEM((2 * SB_HEADS, ATT_TILE, ATT_TILE), F32),
            pltpu.VMEM((2 * SB_HEADS, ATT_TILE, ATT_TILE), F32),
            pltpu.VMEM((2 * DIFF_HEADS, PAIR_W, ATT_TILE), F32),
            pltpu.VMEM((4 * DIFF_HEADS, ATT_TILE, ATT_TILE), F32),
            pltpu.VMEM((4 * DIFF_HEADS, PAIR_W, ATT_TILE), F32),
            pltpu.VMEM((2 * SWA_Q_HEADS, 2 * WINDOW, WINDOW), F32),
            pltpu.VMEM((2 * SWA_Q_HEADS, HEAD_DIM, WINDOW), F32),
        ],
        compiler_params=_compiler_params(("parallel", "arbitrary"), 58 * 1024 * 1024),
    )(sinks, p1, p1, p2, p2, p2, p2, p3, p3, p3, mtri, dbias, sbias, lam_vecs,
      sub_gain.reshape(PAIR_W, 1))


def _merge_kernel(x_ref, osb_ref, odf_ref, osw_ref, ga_ref, gd_ref, gs_ref,
                  wsb_ref, wdf_ref, wsw_ref, wo_ref, o_ref):
    def branch(o_ref_, w_ref, g_ref):
        gate = 1.0 / (1.0 + jnp.exp(-g_ref[...].astype(F32)))
        return gate * jnp.dot(o_ref_[...], w_ref[...], preferred_element_type=F32)

    merged = (branch(osb_ref, wsb_ref, ga_ref) + branch(odf_ref, wdf_ref, gd_ref)
              + branch(osw_ref, wsw_ref, gs_ref))
    o_ref[...] = x_ref[...] + jnp.dot(merged.astype(BF16), wo_ref[...], preferred_element_type=F32)


def _merge(xt, att, p1, w_sb, w_diff, w_swa, w_out):
    t, d = xt.shape
    gcol = 2 * SB_W // d
    row = lambda i: (i, 0)
    const = lambda i: (0, 0)
    branch_out = lambda n: pl.BlockSpec((ROW_TILE, SB_W), lambda i: (i, n))
    return pl.pallas_call(
        _merge_kernel,
        name="merge",
        grid=(t // ROW_TILE,),
        in_specs=[
            pl.BlockSpec((ROW_TILE, d), row),
            branch_out(0), branch_out(1), branch_out(2),
            pl.BlockSpec((ROW_TILE, d), lambda i: (i, gcol)),
            pl.BlockSpec((ROW_TILE, d), lambda i: (i, gcol + 1)),
            pl.BlockSpec((ROW_TILE, d), lambda i: (i, gcol + 2)),
            pl.BlockSpec((SB_W, d), const),
            pl.BlockSpec((DIFF_W, d), const),
            pl.BlockSpec((SWA_Q_W, d), const),
            pl.BlockSpec((d, d), const),
        ],
        out_specs=pl.BlockSpec((ROW_TILE, d), row),
        out_shape=jax.ShapeDtypeStruct((t, d), F32),
        compiler_params=_compiler_params(("parallel",), 48 * 1024 * 1024),
    )(xt, att, att, att, p1, p1, p1,
      w_sb.astype(BF16), w_diff.astype(BF16), w_swa.astype(BF16), w_out.astype(BF16))


def _projection_weights(w_in, q_norm_diff, k_norm_diff, q_norm_swa, k_norm_swa):
    widths = (SB_W, SB_W, SB_W, DIFF_W, DIFF_W, DIFF_W, SWA_Q_W, SWA_KV_W, SWA_KV_W)
    offs = [0]
    for w in widths:
        offs.append(offs[-1] + w)
    qa, ka, va, qd, kd, vd, qs, ks, vs = (w_in[:, offs[n]:offs[n + 1]] for n in range(len(widths)))
    gates = w_in[:, offs[-1]:]
    qscale = SCALE * LOG2E

    def dup_heads(w):
        return jnp.concatenate([w[:, h * HEAD_DIM:(h + 1) * HEAD_DIM]
                                for h in range(SWA_KV_HEADS) for _ in range(2)], axis=1)

    w1 = jnp.concatenate([qa * qscale, ka, gates], axis=1).astype(BF16)
    w2 = jnp.concatenate([qd, kd, qs, dup_heads(ks)], axis=1).astype(BF16)
    w3 = jnp.concatenate([va, vd, vs], axis=1).astype(BF16)
    g2 = jnp.concatenate([
        jnp.tile(q_norm_diff * qscale, DIFF_W // HEAD_DIM), jnp.tile(k_norm_diff, DIFF_W // HEAD_DIM),
        jnp.tile(q_norm_swa * qscale, SWA_Q_W // HEAD_DIM), jnp.tile(k_norm_swa, 2 * SWA_KV_W // HEAD_DIM),
    ]).reshape(1, P2_W).astype(F32)
    return w1, w2, g2, w3


def _group_mean_matrix():
    g = jnp.arange(P2_CHUNK) // HEAD_DIM
    return jnp.where(g[:, None] == g[None, :], 1.0 / HEAD_DIM, 0.0).astype(BF16)


def _neg_suffix_matrix():
    idx = jnp.arange(ATT_TILE)
    return jnp.where(idx[None, :] > idx[:, None], -1.0, 0.0).astype(BF16)


def kernel(x, ffn1_norm, ffn1_w_in, ffn1_w_out, mix_norm, w_in, q_norm_diff, k_norm_diff, q_norm_swa, k_norm_swa, diff_lambda, diff_subln, swa_sinks, rel_bias, w_proj_sb, w_proj_diff, w_proj_swa, w_out, ffn2_norm, ffn2_w_in, ffn2_w_out):
    batch, seq, d = x.shape
    depth = ffn1_norm.shape[0]
    assert d == D_MODEL and seq % ATT_TILE == 0 and (batch * seq) % ROW_TILE == 0
    xt = x.reshape(batch * seq, d)
    dbias, sbias = _bias_tiles(rel_bias)
    gmat = _group_mean_matrix()
    mtri = _neg_suffix_matrix()
    for l in range(depth):
        lam_init = 0.8 - 0.6 * math.exp(-0.3 * l)
        xt = _ffn(xt, ffn1_norm[l], ffn1_w_in[l], ffn1_w_out[l])
        w1, w2, g2, w3 = _projection_weights(w_in[l], q_norm_diff[l], k_norm_diff[l],
                                             q_norm_swa[l], k_norm_swa[l])
        p1, p2, p3 = _projection(xt, mix_norm[l], w1, w2, g2, gmat, w3)
        att = _attention(swa_sinks[l], p1, p2, p3, mtri, dbias, sbias, diff_lambda[l],
                         diff_subln[l], lam_init, batch, seq)
        xt = _merge(xt, att.reshape(batch * seq, ATT_OUT_W), p1,
                    w_proj_sb[l], w_proj_diff[l], w_proj_swa[l], w_out[l])
        xt = _ffn(xt, ffn2_norm[l], ffn2_w_in[l], ffn2_w_out[l])
    return xt.reshape(batch, seq, d)
```

```python
import functools
import math

import jax
import jax.numpy as jnp
from jax import lax
from jax.experimental import pallas as pl
from jax.experimental.pallas import tpu as pltpu

F32 = jnp.float32
BF16 = jnp.bfloat16

D_MODEL = 1024
HEAD_DIM = 64
SB_HEADS = 8
DIFF_HEADS = 4
SWA_Q_HEADS = 8
SWA_KV_HEADS = 2
SWA_GROUP = SWA_Q_HEADS // SWA_KV_HEADS
WINDOW = 128
N_BUCKETS = 32
MAX_DISTANCE = 128
EPS = 1e-6
SCALE = HEAD_DIM ** -0.5
LOG2E = math.log2(math.e)
SB_DEAD_LOG2 = -160.0
MXU_LEAD_MIXED = 2
MXU_LEAD_LOOP = 4

SB_W = SB_HEADS * HEAD_DIM
DIFF_W = DIFF_HEADS * 2 * HEAD_DIM
SWA_Q_W = SWA_Q_HEADS * HEAD_DIM
SWA_KV_W = SWA_KV_HEADS * HEAD_DIM
ATT_OUT_W = SB_W + DIFF_W + SWA_Q_W

V7X_LANES = 128
V7X_MXU_DIM = 256
V7X_VMEM_BYTES = 64 * 1024 * 1024

ATT_TILE = V7X_MXU_DIM
ROW_TILE = 512
PAIR_W = 2 * HEAD_DIM
assert PAIR_W == V7X_LANES and 2 * WINDOW == ATT_TILE

P1_W = 2 * SB_W + 3 * D_MODEL
P2_W = 2 * DIFF_W + SWA_Q_W + 2 * SWA_KV_W
P3_W = SB_W + DIFF_W + SWA_KV_W
P1_CHUNK = 512
P2_CHUNK = 256
P3_CHUNKS = ((0, 256), (256, 256), (512, 256), (768, 256), (1024, 128))
assert sum(w for _, w in P3_CHUNKS) == P3_W

_NT = (((1,), (1,)), ((), ()))


def _compiler_params(semantics, vmem_bytes):
    assert vmem_bytes < V7X_VMEM_BYTES
    return pltpu.CompilerParams(dimension_semantics=semantics, vmem_limit_bytes=vmem_bytes)


def _rmsnorm_rows(x, g):
    ms = jnp.mean(x * x, axis=-1, keepdims=True)
    return x * lax.rsqrt(ms + EPS) * g


def _ffn_kernel(x_ref, g_ref, wg_ref, wu_ref, wo_ref, o_ref):
    x = x_ref[...]
    h = _rmsnorm_rows(x, g_ref[...]).astype(BF16)
    g = jnp.dot(h, wg_ref[...], preferred_element_type=F32)
    u = jnp.dot(h, wu_ref[...], preferred_element_type=F32)
    a = (g * (1.0 / (1.0 + jnp.exp(-g))) * u).astype(BF16)
    o_ref[...] = x + 0.5 * jnp.dot(a, wo_ref[...], preferred_element_type=F32)


def _ffn(xt, gain, w_in, w_out):
    t, d = xt.shape
    d_ff = w_out.shape[0]
    assert d_ff % V7X_MXU_DIM == 0 and t % ROW_TILE == 0
    w_in = w_in.astype(BF16)
    w_out = w_out.astype(BF16)
    resident = pl.Buffered(1)
    return pl.pallas_call(
        _ffn_kernel,
        name="ffn",
        grid=(t // ROW_TILE,),
        in_specs=[
            pl.BlockSpec((ROW_TILE, d), lambda i: (i, 0)),
            pl.BlockSpec((1, d), lambda i: (0, 0)),
            pl.BlockSpec((d, d_ff), lambda i: (0, 0), pipeline_mode=resident),
            pl.BlockSpec((d, d_ff), lambda i: (0, 1), pipeline_mode=resident),
            pl.BlockSpec((d_ff, d), lambda i: (0, 0), pipeline_mode=resident),
        ],
        out_specs=pl.BlockSpec((ROW_TILE, d), lambda i: (i, 0)),
        out_shape=jax.ShapeDtypeStruct((t, d), F32),
        compiler_params=_compiler_params(("parallel",), 56 * 1024 * 1024),
    )(xt, gain.reshape(1, d), w_in, w_in, w_out)


def _proj_kernel(x_ref, ng_ref, w1_ref, w2_ref, g2_ref, gm_ref, w3_ref,
                 p1_ref, p2_ref, p3_ref):
    h = _rmsnorm_rows(x_ref[...], ng_ref[...]).astype(BF16)
    for c in range(P1_W // P1_CHUNK):
        cols = slice(c * P1_CHUNK, (c + 1) * P1_CHUNK)
        p1_ref[:, cols] = jnp.dot(h, w1_ref[:, cols], preferred_element_type=F32).astype(BF16)
    y_all = jnp.dot(h, w2_ref[...], preferred_element_type=F32)
    for c in range(P2_W // P2_CHUNK):
        cols = slice(c * P2_CHUNK, (c + 1) * P2_CHUNK)
        y = y_all[:, cols]
        ms = jnp.dot((y * y).astype(BF16), gm_ref[...], preferred_element_type=F32)
        p2_ref[:, cols] = (y * lax.rsqrt(ms + EPS) * g2_ref[:, cols]).astype(BF16)
    for start, width in P3_CHUNKS:
        cols = slice(start, start + width)
        yt = jnp.dot(h, w3_ref[:, cols], preferred_element_type=F32).T
        for r in range(ROW_TILE // ATT_TILE):
            p3_ref[r, cols, :] = yt[:, r * ATT_TILE:(r + 1) * ATT_TILE].astype(BF16)


def _projection(xt, norm_gain, w1, w2, g2, gmat, w3):
    t, d = xt.shape
    const = lambda i: (0, 0)
    return pl.pallas_call(
        _proj_kernel,
        name="mixer_proj",
        grid=(t // ROW_TILE,),
        in_specs=[
            pl.BlockSpec((ROW_TILE, d), lambda i: (i, 0)),
            pl.BlockSpec((1, d), const),
            pl.BlockSpec((d, P1_W), const, pipeline_mode=pl.Buffered(1)),
            pl.BlockSpec((d, P2_W), const, pipeline_mode=pl.Buffered(1)),
            pl.BlockSpec((1, P2_W), const),
            pl.BlockSpec((P2_CHUNK, P2_CHUNK), const),
            pl.BlockSpec((d, P3_W), const, pipeline_mode=pl.Buffered(1)),
        ],
        out_specs=[
            pl.BlockSpec((ROW_TILE, P1_W), lambda i: (i, 0)),
            pl.BlockSpec((ROW_TILE, P2_W), lambda i: (i, 0)),
            pl.BlockSpec((ROW_TILE // ATT_TILE, P3_W, ATT_TILE), lambda i: (i, 0, 0)),
        ],
        out_shape=[
            jax.ShapeDtypeStruct((t, P1_W), BF16),
            jax.ShapeDtypeStruct((t, P2_W), BF16),
            jax.ShapeDtypeStruct((t // ATT_TILE, P3_W, ATT_TILE), BF16),
        ],
        compiler_params=_compiler_params(("parallel",), 52 * 1024 * 1024),
    )(xt, norm_gain.reshape(1, d), w1, w2, g2, gmat, w3)


def _t5_bucket(dist):
    n = jnp.maximum(dist, 0)
    max_exact = N_BUCKETS // 2
    nf = jnp.maximum(n, 1).astype(F32)
    large = max_exact + (jnp.log(nf / max_exact) / math.log(MAX_DISTANCE / max_exact)
                         * (N_BUCKETS - max_exact)).astype(jnp.int32)
    large = jnp.minimum(large, N_BUCKETS - 1)
    return jnp.where(n < max_exact, n, large)


def _bias_lookup(bucket, tab_ref, head):
    val = jnp.zeros(bucket.shape, F32)
    for b in range(N_BUCKETS):
        val = jnp.where(bucket == b, tab_ref[b, head], val)
    return val


def _bias_kernel(tab_ref, dbias_ref, sbias_ref):
    kk = lax.broadcasted_iota(jnp.int32, (ATT_TILE, ATT_TILE), 0)
    qq = lax.broadcasted_iota(jnp.int32, (ATT_TILE, ATT_TILE), 1)
    for v in range(2):
        dist = qq - kk + v * ATT_TILE
        bucket = _t5_bucket(dist)
        for h in range(DIFF_HEADS):
            val = (_bias_lookup(bucket, tab_ref, h) - tab_ref[N_BUCKETS - 1, h]) * LOG2E
            dbias_ref[h, v] = jnp.where(dist >= 0, val, -jnp.inf)
    kk = lax.broadcasted_iota(jnp.int32, (2 * WINDOW, WINDOW), 0)
    qq = lax.broadcasted_iota(jnp.int32, (2 * WINDOW, WINDOW), 1)
    for v in range(2):
        dist = qq - kk + (1 - v) * WINDOW
        bucket = _t5_bucket(dist)
        for h in range(SWA_Q_HEADS):
            val = _bias_lookup(bucket, tab_ref, DIFF_HEADS + h) * LOG2E
            sbias_ref[h, v] = jnp.where(dist >= 0, jnp.where(dist < WINDOW, val, -jnp.inf), -jnp.inf)


def _bias_tiles(rel_bias):
    return pl.pallas_call(
        _bias_kernel,
        name="bias_tiles",
        in_specs=[pl.BlockSpec(memory_space=pltpu.SMEM)],
        out_specs=[pl.BlockSpec(memory_space=pltpu.VMEM), pl.BlockSpec(memory_space=pltpu.VMEM)],
        out_shape=[
            jax.ShapeDtypeStruct((DIFF_HEADS, 2, ATT_TILE, ATT_TILE), F32),
            jax.ShapeDtypeStruct((SWA_Q_HEADS, 2, 2 * WINDOW, WINDOW), F32),
        ],
    )(rel_bias)


def _split_pairs(q, n_pairs):
    lane = lax.broadcasted_iota(jnp.int32, (1, PAIR_W), 1)
    low = jnp.where(lane < HEAD_DIM, 1.0, 0.0).astype(BF16)
    high = jnp.where(lane >= HEAD_DIM, 1.0, 0.0).astype(BF16)
    heads = []
    for p in range(n_pairs):
        pair = q[:, p * PAIR_W:(p + 1) * PAIR_W]
        heads.append(pair * low)
        heads.append(pair * high)
    return heads


def _neg_abs(x):
    bits = lax.bitcast_convert_type(x, jnp.uint32) | jnp.uint32(0x80000000)
    return lax.bitcast_convert_type(bits, F32)


def _lead(rounds):
    for _ in range(rounds):
        yield


def _interleave(chains):
    live = []
    started = 0
    while started < len(chains) or live:
        if started < len(chains):
            live.append(started)
            started += 1
        for n in reversed(list(live)):
            try:
                next(chains[n])
            except StopIteration:
                live.remove(n)


def _round_robin(groups):
    merged = []
    for n in range(max(len(g) for g in groups)):
        merged.extend(g[n] for g in groups if n < len(g))
    return merged


def _key_tile(ref, j):
    return ref[0, pl.ds(pl.multiple_of(j * ATT_TILE, ATT_TILE), ATT_TILE), :]


def _attn_kernel(sink_ref, qa_ref, ka_ref, qd_ref, kd_ref, qs_ref, ks_ref,
                 vta_ref, vtd_ref, vts_ref, mtri_ref, dbias_ref, sbias_ref, lam_ref, gain_ref,
                 o_ref, acc_a, z_ref, btw_ref, acc_d, sd_ref, pvd_ref, ss_ref, pvs_ref,
                 *, lam_init):
    i = pl.program_id(1)
    first = i == 0
    prev = jnp.maximum(i - 1, 0)

    qa_heads = _split_pairs(qa_ref[0], SB_HEADS // 2)
    mtri = mtri_ref[...]
    kk = lax.broadcasted_iota(jnp.int32, (ATT_TILE, ATT_TILE), 0)
    qq = lax.broadcasted_iota(jnp.int32, (ATT_TILE, ATT_TILE), 1)
    strictly_before = kk < qq

    def sb_chains(tiles, carry, lead):
        def chain(k, vt, keep, fresh, h, slot):
            k_pair = k[:, (h // 2) * PAIR_W:(h // 2 + 1) * PAIR_W]
            rows = slice(h * HEAD_DIM, (h + 1) * HEAD_DIM)
            z_ref[slot] = lax.dot_general(k_pair, qa_heads[h], _NT, preferred_element_type=F32)
            yield from _lead(lead)
            z = z_ref[slot]
            sp = jnp.maximum(z, 0.0) + jnp.log(1.0 + jnp.exp2(_neg_abs(z))) * LOG2E
            if keep is not None:
                sp = jnp.where(keep, sp, 0.0)
            sp_b = sp.astype(BF16)
            yield
            btw_ref[slot] = jnp.dot(mtri, sp_b, preferred_element_type=F32)
            yield from _lead(lead)
            between = btw_ref[slot]
            after = carry[h]
            carry[h] = after + (between[0:1, :] - sp_b[0:1, :].astype(F32))
            w = jnp.exp2((z - sp) + between)
            if keep is not None:
                w = jnp.where(keep, w, 0.0)
            w = w.astype(BF16)
            yield
            pv = jnp.dot(vt[rows], w, preferred_element_type=F32)
            if fresh:
                acc_a[rows, :] = pv
            else:
                acc_a[rows, :] += pv * jnp.exp2(after)

        chains = []
        for t, (j, keep, fresh) in enumerate(tiles):
            k = _key_tile(ka_ref, j)
            vt = vta_ref[j]
            chains.extend(chain(k, vt, keep, fresh, h, SB_HEADS * t + h) for h in range(SB_HEADS))
        return chains

    qd_comps = _split_pairs(qd_ref[0], DIFF_HEADS)

    def diff_chains(tiles, m_run, l_run, lead):
        def chain(k, vt, variant, fresh, n, slot):
            h = n // 2
            sd_ref[slot] = lax.dot_general(k[:, h * PAIR_W:(h + 1) * PAIR_W], qd_comps[n], _NT,
                                           preferred_element_type=F32)
            yield from _lead(lead)
            s = sd_ref[slot]
            if variant is not None:
                s = s + dbias_ref[h, variant]
            m_prev = m_run[n]
            m_new = jnp.maximum(m_prev, jnp.max(s, axis=0, keepdims=True))
            alpha = jnp.exp2(m_prev - m_new)
            m_run[n] = m_new
            yield
            p = jnp.exp2(s - m_new)
            l_run[n] = alpha * l_run[n] + jnp.sum(p, axis=0, keepdims=True)
            p = p.astype(BF16)
            yield
            pvd_ref[slot] = jnp.dot(vt[h * PAIR_W:(h + 1) * PAIR_W], p, preferred_element_type=F32)
            yield from _lead(lead)
            acc_d[n] = pvd_ref[slot] if fresh else alpha * acc_d[n] + pvd_ref[slot]

        chains = []
        for t, (j, variant, fresh) in enumerate(tiles):
            k = _key_tile(kd_ref, j)
            vt = vtd_ref[j]
            chains.extend(chain(k, vt, variant, fresh, n, 2 * DIFF_HEADS * t + n)
                          for n in range(2 * DIFF_HEADS))
        return chains

    def swa_chains(outs, lead):
        qs = qs_ref[0]
        k_hi = _key_tile(ks_ref, i)
        vt_hi = vts_ref[i]
        lo_start = pl.multiple_of(jnp.maximum(i * ATT_TILE - WINDOW, 0), WINDOW)
        k_lo = ks_ref[0, pl.ds(lo_start, 2 * WINDOW), :]
        vt_lo = jnp.where(first, vt_hi,
                          jnp.concatenate([vts_ref[prev][:, WINDOW:], vt_hi[:, :WINDOW]], axis=1))
        lo_variant = jnp.where(first, 1, 0)
        halves = ((_split_pairs(qs[:WINDOW], SWA_Q_HEADS // 2), k_lo, vt_lo, lo_variant),
                  (_split_pairs(qs[WINDOW:], SWA_Q_HEADS // 2), k_hi, vt_hi, 0))

        def chain(q_heads, k_win, vt_win, variant, h, slot):
            kv = h // SWA_GROUP
            k_pair = k_win[:, kv * PAIR_W:(kv + 1) * PAIR_W]
            ss_ref[slot] = lax.dot_general(k_pair, q_heads[h], _NT, preferred_element_type=F32)
            yield from _lead(lead)
            sink = sink_ref[h] * LOG2E
            s = ss_ref[slot] + sbias_ref[h, variant]
            m = jnp.maximum(jnp.max(s, axis=0, keepdims=True), sink)
            e = jnp.exp2(s - m)
            denom = jnp.sum(e, axis=0, keepdims=True) + jnp.exp2(sink - m)
            e = e.astype(BF16)
            yield
            pvs_ref[slot] = jnp.dot(vt_win[kv * HEAD_DIM:(kv + 1) * HEAD_DIM], e,
                                    preferred_element_type=F32)
            yield from _lead(lead)
            outs[slot] = pvs_ref[slot] / denom

        return [chain(*half, h, a * SWA_Q_HEADS + h)
                for a, half in enumerate(halves) for h in range(SWA_Q_HEADS)]

    def head_tiles(sb_tiles, diff_tiles):
        carry = [jnp.zeros((1, ATT_TILE), F32)] * SB_HEADS
        m_run = [jnp.full((1, ATT_TILE), -jnp.inf, F32)] * (2 * DIFF_HEADS)
        l_run = [jnp.zeros((1, ATT_TILE), F32)] * (2 * DIFF_HEADS)
        swa = [None] * (2 * SWA_Q_HEADS)
        _interleave(_round_robin([sb_chains(sb_tiles, carry, MXU_LEAD_MIXED),
                                  diff_chains(diff_tiles, m_run, l_run, MXU_LEAD_MIXED),
                                  swa_chains(swa, MXU_LEAD_MIXED)]))
        return tuple(carry), tuple(zip(m_run, l_run)), tuple(swa)

    carries, state, swa = lax.cond(
        i >= 1,
        lambda: head_tiles([(i, strictly_before, True), (i - 1, None, False)],
                           [(i, 0, True), (i - 1, 1, False)]),
        lambda: head_tiles([(i, strictly_before, True)], [(i, 0, True)]))

    def far_tiles(tiles, st):
        m_run = [s[0] for s in st]
        l_run = [s[1] for s in st]
        _interleave(diff_chains(tiles, m_run, l_run, MXU_LEAD_LOOP))
        return tuple(zip(m_run, l_run))

    n_far = jnp.maximum(i - 1, 0)
    state = lax.fori_loop(
        0, lax.shift_right_logical(n_far, 1),
        lambda t, st: far_tiles([(i - 2 - 2 * t, None, False), (i - 3 - 2 * t, None, False)], st), state)
    state = lax.fori_loop(0, n_far & 1, lambda t, st: far_tiles([(0, None, False)], st), state)

    def live(c):
        j, carry = c
        return jnp.logical_and(j >= 0, jnp.max(jnp.concatenate(carry, axis=0)) > SB_DEAD_LOG2)

    def earlier_tile(c):
        j, carry = c
        carry = list(carry)
        _interleave(sb_chains([(j, None, False)], carry, MXU_LEAD_LOOP))
        return j - 1, tuple(carry)

    lax.while_loop(live, earlier_tile, (i - 2, carries))

    o_ref[0, :, 0:SB_W] = acc_a[...].T.astype(BF16)

    lv = lam_ref[...]
    lam = (jnp.exp(jnp.sum(lv[0:1] * lv[1:2], axis=-1, keepdims=True))
           - jnp.exp(jnp.sum(lv[2:3] * lv[3:4], axis=-1, keepdims=True)) + lam_init)
    outs = []
    for h in range(DIFF_HEADS):
        (_, l1), (_, l2) = state[2 * h], state[2 * h + 1]
        o = acc_d[2 * h] / l1 - lam * (acc_d[2 * h + 1] / l2)
        ms = jnp.mean(o * o, axis=0, keepdims=True)
        outs.append(o * lax.rsqrt(ms + EPS) * gain_ref[...] * (1.0 - lam_init))
    o_ref[0, :, SB_W:SB_W + DIFF_W] = jnp.concatenate(outs, axis=0).T.astype(BF16)

    swa_t = jnp.concatenate([jnp.concatenate(swa[:SWA_Q_HEADS], axis=0),
                             jnp.concatenate(swa[SWA_Q_HEADS:], axis=0)], axis=1)
    o_ref[0, :, SB_W + DIFF_W:ATT_OUT_W] = swa_t.T.astype(BF16)


def _attention(sinks, p1, p2, p3, mtri, dbias, sbias, lam_vecs, sub_gain, lam_init, batch, seq):
    nq = seq // ATT_TILE
    p1 = p1.reshape(batch, seq, P1_W)
    p2 = p2.reshape(batch, seq, P2_W)
    kdup_w = 2 * SWA_KV_W
    q_tile = lambda col: pl.BlockSpec((1, ATT_TILE, SB_W), lambda b, i: (b, i, col))
    keys = lambda col: pl.BlockSpec((1, seq, SB_W), lambda b, i: (b, 0, col))
    values_t = lambda rows, blk: pl.BlockSpec((nq, rows, ATT_TILE), lambda b, i: (b, blk, 0))
    whole = lambda shape: pl.BlockSpec(shape, lambda b, i: (0,) * len(shape),
                                       pipeline_mode=pl.Buffered(1))
    assert SB_W == DIFF_W == SWA_Q_W
    return pl.pallas_call(
        functools.partial(_attn_kernel, lam_init=lam_init),
        name="attention",
        grid=(batch, nq),
        in_specs=[
            pl.BlockSpec(memory_space=pltpu.SMEM),
            q_tile(0), keys(1),
            q_tile(0), keys(1),
            q_tile(2),
            pl.BlockSpec((1, seq, kdup_w), lambda b, i: (b, 0, (2 * DIFF_W + SWA_Q_W) // kdup_w)),
            values_t(SB_W, 0), values_t(DIFF_W, 1),
            values_t(SWA_KV_W, (SB_W + DIFF_W) // SWA_KV_W),
            whole((ATT_TILE, ATT_TILE)),
            whole((DIFF_HEADS, 2, ATT_TILE, ATT_TILE)),
            whole((SWA_Q_HEADS, 2, 2 * WINDOW, WINDOW)),
            whole((4, HEAD_DIM)),
            whole((PAIR_W, 1)),
        ],
        out_specs=pl.BlockSpec((1, ATT_TILE, ATT_OUT_W), lambda b, i: (b, i, 0)),
        out_shape=jax.ShapeDtypeStruct((batch, seq, ATT_OUT_W), BF16),
        scratch_shapes=[
            pltpu.VMEM((SB_W, ATT_TILE), F32),
            pltpu.VMEM((2 * SB_HEADS, ATT_TILE, ATT_TILE), F32),
            pltpu.VMEM((2 * SB_HEADS, ATT_TILE, ATT_TILE), F32),
            pltpu.VMEM((2 * DIFF_HEADS, PAIR_W, ATT_TILE), F32),
            pltpu.VMEM((4 * DIFF_HEADS, ATT_TILE, ATT_TILE), F32),
            pltpu.VMEM((4 * DIFF_HEADS, PAIR_W, ATT_TILE), F32),
            pltpu.VMEM((2 * SWA_Q_HEADS, 2 * WINDOW, WINDOW), F32),
            pltpu.VMEM((2 * SWA_Q_HEADS, HEAD_DIM, WINDOW), F32),
        ],
        compiler_params=_compiler_params(("parallel", "arbitrary"), 58 * 1024 * 1024),
    )(sinks, p1, p1, p2, p2, p2, p2, p3, p3, p3, mtri, dbias, sbias, lam_vecs,
      sub_gain.reshape(PAIR_W, 1))


def _merge_kernel(x_ref, osb_ref, odf_ref, osw_ref, ga_ref, gd_ref, gs_ref,
                  wsb_ref, wdf_ref, wsw_ref, wo_ref, o_ref):
    def branch(o_ref_, w_ref, g_ref):
        gate = 1.0 / (1.0 + jnp.exp(-g_ref[...].astype(F32)))
        return gate * jnp.dot(o_ref_[...], w_ref[...], preferred_element_type=F32)

    merged = (branch(osb_ref, wsb_ref, ga_ref) + branch(odf_ref, wdf_ref, gd_ref)
              + branch(osw_ref, wsw_ref, gs_ref))
    o_ref[...] = x_ref[...] + jnp.dot(merged.astype(BF16), wo_ref[...], preferred_element_type=F32)


def _merge(xt, att, p1, w_sb, w_diff, w_swa, w_out):
    t, d = xt.shape
    gcol = 2 * SB_W // d
    row = lambda i: (i, 0)
    const = lambda i: (0, 0)
    branch_out = lambda n: pl.BlockSpec((ROW_TILE, SB_W), lambda i: (i, n))
    return pl.pallas_call(
        _merge_kernel,
        name="merge",
        grid=(t // ROW_TILE,),
        in_specs=[
            pl.BlockSpec((ROW_TILE, d), row),
            branch_out(0), branch_out(1), branch_out(2),
            pl.BlockSpec((ROW_TILE, d), lambda i: (i, gcol)),
            pl.BlockSpec((ROW_TILE, d), lambda i: (i, gcol + 1)),
            pl.BlockSpec((ROW_TILE, d), lambda i: (i, gcol + 2)),
            pl.BlockSpec((SB_W, d), const),
            pl.BlockSpec((DIFF_W, d), const),
            pl.BlockSpec((SWA_Q_W, d), const),
            pl.BlockSpec((d, d), const),
        ],
        out_specs=pl.BlockSpec((ROW_TILE, d), row),
        out_shape=jax.ShapeDtypeStruct((t, d), F32),
        compiler_params=_compiler_params(("parallel",), 48 * 1024 * 1024),
    )(xt, att, att, att, p1, p1, p1,
      w_sb.astype(BF16), w_diff.astype(BF16), w_swa.astype(BF16), w_out.astype(BF16))


def _projection_weights(w_in, q_norm_diff, k_norm_diff, q_norm_swa, k_norm_swa):
    widths = (SB_W, SB_W, SB_W, DIFF_W, DIFF_W, DIFF_W, SWA_Q_W, SWA_KV_W, SWA_KV_W)
    offs = [0]
    for w in widths:
        offs.append(offs[-1] + w)
    qa, ka, va, qd, kd, vd, qs, ks, vs = (w_in[:, offs[n]:offs[n + 1]] for n in range(len(widths)))
    gates = w_in[:, offs[-1]:]
    qscale = SCALE * LOG2E

    def dup_heads(w):
        return jnp.concatenate([w[:, h * HEAD_DIM:(h + 1) * HEAD_DIM]
                                for h in range(SWA_KV_HEADS) for _ in range(2)], axis=1)

    w1 = jnp.concatenate([qa * qscale, ka, gates], axis=1).astype(BF16)
    w2 = jnp.concatenate([qd, kd, qs, dup_heads(ks)], axis=1).astype(BF16)
    w3 = jnp.concatenate([va, vd, vs], axis=1).astype(BF16)
    g2 = jnp.concatenate([
        jnp.tile(q_norm_diff * qscale, DIFF_W // HEAD_DIM), jnp.tile(k_norm_diff, DIFF_W // HEAD_DIM),
        jnp.tile(q_norm_swa * qscale, SWA_Q_W // HEAD_DIM), jnp.tile(k_norm_swa, 2 * SWA_KV_W // HEAD_DIM),
    ]).reshape(1, P2_W).astype(F32)
    return w1, w2, g2, w3


def _group_mean_matrix():
    g = jnp.arange(P2_CHUNK) // HEAD_DIM
    return jnp.where(g[:, None] == g[None, :], 1.0 / HEAD_DIM, 0.0).astype(BF16)


def _neg_suffix_matrix():
    idx = jnp.arange(ATT_TILE)
    return jnp.where(idx[None, :] > idx[:, None], -1.0, 0.0).astype(BF16)


def kernel(x, ffn1_norm, ffn1_w_in, ffn1_w_out, mix_norm, w_in, q_norm_diff, k_norm_diff, q_norm_swa, k_norm_swa, diff_lambda, diff_subln, swa_sinks, rel_bias, w_proj_sb, w_proj_diff, w_proj_swa, w_out, ffn2_norm, ffn2_w_in, ffn2_w_out):
    batch, seq, d = x.shape
    depth = ffn1_norm.shape[0]
    assert d == D_MODEL and seq % ATT_TILE == 0 and (batch * seq) % ROW_TILE == 0
    xt = x.reshape(batch * seq, d)
    dbias, sbias = _bias_tiles(rel_bias)
    gmat = _group_mean_matrix()
    mtri = _neg_suffix_matrix()
    for l in range(depth):
        lam_init = 0.8 - 0.6 * math.exp(-0.3 * l)
        xt = _ffn(xt, ffn1_norm[l], ffn1_w_in[l], ffn1_w_out[l])
        w1, w2, g2, w3 = _projection_weights(w_in[l], q_norm_diff[l], k_norm_diff[l],
                                             q_norm_swa[l], k_norm_swa[l])
        p1, p2, p3 = _projection(xt, mix_norm[l], w1, w2, g2, gmat, w3)
        att = _attention(swa_sinks[l], p1, p2, p3, mtri, dbias, sbias, diff_lambda[l],
                         diff_subln[l], lam_init, batch, seq)
        xt = _merge(xt, att.reshape(batch * seq, ATT_OUT_W), p1,
                    w_proj_sb[l], w_proj_diff[l], w_proj_swa[l], w_out[l])
        xt = _ffn(xt, ffn2_norm[l], ffn2_w_in[l], ffn2_w_out[l])
    return xt.reshape(batch, seq, d)
```

```python
import functools
import math

import jax
import jax.numpy as jnp
from jax import lax
from jax.experimental import pallas as pl
from jax.experimental.pallas import tpu as pltpu

F32 = jnp.float32
BF16 = jnp.bfloat16

D_MODEL = 1024
HEAD_DIM = 64
SB_HEADS = 8
DIFF_HEADS = 4
SWA_Q_HEADS = 8
SWA_KV_HEADS = 2
SWA_GROUP = SWA_Q_HEADS // SWA_KV_HEADS
WINDOW = 128
N_BUCKETS = 32
MAX_DISTANCE = 128
EPS = 1e-6
SCALE = HEAD_DIM ** -0.5
LOG2E = math.log2(math.e)
SB_DEAD_LOG2 = -160.0
MXU_LEAD_MIXED = 2
MXU_LEAD_LOOP = 4

SB_W = SB_HEADS * HEAD_DIM
DIFF_W = DIFF_HEADS * 2 * HEAD_DIM
SWA_Q_W = SWA_Q_HEADS * HEAD_DIM
SWA_KV_W = SWA_KV_HEADS * HEAD_DIM
ATT_OUT_W = SB_W + DIFF_W + SWA_Q_W

V7X_LANES = 128
V7X_MXU_DIM = 256
V7X_VMEM_BYTES = 64 * 1024 * 1024

ATT_TILE = V7X_MXU_DIM
ROW_TILE = 512
PAIR_W = 2 * HEAD_DIM
assert PAIR_W == V7X_LANES and 2 * WINDOW == ATT_TILE

P1_W = 2 * SB_W + 3 * D_MODEL
P2_W = 2 * DIFF_W + SWA_Q_W + 2 * SWA_KV_W
P3_W = SB_W + DIFF_W + SWA_KV_W
P1_CHUNK = 512
P2_CHUNK = 256
P3_CHUNKS = ((0, 256), (256, 256), (512, 256), (768, 256), (1024, 128))
assert sum(w for _, w in P3_CHUNKS) == P3_W
MERGE_FFN_CHUNKS = ((0, 6 * V7X_MXU_DIM), (6 * V7X_MXU_DIM, 5 * V7X_MXU_DIM))

_NT = (((1,), (1,)), ((), ()))


def _compiler_params(semantics, vmem_bytes):
    assert vmem_bytes < V7X_VMEM_BYTES
    return pltpu.CompilerParams(dimension_semantics=semantics, vmem_limit_bytes=vmem_bytes)


def _rmsnorm_rows(x, g):
    ms = jnp.mean(x * x, axis=-1, keepdims=True)
    return x * lax.rsqrt(ms + EPS) * g


def _ffn_kernel(x_ref, g_ref, wg_ref, wu_ref, wo_ref, o_ref):
    x = x_ref[...]
    h = _rmsnorm_rows(x, g_ref[...]).astype(BF16)
    g = jnp.dot(h, wg_ref[...], preferred_element_type=F32)
    u = jnp.dot(h, wu_ref[...], preferred_element_type=F32)
    a = (g * (1.0 / (1.0 + jnp.exp(-g))) * u).astype(BF16)
    o_ref[...] = x + 0.5 * jnp.dot(a, wo_ref[...], preferred_element_type=F32)


def _ffn(xt, gain, w_in, w_out):
    t, d = xt.shape
    d_ff = w_out.shape[0]
    assert d_ff % V7X_MXU_DIM == 0 and t % ROW_TILE == 0
    w_in = w_in.astype(BF16)
    w_out = w_out.astype(BF16)
    resident = pl.Buffered(1)
    return pl.pallas_call(
        _ffn_kernel,
        name="ffn",
        grid=(t // ROW_TILE,),
        in_specs=[
            pl.BlockSpec((ROW_TILE, d), lambda i: (i, 0)),
            pl.BlockSpec((1, d), lambda i: (0, 0)),
            pl.BlockSpec((d, d_ff), lambda i: (0, 0), pipeline_mode=resident),
            pl.BlockSpec((d, d_ff), lambda i: (0, 1), pipeline_mode=resident),
            pl.BlockSpec((d_ff, d), lambda i: (0, 0), pipeline_mode=resident),
        ],
        out_specs=pl.BlockSpec((ROW_TILE, d), lambda i: (i, 0)),
        out_shape=jax.ShapeDtypeStruct((t, d), F32),
        compiler_params=_compiler_params(("parallel",), 56 * 1024 * 1024),
    )(xt, gain.reshape(1, d), w_in, w_in, w_out)


def _proj_kernel(x_ref, ng_ref, w1_ref, w2_ref, g2_ref, gm_ref, w3_ref,
                 p1_ref, p2_ref, p3_ref):
    h = _rmsnorm_rows(x_ref[...], ng_ref[...]).astype(BF16)
    for c in range(P1_W // P1_CHUNK):
        cols = slice(c * P1_CHUNK, (c + 1) * P1_CHUNK)
        p1_ref[:, cols] = jnp.dot(h, w1_ref[:, cols], preferred_element_type=F32).astype(BF16)
    y_all = jnp.dot(h, w2_ref[...], preferred_element_type=F32)
    for c in range(P2_W // P2_CHUNK):
        cols = slice(c * P2_CHUNK, (c + 1) * P2_CHUNK)
        y = y_all[:, cols]
        ms = jnp.dot((y * y).astype(BF16), gm_ref[...], preferred_element_type=F32)
        p2_ref[:, cols] = (y * lax.rsqrt(ms + EPS) * g2_ref[:, cols]).astype(BF16)
    for start, width in P3_CHUNKS:
        cols = slice(start, start + width)
        yt = jnp.dot(h, w3_ref[:, cols], preferred_element_type=F32).T
        for r in range(ROW_TILE // ATT_TILE):
            p3_ref[r, cols, :] = yt[:, r * ATT_TILE:(r + 1) * ATT_TILE].astype(BF16)


def _projection(xt, norm_gain, w1, w2, g2, gmat, w3):
    t, d = xt.shape
    const = lambda i: (0, 0)
    return pl.pallas_call(
        _proj_kernel,
        name="mixer_proj",
        grid=(t // ROW_TILE,),
        in_specs=[
            pl.BlockSpec((ROW_TILE, d), lambda i: (i, 0)),
            pl.BlockSpec((1, d), const),
            pl.BlockSpec((d, P1_W), const, pipeline_mode=pl.Buffered(1)),
            pl.BlockSpec((d, P2_W), const, pipeline_mode=pl.Buffered(1)),
            pl.BlockSpec((1, P2_W), const),
            pl.BlockSpec((P2_CHUNK, P2_CHUNK), const),
            pl.BlockSpec((d, P3_W), const, pipeline_mode=pl.Buffered(1)),
        ],
        out_specs=[
            pl.BlockSpec((ROW_TILE, P1_W), lambda i: (i, 0)),
            pl.BlockSpec((ROW_TILE, P2_W), lambda i: (i, 0)),
            pl.BlockSpec((ROW_TILE // ATT_TILE, P3_W, ATT_TILE), lambda i: (i, 0, 0)),
        ],
        out_shape=[
            jax.ShapeDtypeStruct((t, P1_W), BF16),
            jax.ShapeDtypeStruct((t, P2_W), BF16),
            jax.ShapeDtypeStruct((t // ATT_TILE, P3_W, ATT_TILE), BF16),
        ],
        compiler_params=_compiler_params(("parallel",), 52 * 1024 * 1024),
    )(xt, norm_gain.reshape(1, d), w1, w2, g2, gmat, w3)


def _t5_bucket(dist):
    n = jnp.maximum(dist, 0)
    max_exact = N_BUCKETS // 2
    nf = jnp.maximum(n, 1).astype(F32)
    large = max_exact + (jnp.log(nf / max_exact) / math.log(MAX_DISTANCE / max_exact)
                         * (N_BUCKETS - max_exact)).astype(jnp.int32)
    large = jnp.minimum(large, N_BUCKETS - 1)
    return jnp.where(n < max_exact, n, large)


def _bias_lookup(bucket, tab_ref, head):
    val = jnp.zeros(bucket.shape, F32)
    for b in range(N_BUCKETS):
        val = jnp.where(bucket == b, tab_ref[b, head], val)
    return val


def _bias_kernel(tab_ref, dbias_ref, sbias_ref):
    kk = lax.broadcasted_iota(jnp.int32, (ATT_TILE, ATT_TILE), 0)
    qq = lax.broadcasted_iota(jnp.int32, (ATT_TILE, ATT_TILE), 1)
    for v in range(2):
        dist = qq - kk + v * ATT_TILE
        bucket = _t5_bucket(dist)
        for h in range(DIFF_HEADS):
            val = (_bias_lookup(bucket, tab_ref, h) - tab_ref[N_BUCKETS - 1, h]) * LOG2E
            dbias_ref[h, v] = jnp.where(dist >= 0, val, -jnp.inf)
    kk = lax.broadcasted_iota(jnp.int32, (2 * WINDOW, WINDOW), 0)
    qq = lax.broadcasted_iota(jnp.int32, (2 * WINDOW, WINDOW), 1)
    for v in range(2):
        dist = qq - kk + (1 - v) * WINDOW
        bucket = _t5_bucket(dist)
        for h in range(SWA_Q_HEADS):
            val = _bias_lookup(bucket, tab_ref, DIFF_HEADS + h) * LOG2E
            sbias_ref[h, v] = jnp.where(dist >= 0, jnp.where(dist < WINDOW, val, -jnp.inf), -jnp.inf)


def _bias_tiles(rel_bias):
    return pl.pallas_call(
        _bias_kernel,
        name="bias_tiles",
        in_specs=[pl.BlockSpec(memory_space=pltpu.SMEM)],
        out_specs=[pl.BlockSpec(memory_space=pltpu.VMEM), pl.BlockSpec(memory_space=pltpu.VMEM)],
        out_shape=[
            jax.ShapeDtypeStruct((DIFF_HEADS, 2, ATT_TILE, ATT_TILE), F32),
            jax.ShapeDtypeStruct((SWA_Q_HEADS, 2, 2 * WINDOW, WINDOW), F32),
        ],
    )(rel_bias)


def _split_pairs(q, n_pairs):
    lane = lax.broadcasted_iota(jnp.int32, (1, PAIR_W), 1)
    low = jnp.where(lane < HEAD_DIM, 1.0, 0.0).astype(BF16)
    high = jnp.where(lane >= HEAD_DIM, 1.0, 0.0).astype(BF16)
    heads = []
    for p in range(n_pairs):
        pair = q[:, p * PAIR_W:(p + 1) * PAIR_W]
        heads.append(pair * low)
        heads.append(pair * high)
    return heads


def _neg_abs(x):
    bits = lax.bitcast_convert_type(x, jnp.uint32) | jnp.uint32(0x80000000)
    return lax.bitcast_convert_type(bits, F32)


def _lead(rounds):
    for _ in range(rounds):
        yield


def _interleave(chains):
    live = []
    started = 0
    while started < len(chains) or live:
        if started < len(chains):
            live.append(started)
            started += 1
        for n in reversed(list(live)):
            try:
                next(chains[n])
            except StopIteration:
                live.remove(n)


def _round_robin(groups):
    merged = []
    for n in range(max(len(g) for g in groups)):
        merged.extend(g[n] for g in groups if n < len(g))
    return merged


def _key_tile(ref, j):
    return ref[0, pl.ds(pl.multiple_of(j * ATT_TILE, ATT_TILE), ATT_TILE), :]


def _attn_kernel(sink_ref, qa_ref, ka_ref, qd_ref, kd_ref, qs_ref, ks_ref,
                 vta_ref, vtd_ref, vts_ref, mtri_ref, dbias_ref, sbias_ref, lam_ref, gain_ref,
                 o_ref, acc_a, z_ref, btw_ref, acc_d, sd_ref, pvd_ref, ss_ref, pvs_ref,
                 *, lam_init):
    i = pl.program_id(1)
    first = i == 0
    prev = jnp.maximum(i - 1, 0)

    qa_heads = _split_pairs(qa_ref[0], SB_HEADS // 2)
    mtri = mtri_ref[...]
    kk = lax.broadcasted_iota(jnp.int32, (ATT_TILE, ATT_TILE), 0)
    qq = lax.broadcasted_iota(jnp.int32, (ATT_TILE, ATT_TILE), 1)
    strictly_before = kk < qq

    def sb_chains(tiles, carry, lead):
        def chain(k, vt, keep, fresh, h, slot):
            k_pair = k[:, (h // 2) * PAIR_W:(h // 2 + 1) * PAIR_W]
            rows = slice(h * HEAD_DIM, (h + 1) * HEAD_DIM)
            z_ref[slot] = lax.dot_general(k_pair, qa_heads[h], _NT, preferred_element_type=F32)
            yield from _lead(lead)
            z = z_ref[slot]
            sp = jnp.maximum(z, 0.0) + jnp.log(1.0 + jnp.exp2(_neg_abs(z))) * LOG2E
            if keep is not None:
                sp = jnp.where(keep, sp, 0.0)
            sp_b = sp.astype(BF16)
            yield
            btw_ref[slot] = jnp.dot(mtri, sp_b, preferred_element_type=F32)
            yield from _lead(lead)
            between = btw_ref[slot]
            after = carry[h]
            carry[h] = after + (between[0:1, :] - sp_b[0:1, :].astype(F32))
            w = jnp.exp2((z - sp) + between)
            if keep is not None:
                w = jnp.where(keep, w, 0.0)
            w = w.astype(BF16)
            yield
            pv = jnp.dot(vt[rows], w, preferred_element_type=F32)
            if fresh:
                acc_a[rows, :] = pv
            else:
                acc_a[rows, :] += pv * jnp.exp2(after)

        chains = []
        for t, (j, keep, fresh) in enumerate(tiles):
            k = _key_tile(ka_ref, j)
            vt = vta_ref[j]
            chains.extend(chain(k, vt, keep, fresh, h, SB_HEADS * t + h) for h in range(SB_HEADS))
        return chains

    qd_comps = _split_pairs(qd_ref[0], DIFF_HEADS)

    def diff_chains(tiles, m_run, l_run, lead):
        def chain(k, vt, variant, fresh, n, slot):
            h = n // 2
            sd_ref[slot] = lax.dot_general(k[:, h * PAIR_W:(h + 1) * PAIR_W], qd_comps[n], _NT,
                                           preferred_element_type=F32)
            yield from _lead(lead)
            s = sd_ref[slot]
            if variant is not None:
                s = s + dbias_ref[h, variant]
            m_prev = m_run[n]
            m_new = jnp.maximum(m_prev, jnp.max(s, axis=0, keepdims=True))
            alpha = jnp.exp2(m_prev - m_new)
            m_run[n] = m_new
            yield
            p = jnp.exp2(s - m_new)
            l_run[n] = alpha * l_run[n] + jnp.sum(p, axis=0, keepdims=True)
            p = p.astype(BF16)
            yield
            pvd_ref[slot] = jnp.dot(vt[h * PAIR_W:(h + 1) * PAIR_W], p, preferred_element_type=F32)
            yield from _lead(lead)
            acc_d[n] = pvd_ref[slot] if fresh else alpha * acc_d[n] + pvd_ref[slot]

        chains = []
        for t, (j, variant, fresh) in enumerate(tiles):
            k = _key_tile(kd_ref, j)
            vt = vtd_ref[j]
            chains.extend(chain(k, vt, variant, fresh, n, 2 * DIFF_HEADS * t + n)
                          for n in range(2 * DIFF_HEADS))
        return chains

    def swa_chains(outs, lead):
        qs = qs_ref[0]
        k_hi = _key_tile(ks_ref, i)
        vt_hi = vts_ref[i]
        lo_start = pl.multiple_of(jnp.maximum(i * ATT_TILE - WINDOW, 0), WINDOW)
        k_lo = ks_ref[0, pl.ds(lo_start, 2 * WINDOW), :]
        vt_lo = jnp.where(first, vt_hi,
                          jnp.concatenate([vts_ref[prev][:, WINDOW:], vt_hi[:, :WINDOW]], axis=1))
        lo_variant = jnp.where(first, 1, 0)
        halves = ((_split_pairs(qs[:WINDOW], SWA_Q_HEADS // 2), k_lo, vt_lo, lo_variant),
                  (_split_pairs(qs[WINDOW:], SWA_Q_HEADS // 2), k_hi, vt_hi, 0))

        def chain(q_heads, k_win, vt_win, variant, h, slot):
            kv = h // SWA_GROUP
            k_pair = k_win[:, kv * PAIR_W:(kv + 1) * PAIR_W]
            ss_ref[slot] = lax.dot_general(k_pair, q_heads[h], _NT, preferred_element_type=F32)
            yield from _lead(lead)
            sink = sink_ref[h] * LOG2E
            s = ss_ref[slot] + sbias_ref[h, variant]
            m = jnp.maximum(jnp.max(s, axis=0, keepdims=True), sink)
            e = jnp.exp2(s - m)
            denom = jnp.sum(e, axis=0, keepdims=True) + jnp.exp2(sink - m)
            e = e.astype(BF16)
            yield
            pvs_ref[slot] = jnp.dot(vt_win[kv * HEAD_DIM:(kv + 1) * HEAD_DIM], e,
                                    preferred_element_type=F32)
            yield from _lead(lead)
            outs[slot] = pvs_ref[slot] / denom

        return [chain(*half, h, a * SWA_Q_HEADS + h)
                for a, half in enumerate(halves) for h in range(SWA_Q_HEADS)]

    def head_tiles(sb_tiles, diff_tiles):
        carry = [jnp.zeros((1, ATT_TILE), F32)] * SB_HEADS
        m_run = [jnp.full((1, ATT_TILE), -jnp.inf, F32)] * (2 * DIFF_HEADS)
        l_run = [jnp.zeros((1, ATT_TILE), F32)] * (2 * DIFF_HEADS)
        swa = [None] * (2 * SWA_Q_HEADS)
        _interleave(_round_robin([sb_chains(sb_tiles, carry, MXU_LEAD_MIXED),
                                  diff_chains(diff_tiles, m_run, l_run, MXU_LEAD_MIXED),
                                  swa_chains(swa, MXU_LEAD_MIXED)]))
        return tuple(carry), tuple(zip(m_run, l_run)), tuple(swa)

    carries, state, swa = lax.cond(
        i >= 1,
        lambda: head_tiles([(i, strictly_before, True), (i - 1, None, False)],
                           [(i, 0, True), (i - 1, 1, False)]),
        lambda: head_tiles([(i, strictly_before, True)], [(i, 0, True)]))

    def far_tiles(tiles, st):
        m_run = [s[0] for s in st]
        l_run = [s[1] for s in st]
        _interleave(diff_chains(tiles, m_run, l_run, MXU_LEAD_LOOP))
        return tuple(zip(m_run, l_run))

    n_far = jnp.maximum(i - 1, 0)
    state = lax.fori_loop(
        0, lax.shift_right_logical(n_far, 1),
        lambda t, st: far_tiles([(i - 2 - 2 * t, None, False), (i - 3 - 2 * t, None, False)], st), state)
    state = lax.fori_loop(0, n_far & 1, lambda t, st: far_tiles([(0, None, False)], st), state)

    def live(c):
        j, carry = c
        return jnp.logical_and(j >= 0, jnp.max(jnp.concatenate(carry, axis=0)) > SB_DEAD_LOG2)

    def earlier_tile(c):
        j, carry = c
        carry = list(carry)
        _interleave(sb_chains([(j, None, False)], carry, MXU_LEAD_LOOP))
        return j - 1, tuple(carry)

    lax.while_loop(live, earlier_tile, (i - 2, carries))

    o_ref[0, :, 0:SB_W] = acc_a[...].T.astype(BF16)

    lv = lam_ref[...]
    lam = (jnp.exp(jnp.sum(lv[0:1] * lv[1:2], axis=-1, keepdims=True))
           - jnp.exp(jnp.sum(lv[2:3] * lv[3:4], axis=-1, keepdims=True)) + lam_init)
    outs = []
    for h in range(DIFF_HEADS):
        (_, l1), (_, l2) = state[2 * h], state[2 * h + 1]
        o = acc_d[2 * h] / l1 - lam * (acc_d[2 * h + 1] / l2)
        ms = jnp.mean(o * o, axis=0, keepdims=True)
        outs.append(o * lax.rsqrt(ms + EPS) * gain_ref[...] * (1.0 - lam_init))
    o_ref[0, :, SB_W:SB_W + DIFF_W] = jnp.concatenate(outs, axis=0).T.astype(BF16)

    swa_t = jnp.concatenate([jnp.concatenate(swa[:SWA_Q_HEADS], axis=0),
                             jnp.concatenate(swa[SWA_Q_HEADS:], axis=0)], axis=1)
    o_ref[0, :, SB_W + DIFF_W:ATT_OUT_W] = swa_t.T.astype(BF16)


def _attention(sinks, p1, p2, p3, mtri, dbias, sbias, lam_vecs, sub_gain, lam_init, batch, seq):
    nq = seq // ATT_TILE
    p1 = p1.reshape(batch, seq, P1_W)
    p2 = p2.reshape(batch, seq, P2_W)
    kdup_w = 2 * SWA_KV_W
    q_tile = lambda col: pl.BlockSpec((1, ATT_TILE, SB_W), lambda b, i: (b, i, col))
    keys = lambda col: pl.BlockSpec((1, seq, SB_W), lambda b, i: (b, 0, col))
    values_t = lambda rows, blk: pl.BlockSpec((nq, rows, ATT_TILE), lambda b, i: (b, blk, 0))
    whole = lambda shape: pl.BlockSpec(shape, lambda b, i: (0,) * len(shape),
                                       pipeline_mode=pl.Buffered(1))
    assert SB_W == DIFF_W == SWA_Q_W
    return pl.pallas_call(
        functools.partial(_attn_kernel, lam_init=lam_init),
        name="attention",
        grid=(batch, nq),
        in_specs=[
            pl.BlockSpec(memory_space=pltpu.SMEM),
            q_tile(0), keys(1),
            q_tile(0), keys(1),
            q_tile(2),
            pl.BlockSpec((1, seq, kdup_w), lambda b, i: (b, 0, (2 * DIFF_W + SWA_Q_W) // kdup_w)),
            values_t(SB_W, 0), values_t(DIFF_W, 1),
            values_t(SWA_KV_W, (SB_W + DIFF_W) // SWA_KV_W),
            whole((ATT_TILE, ATT_TILE)),
            whole((DIFF_HEADS, 2, ATT_TILE, ATT_TILE)),
            whole((SWA_Q_HEADS, 2, 2 * WINDOW, WINDOW)),
            whole((4, HEAD_DIM)),
            whole((PAIR_W, 1)),
        ],
        out_specs=pl.BlockSpec((1, ATT_TILE, ATT_OUT_W), lambda b, i: (b, i, 0)),
        out_shape=jax.ShapeDtypeStruct((batch, seq, ATT_OUT_W), BF16),
        scratch_shapes=[
            pltpu.VMEM((SB_W, ATT_TILE), F32),
            pltpu.VMEM((2 * SB_HEADS, ATT_TILE, ATT_TILE), F32),
            pltpu.VMEM((2 * SB_HEADS, ATT_TILE, ATT_TILE), F32),
            pltpu.VMEM((2 * DIFF_HEADS, PAIR_W, ATT_TILE), F32),
            pltpu.VMEM((4 * DIFF_HEADS, ATT_TILE, ATT_TILE), F32),
            pltpu.VMEM((4 * DIFF_HEADS, PAIR_W, ATT_TILE), F32),
            pltpu.VMEM((2 * SWA_Q_HEADS, 2 * WINDOW, WINDOW), F32),
            pltpu.VMEM((2 * SWA_Q_HEADS, HEAD_DIM, WINDOW), F32),
        ],
        compiler_params=_compiler_params(("parallel", "arbitrary"), 58 * 1024 * 1024),
    )(sinks, p1, p1, p2, p2, p2, p2, p3, p3, p3, mtri, dbias, sbias, lam_vecs,
      sub_gain.reshape(PAIR_W, 1))


def _merge_ffn_kernel(x_ref, osb_ref, odf_ref, osw_ref, ga_ref, gd_ref, gs_ref,
                      wsb_ref, wdf_ref, wsw_ref, wo_ref, ng_ref, wg_ref, wu_ref, wd_ref, o_ref):
    def branch(o_ref_, w_ref, g_ref):
        gate = 1.0 / (1.0 + jnp.exp(-g_ref[...].astype(F32)))
        return gate * jnp.dot(o_ref_[...], w_ref[...], preferred_element_type=F32)

    merged = (branch(osb_ref, wsb_ref, ga_ref) + branch(odf_ref, wdf_ref, gd_ref)
              + branch(osw_ref, wsw_ref, gs_ref))
    x = x_ref[...] + jnp.dot(merged.astype(BF16), wo_ref[...], preferred_element_type=F32)
    h = _rmsnorm_rows(x, ng_ref[...]).astype(BF16)
    out = x
    for start, width in MERGE_FFN_CHUNKS:
        cols = slice(start, start + width)
        g = jnp.dot(h, wg_ref[:, cols], preferred_element_type=F32)
        u = jnp.dot(h, wu_ref[:, cols], preferred_element_type=F32)
        a = (g * (1.0 / (1.0 + jnp.exp(-g))) * u).astype(BF16)
        out = out + 0.5 * jnp.dot(a, wd_ref[cols, :], preferred_element_type=F32)
    o_ref[...] = out


def _merge_ffn(xt, att, p1, w_sb, w_diff, w_swa, w_out, gain, w_in, w_down):
    t, d = xt.shape
    d_ff = w_down.shape[0]
    assert sum(w for _, w in MERGE_FFN_CHUNKS) == d_ff
    gcol = 2 * SB_W // d
    row = lambda i: (i, 0)
    resident = lambda shape, col=0: pl.BlockSpec(shape, lambda i: (0, col), pipeline_mode=pl.Buffered(1))
    branch_out = lambda n: pl.BlockSpec((ROW_TILE, SB_W), lambda i: (i, n))
    w_in = w_in.astype(BF16)
    return pl.pallas_call(
        _merge_ffn_kernel,
        name="merge_ffn",
        grid=(t // ROW_TILE,),
        in_specs=[
            pl.BlockSpec((ROW_TILE, d), row),
            branch_out(0), branch_out(1), branch_out(2),
            pl.BlockSpec((ROW_TILE, d), lambda i: (i, gcol)),
            pl.BlockSpec((ROW_TILE, d), lambda i: (i, gcol + 1)),
            pl.BlockSpec((ROW_TILE, d), lambda i: (i, gcol + 2)),
            resident((SB_W, d)), resident((DIFF_W, d)), resident((SWA_Q_W, d)), resident((d, d)),
            pl.BlockSpec((1, d), lambda i: (0, 0)),
            resident((d, d_ff), 0), resident((d, d_ff), 1), resident((d_ff, d)),
        ],
        out_specs=pl.BlockSpec((ROW_TILE, d), row),
        out_shape=jax.ShapeDtypeStruct((t, d), F32),
        compiler_params=_compiler_params(("parallel",), 60 * 1024 * 1024),
    )(xt, att, att, att, p1, p1, p1,
      w_sb.astype(BF16), w_diff.astype(BF16), w_swa.astype(BF16), w_out.astype(BF16),
      gain.reshape(1, d), w_in, w_in, w_down.astype(BF16))


def _projection_weights(w_in, q_norm_diff, k_norm_diff, q_norm_swa, k_norm_swa):
    widths = (SB_W, SB_W, SB_W, DIFF_W, DIFF_W, DIFF_W, SWA_Q_W, SWA_KV_W, SWA_KV_W)
    offs = [0]
    for w in widths:
        offs.append(offs[-1] + w)
    qa, ka, va, qd, kd, vd, qs, ks, vs = (w_in[:, offs[n]:offs[n + 1]] for n in range(len(widths)))
    gates = w_in[:, offs[-1]:]
    qscale = SCALE * LOG2E

    def dup_heads(w):
        return jnp.concatenate([w[:, h * HEAD_DIM:(h + 1) * HEAD_DIM]
                                for h in range(SWA_KV_HEADS) for _ in range(2)], axis=1)

    w1 = jnp.concatenate([qa * qscale, ka, gates], axis=1).astype(BF16)
    w2 = jnp.concatenate([qd, kd, qs, dup_heads(ks)], axis=1).astype(BF16)
    w3 = jnp.concatenate([va, vd, vs], axis=1).astype(BF16)
    g2 = jnp.concatenate([
        jnp.tile(q_norm_diff * qscale, DIFF_W // HEAD_DIM), jnp.tile(k_norm_diff, DIFF_W // HEAD_DIM),
        jnp.tile(q_norm_swa * qscale, SWA_Q_W // HEAD_DIM), jnp.tile(k_norm_swa, 2 * SWA_KV_W // HEAD_DIM),
    ]).reshape(1, P2_W).astype(F32)
    return w1, w2, g2, w3


def _group_mean_matrix():
    g = jnp.arange(P2_CHUNK) // HEAD_DIM
    return jnp.where(g[:, None] == g[None, :], 1.0 / HEAD_DIM, 0.0).astype(BF16)


def _neg_suffix_matrix():
    idx = jnp.arange(ATT_TILE)
    return jnp.where(idx[None, :] > idx[:, None], -1.0, 0.0).astype(BF16)


def kernel(x, ffn1_norm, ffn1_w_in, ffn1_w_out, mix_norm, w_in, q_norm_diff, k_norm_diff, q_norm_swa, k_norm_swa, diff_lambda, diff_subln, swa_sinks, rel_bias, w_proj_sb, w_proj_diff, w_proj_swa, w_out, ffn2_norm, ffn2_w_in, ffn2_w_out):
    batch, seq, d = x.shape
    depth = ffn1_norm.shape[0]
    assert d == D_MODEL and seq % ATT_TILE == 0 and (batch * seq) % ROW_TILE == 0
    xt = x.reshape(batch * seq, d)
    dbias, sbias = _bias_tiles(rel_bias)
    gmat = _group_mean_matrix()
    mtri = _neg_suffix_matrix()
    for l in range(depth):
        lam_init = 0.8 - 0.6 * math.exp(-0.3 * l)
        xt = _ffn(xt, ffn1_norm[l], ffn1_w_in[l], ffn1_w_out[l])
        w1, w2, g2, w3 = _projection_weights(w_in[l], q_norm_diff[l], k_norm_diff[l],
                                             q_norm_swa[l], k_norm_swa[l])
        p1, p2, p3 = _projection(xt, mix_norm[l], w1, w2, g2, gmat, w3)
        att = _attention(swa_sinks[l], p1, p2, p3, mtri, dbias, sbias, diff_lambda[l],
                         diff_subln[l], lam_init, batch, seq)
        xt = _merge_ffn(xt, att.reshape(batch * seq, ATT_OUT_W), p1,
                        w_proj_sb[l], w_proj_diff[l], w_proj_swa[l], w_out[l],
                        ffn2_norm[l], ffn2_w_in[l], ffn2_w_out[l])
    return xt.reshape(batch, seq, d)
```

```python
import functools
import math

import jax
import jax.numpy as jnp
from jax import lax
from jax.experimental import pallas as pl
from jax.experimental.pallas import tpu as pltpu

F32 = jnp.float32
BF16 = jnp.bfloat16

D_MODEL = 1024
HEAD_DIM = 64
SB_HEADS = 8
DIFF_HEADS = 4
SWA_Q_HEADS = 8
SWA_KV_HEADS = 2
SWA_GROUP = SWA_Q_HEADS // SWA_KV_HEADS
WINDOW = 128
N_BUCKETS = 32
MAX_DISTANCE = 128
EPS = 1e-6
SCALE = HEAD_DIM ** -0.5
LOG2E = math.log2(math.e)
SB_DEAD_LOG2 = -160.0
MXU_LEAD_MIXED = 2
MXU_LEAD_LOOP = 4

SB_W = SB_HEADS * HEAD_DIM
DIFF_W = DIFF_HEADS * 2 * HEAD_DIM
SWA_Q_W = SWA_Q_HEADS * HEAD_DIM
SWA_KV_W = SWA_KV_HEADS * HEAD_DIM
ATT_OUT_W = SB_W + DIFF_W + SWA_Q_W

V7X_LANES = 128
V7X_MXU_DIM = 256
V7X_VMEM_BYTES = 64 * 1024 * 1024

ATT_TILE = V7X_MXU_DIM
ROW_TILE = 512
PAIR_W = 2 * HEAD_DIM
SUM_ROWS = 16
assert PAIR_W == V7X_LANES and 2 * WINDOW == ATT_TILE

P1_W = 2 * SB_W + 3 * D_MODEL
P2_W = 2 * DIFF_W + SWA_Q_W + 2 * SWA_KV_W
P3_W = SB_W + DIFF_W + SWA_KV_W
P1_CHUNK = 512
P2_CHUNK = 256
P3_CHUNKS = ((0, 256), (256, 256), (512, 256), (768, 256), (1024, 128))
assert sum(w for _, w in P3_CHUNKS) == P3_W
MERGE_FFN_CHUNKS = ((0, 6 * V7X_MXU_DIM), (6 * V7X_MXU_DIM, 5 * V7X_MXU_DIM))

_NT = (((1,), (1,)), ((), ()))


def _compiler_params(semantics, vmem_bytes):
    assert vmem_bytes < V7X_VMEM_BYTES
    return pltpu.CompilerParams(dimension_semantics=semantics, vmem_limit_bytes=vmem_bytes)


def _rmsnorm_rows(x, g):
    ms = jnp.mean(x * x, axis=-1, keepdims=True)
    return x * lax.rsqrt(ms + EPS) * g


def _ffn_kernel(x_ref, g_ref, wg_ref, wu_ref, wo_ref, o_ref):
    x = x_ref[...]
    h = _rmsnorm_rows(x, g_ref[...]).astype(BF16)
    g = jnp.dot(h, wg_ref[...], preferred_element_type=F32)
    u = jnp.dot(h, wu_ref[...], preferred_element_type=F32)
    a = (g * (1.0 / (1.0 + jnp.exp(-g))) * u).astype(BF16)
    o_ref[...] = x + 0.5 * jnp.dot(a, wo_ref[...], preferred_element_type=F32)


def _ffn(xt, gain, w_in, w_out):
    t, d = xt.shape
    d_ff = w_out.shape[0]
    assert d_ff % V7X_MXU_DIM == 0 and t % ROW_TILE == 0
    w_in = w_in.astype(BF16)
    w_out = w_out.astype(BF16)
    resident = pl.Buffered(1)
    return pl.pallas_call(
        _ffn_kernel,
        name="ffn",
        grid=(t // ROW_TILE,),
        in_specs=[
            pl.BlockSpec((ROW_TILE, d), lambda i: (i, 0)),
            pl.BlockSpec((1, d), lambda i: (0, 0)),
            pl.BlockSpec((d, d_ff), lambda i: (0, 0), pipeline_mode=resident),
            pl.BlockSpec((d, d_ff), lambda i: (0, 1), pipeline_mode=resident),
            pl.BlockSpec((d_ff, d), lambda i: (0, 0), pipeline_mode=resident),
        ],
        out_specs=pl.BlockSpec((ROW_TILE, d), lambda i: (i, 0)),
        out_shape=jax.ShapeDtypeStruct((t, d), F32),
        compiler_params=_compiler_params(("parallel",), 56 * 1024 * 1024),
    )(xt, gain.reshape(1, d), w_in, w_in, w_out)


def _proj_kernel(x_ref, ng_ref, w1_ref, w2_ref, g2_ref, gm_ref, w3_ref,
                 p1_ref, p2_ref, p3_ref):
    h = _rmsnorm_rows(x_ref[...], ng_ref[...]).astype(BF16)
    for c in range(P1_W // P1_CHUNK):
        cols = slice(c * P1_CHUNK, (c + 1) * P1_CHUNK)
        p1_ref[:, cols] = jnp.dot(h, w1_ref[:, cols], preferred_element_type=F32).astype(BF16)
    y_all = jnp.dot(h, w2_ref[...], preferred_element_type=F32)
    for c in range(P2_W // P2_CHUNK):
        cols = slice(c * P2_CHUNK, (c + 1) * P2_CHUNK)
        y = y_all[:, cols]
        ms = jnp.dot((y * y).astype(BF16), gm_ref[...], preferred_element_type=F32)
        p2_ref[:, cols] = (y * lax.rsqrt(ms + EPS) * g2_ref[:, cols]).astype(BF16)
    for start, width in P3_CHUNKS:
        cols = slice(start, start + width)
        yt = jnp.dot(h, w3_ref[:, cols], preferred_element_type=F32).T
        for r in range(ROW_TILE // ATT_TILE):
            p3_ref[r, cols, :] = yt[:, r * ATT_TILE:(r + 1) * ATT_TILE].astype(BF16)


def _projection(xt, norm_gain, w1, w2, g2, gmat, w3):
    t, d = xt.shape
    const = lambda i: (0, 0)
    return pl.pallas_call(
        _proj_kernel,
        name="mixer_proj",
        grid=(t // ROW_TILE,),
        in_specs=[
            pl.BlockSpec((ROW_TILE, d), lambda i: (i, 0)),
            pl.BlockSpec((1, d), const),
            pl.BlockSpec((d, P1_W), const, pipeline_mode=pl.Buffered(1)),
            pl.BlockSpec((d, P2_W), const, pipeline_mode=pl.Buffered(1)),
            pl.BlockSpec((1, P2_W), const),
            pl.BlockSpec((P2_CHUNK, P2_CHUNK), const),
            pl.BlockSpec((d, P3_W), const, pipeline_mode=pl.Buffered(1)),
        ],
        out_specs=[
            pl.BlockSpec((ROW_TILE, P1_W), lambda i: (i, 0)),
            pl.BlockSpec((ROW_TILE, P2_W), lambda i: (i, 0)),
            pl.BlockSpec((ROW_TILE // ATT_TILE, P3_W, ATT_TILE), lambda i: (i, 0, 0)),
        ],
        out_shape=[
            jax.ShapeDtypeStruct((t, P1_W), BF16),
            jax.ShapeDtypeStruct((t, P2_W), BF16),
            jax.ShapeDtypeStruct((t // ATT_TILE, P3_W, ATT_TILE), BF16),
        ],
        compiler_params=_compiler_params(("parallel",), 52 * 1024 * 1024),
    )(xt, norm_gain.reshape(1, d), w1, w2, g2, gmat, w3)


def _t5_bucket(dist):
    n = jnp.maximum(dist, 0)
    max_exact = N_BUCKETS // 2
    nf = jnp.maximum(n, 1).astype(F32)
    large = max_exact + (jnp.log(nf / max_exact) / math.log(MAX_DISTANCE / max_exact)
                         * (N_BUCKETS - max_exact)).astype(jnp.int32)
    large = jnp.minimum(large, N_BUCKETS - 1)
    return jnp.where(n < max_exact, n, large)


def _bias_lookup(bucket, tab_ref, head):
    val = jnp.zeros(bucket.shape, F32)
    for b in range(N_BUCKETS):
        val = jnp.where(bucket == b, tab_ref[b, head], val)
    return val


def _bias_kernel(tab_ref, dbias_ref, sbias_ref):
    kk = lax.broadcasted_iota(jnp.int32, (ATT_TILE, ATT_TILE), 0)
    qq = lax.broadcasted_iota(jnp.int32, (ATT_TILE, ATT_TILE), 1)
    for v in range(2):
        dist = qq - kk + v * ATT_TILE
        bucket = _t5_bucket(dist)
        for h in range(DIFF_HEADS):
            val = (_bias_lookup(bucket, tab_ref, h) - tab_ref[N_BUCKETS - 1, h]) * LOG2E
            dbias_ref[h, v] = jnp.where(dist >= 0, val, -jnp.inf)
    kk = lax.broadcasted_iota(jnp.int32, (2 * WINDOW, WINDOW), 0)
    qq = lax.broadcasted_iota(jnp.int32, (2 * WINDOW, WINDOW), 1)
    for v in range(2):
        dist = qq - kk + (1 - v) * WINDOW
        bucket = _t5_bucket(dist)
        for h in range(SWA_Q_HEADS):
            val = _bias_lookup(bucket, tab_ref, DIFF_HEADS + h) * LOG2E
            sbias_ref[h, v] = jnp.where(dist >= 0, jnp.where(dist < WINDOW, val, -jnp.inf), -jnp.inf)


def _bias_tiles(rel_bias):
    return pl.pallas_call(
        _bias_kernel,
        name="bias_tiles",
        in_specs=[pl.BlockSpec(memory_space=pltpu.SMEM)],
        out_specs=[pl.BlockSpec(memory_space=pltpu.VMEM), pl.BlockSpec(memory_space=pltpu.VMEM)],
        out_shape=[
            jax.ShapeDtypeStruct((DIFF_HEADS, 2, ATT_TILE, ATT_TILE), F32),
            jax.ShapeDtypeStruct((SWA_Q_HEADS, 2, 2 * WINDOW, WINDOW), F32),
        ],
    )(rel_bias)


def _split_pairs(q, n_pairs):
    lane = lax.broadcasted_iota(jnp.int32, (1, PAIR_W), 1)
    low = jnp.where(lane < HEAD_DIM, 1.0, 0.0).astype(BF16)
    high = jnp.where(lane >= HEAD_DIM, 1.0, 0.0).astype(BF16)
    heads = []
    for p in range(n_pairs):
        pair = q[:, p * PAIR_W:(p + 1) * PAIR_W]
        heads.append(pair * low)
        heads.append(pair * high)
    return heads


def _neg_abs(x):
    bits = lax.bitcast_convert_type(x, jnp.uint32) | jnp.uint32(0x80000000)
    return lax.bitcast_convert_type(bits, F32)


def _lead(rounds):
    for _ in range(rounds):
        yield


def _interleave(chains):
    live = []
    started = 0
    while started < len(chains) or live:
        if started < len(chains):
            live.append(started)
            started += 1
        for n in reversed(list(live)):
            try:
                next(chains[n])
            except StopIteration:
                live.remove(n)


def _round_robin(groups):
    merged = []
    for n in range(max(len(g) for g in groups)):
        merged.extend(g[n] for g in groups if n < len(g))
    return merged


def _key_tile(ref, j):
    return ref[0, pl.ds(pl.multiple_of(j * ATT_TILE, ATT_TILE), ATT_TILE), :]


def _attn_kernel(sink_ref, qa_ref, ka_ref, qd_ref, kd_ref, qs_ref, ks_ref,
                 vta_ref, vtd_ref, vts_ref, mtri_ref, dbias_ref, sbias_ref, lam_ref, gain_ref,
                 o_ref, acc_a, z_ref, btw_ref, acc_d, sd_ref, pvd_ref, ss_ref, pvs_ref,
                 *, lam_init):
    i = pl.program_id(1)
    first = i == 0
    prev = jnp.maximum(i - 1, 0)
    ones_rows = jnp.ones((SUM_ROWS, ATT_TILE), BF16)

    qa_heads = _split_pairs(qa_ref[0], SB_HEADS // 2)
    mtri = mtri_ref[...]
    kk = lax.broadcasted_iota(jnp.int32, (ATT_TILE, ATT_TILE), 0)
    qq = lax.broadcasted_iota(jnp.int32, (ATT_TILE, ATT_TILE), 1)
    strictly_before = kk < qq

    def sb_chains(tiles, carry, lead):
        def chain(k, vt, keep, fresh, h, slot):
            k_pair = k[:, (h // 2) * PAIR_W:(h // 2 + 1) * PAIR_W]
            rows = slice(h * HEAD_DIM, (h + 1) * HEAD_DIM)
            z_ref[slot] = lax.dot_general(k_pair, qa_heads[h], _NT, preferred_element_type=F32)
            yield from _lead(lead)
            z = z_ref[slot]
            sp = jnp.maximum(z, 0.0) + jnp.log(1.0 + jnp.exp2(_neg_abs(z))) * LOG2E
            if keep is not None:
                sp = jnp.where(keep, sp, 0.0)
            sp_b = sp.astype(BF16)
            yield
            btw_ref[slot] = jnp.dot(mtri, sp_b, preferred_element_type=F32)
            yield from _lead(lead)
            between = btw_ref[slot]
            after = carry[h]
            carry[h] = after + (between[0:1, :] - sp_b[0:1, :].astype(F32))
            w = jnp.exp2((z - sp) + between)
            if keep is not None:
                w = jnp.where(keep, w, 0.0)
            w = w.astype(BF16)
            yield
            pv = jnp.dot(vt[rows], w, preferred_element_type=F32)
            if fresh:
                acc_a[rows, :] = pv
            else:
                acc_a[rows, :] += pv * jnp.exp2(after)

        chains = []
        for t, (j, keep, fresh) in enumerate(tiles):
            k = _key_tile(ka_ref, j)
            vt = vta_ref[j]
            chains.extend(chain(k, vt, keep, fresh, h, SB_HEADS * t + h) for h in range(SB_HEADS))
        return chains

    qd_comps = _split_pairs(qd_ref[0], DIFF_HEADS)

    def diff_chains(tiles, m_run, l_run, lead):
        def chain(k, vt, variant, fresh, n, slot):
            h = n // 2
            sd_ref[slot] = lax.dot_general(k[:, h * PAIR_W:(h + 1) * PAIR_W], qd_comps[n], _NT,
                                           preferred_element_type=F32)
            yield from _lead(lead)
            s = sd_ref[slot]
            if variant is not None:
                s = s + dbias_ref[h, variant]
            m_prev = m_run[n]
            m_new = jnp.maximum(m_prev, jnp.max(s, axis=0, keepdims=True))
            alpha = jnp.exp2(m_prev - m_new)
            m_run[n] = m_new
            yield
            p = jnp.exp2(s - m_new).astype(BF16)
            yield
            pvd_ref[slot] = jnp.dot(jnp.concatenate([vt[h * PAIR_W:(h + 1) * PAIR_W], ones_rows], axis=0),
                                    p, preferred_element_type=F32)
            yield from _lead(lead)
            pv = pvd_ref[slot]
            if fresh:
                l_run[n] = pv[PAIR_W:PAIR_W + 1]
                acc_d[n] = pv[:PAIR_W]
            else:
                l_run[n] = alpha * l_run[n] + pv[PAIR_W:PAIR_W + 1]
                acc_d[n] = alpha * acc_d[n] + pv[:PAIR_W]

        chains = []
        for t, (j, variant, fresh) in enumerate(tiles):
            k = _key_tile(kd_ref, j)
            vt = vtd_ref[j]
            chains.extend(chain(k, vt, variant, fresh, n, 2 * DIFF_HEADS * t + n)
                          for n in range(2 * DIFF_HEADS))
        return chains

    def swa_chains(outs, lead):
        qs = qs_ref[0]
        k_hi = _key_tile(ks_ref, i)
        vt_hi = vts_ref[i]
        lo_start = pl.multiple_of(jnp.maximum(i * ATT_TILE - WINDOW, 0), WINDOW)
        k_lo = ks_ref[0, pl.ds(lo_start, 2 * WINDOW), :]
        vt_lo = jnp.where(first, vt_hi,
                          jnp.concatenate([vts_ref[prev][:, WINDOW:], vt_hi[:, :WINDOW]], axis=1))
        lo_variant = jnp.where(first, 1, 0)
        halves = ((_split_pairs(qs[:WINDOW], SWA_Q_HEADS // 2), k_lo, vt_lo, lo_variant),
                  (_split_pairs(qs[WINDOW:], SWA_Q_HEADS // 2), k_hi, vt_hi, 0))

        def chain(q_heads, k_win, vt_win, variant, h, slot):
            kv = h // SWA_GROUP
            k_pair = k_win[:, kv * PAIR_W:(kv + 1) * PAIR_W]
            ss_ref[slot] = lax.dot_general(k_pair, q_heads[h], _NT, preferred_element_type=F32)
            yield from _lead(lead)
            sink = sink_ref[h] * LOG2E
            s = ss_ref[slot] + sbias_ref[h, variant]
            m = jnp.maximum(jnp.max(s, axis=0, keepdims=True), sink)
            e = jnp.exp2(s - m).astype(BF16)
            yield
            pvs_ref[slot] = jnp.dot(
                jnp.concatenate([vt_win[kv * HEAD_DIM:(kv + 1) * HEAD_DIM], ones_rows], axis=0),
                e, preferred_element_type=F32)
            yield from _lead(lead)
            pv = pvs_ref[slot]
            denom = pv[HEAD_DIM:HEAD_DIM + 1] + jnp.exp2(sink - m)
            outs[slot] = pv[:HEAD_DIM] / denom

        return [chain(*half, h, a * SWA_Q_HEADS + h)
                for a, half in enumerate(halves) for h in range(SWA_Q_HEADS)]

    def head_tiles(sb_tiles, diff_tiles):
        carry = [jnp.zeros((1, ATT_TILE), F32)] * SB_HEADS
        m_run = [jnp.full((1, ATT_TILE), -jnp.inf, F32)] * (2 * DIFF_HEADS)
        l_run = [jnp.zeros((1, ATT_TILE), F32)] * (2 * DIFF_HEADS)
        swa = [None] * (2 * SWA_Q_HEADS)
        _interleave(_round_robin([sb_chains(sb_tiles, carry, MXU_LEAD_MIXED),
                                  diff_chains(diff_tiles, m_run, l_run, MXU_LEAD_MIXED),
                                  swa_chains(swa, MXU_LEAD_MIXED)]))
        return tuple(carry), tuple(zip(m_run, l_run)), tuple(swa)

    carries, state, swa = lax.cond(
        i >= 1,
        lambda: head_tiles([(i, strictly_before, True), (i - 1, None, False)],
                           [(i, 0, True), (i - 1, 1, False)]),
        lambda: head_tiles([(i, strictly_before, True)], [(i, 0, True)]))

    def far_tiles(tiles, st):
        m_run = [s[0] for s in st]
        l_run = [s[1] for s in st]
        _interleave(diff_chains(tiles, m_run, l_run, MXU_LEAD_LOOP))
        return tuple(zip(m_run, l_run))

    n_far = jnp.maximum(i - 1, 0)
    state = lax.fori_loop(
        0, lax.shift_right_logical(n_far, 1),
        lambda t, st: far_tiles([(i - 2 - 2 * t, None, False), (i - 3 - 2 * t, None, False)], st), state)
    state = lax.fori_loop(0, n_far & 1, lambda t, st: far_tiles([(0, None, False)], st), state)

    def live(c):
        j, carry = c
        return jnp.logical_and(j >= 0, jnp.max(jnp.concatenate(carry, axis=0)) > SB_DEAD_LOG2)

    def earlier_tile(c):
        j, carry = c
        carry = list(carry)
        _interleave(sb_chains([(j, None, False)], carry, MXU_LEAD_LOOP))
        return j - 1, tuple(carry)

    lax.while_loop(live, earlier_tile, (i - 2, carries))

    o_ref[0, :, 0:SB_W] = acc_a[...].T.astype(BF16)

    lv = lam_ref[...]
    lam = (jnp.exp(jnp.sum(lv[0:1] * lv[1:2], axis=-1, keepdims=True))
           - jnp.exp(jnp.sum(lv[2:3] * lv[3:4], axis=-1, keepdims=True)) + lam_init)
    outs = []
    for h in range(DIFF_HEADS):
        (_, l1), (_, l2) = state[2 * h], state[2 * h + 1]
        o = acc_d[2 * h] / l1 - lam * (acc_d[2 * h + 1] / l2)
        ms = jnp.mean(o * o, axis=0, keepdims=True)
        outs.append(o * lax.rsqrt(ms + EPS) * gain_ref[...] * (1.0 - lam_init))
    o_ref[0, :, SB_W:SB_W + DIFF_W] = jnp.concatenate(outs, axis=0).T.astype(BF16)

    swa_t = jnp.concatenate([jnp.concatenate(swa[:SWA_Q_HEADS], axis=0),
                             jnp.concatenate(swa[SWA_Q_HEADS:], axis=0)], axis=1)
    o_ref[0, :, SB_W + DIFF_W:ATT_OUT_W] = swa_t.T.astype(BF16)


def _attention(sinks, p1, p2, p3, mtri, dbias, sbias, lam_vecs, sub_gain, lam_init, batch, seq):
    nq = seq // ATT_TILE
    p1 = p1.reshape(batch, seq, P1_W)
    p2 = p2.reshape(batch, seq, P2_W)
    kdup_w = 2 * SWA_KV_W
    q_tile = lambda col: pl.BlockSpec((1, ATT_TILE, SB_W), lambda b, i: (b, i, col))
    keys = lambda col: pl.BlockSpec((1, seq, SB_W), lambda b, i: (b, 0, col))
    values_t = lambda rows, blk: pl.BlockSpec((nq, rows, ATT_TILE), lambda b, i: (b, blk, 0))
    whole = lambda shape: pl.BlockSpec(shape, lambda b, i: (0,) * len(shape),
                                       pipeline_mode=pl.Buffered(1))
    assert SB_W == DIFF_W == SWA_Q_W
    return pl.pallas_call(
        functools.partial(_attn_kernel, lam_init=lam_init),
        name="attention",
        grid=(batch, nq),
        in_specs=[
            pl.BlockSpec(memory_space=pltpu.SMEM),
            q_tile(0), keys(1),
            q_tile(0), keys(1),
            q_tile(2),
            pl.BlockSpec((1, seq, kdup_w), lambda b, i: (b, 0, (2 * DIFF_W + SWA_Q_W) // kdup_w)),
            values_t(SB_W, 0), values_t(DIFF_W, 1),
            values_t(SWA_KV_W, (SB_W + DIFF_W) // SWA_KV_W),
            whole((ATT_TILE, ATT_TILE)),
            whole((DIFF_HEADS, 2, ATT_TILE, ATT_TILE)),
            whole((SWA_Q_HEADS, 2, 2 * WINDOW, WINDOW)),
            whole((4, HEAD_DIM)),
            whole((PAIR_W, 1)),
        ],
        out_specs=pl.BlockSpec((1, ATT_TILE, ATT_OUT_W), lambda b, i: (b, i, 0)),
        out_shape=jax.ShapeDtypeStruct((batch, seq, ATT_OUT_W), BF16),
        scratch_shapes=[
            pltpu.VMEM((SB_W, ATT_TILE), F32),
            pltpu.VMEM((2 * SB_HEADS, ATT_TILE, ATT_TILE), F32),
            pltpu.VMEM((2 * SB_HEADS, ATT_TILE, ATT_TILE), F32),
            pltpu.VMEM((2 * DIFF_HEADS, PAIR_W, ATT_TILE), F32),
            pltpu.VMEM((4 * DIFF_HEADS, ATT_TILE, ATT_TILE), F32),
            pltpu.VMEM((4 * DIFF_HEADS, PAIR_W + SUM_ROWS, ATT_TILE), F32),
            pltpu.VMEM((2 * SWA_Q_HEADS, 2 * WINDOW, WINDOW), F32),
            pltpu.VMEM((2 * SWA_Q_HEADS, HEAD_DIM + SUM_ROWS, WINDOW), F32),
        ],
        compiler_params=_compiler_params(("parallel", "arbitrary"), 58 * 1024 * 1024),
    )(sinks, p1, p1, p2, p2, p2, p2, p3, p3, p3, mtri, dbias, sbias, lam_vecs,
      sub_gain.reshape(PAIR_W, 1))


def _merge_ffn_kernel(x_ref, osb_ref, odf_ref, osw_ref, ga_ref, gd_ref, gs_ref,
                      wsb_ref, wdf_ref, wsw_ref, wo_ref, ng_ref, wg_ref, wu_ref, wd_ref, o_ref):
    def branch(o_ref_, w_ref, g_ref):
        gate = 1.0 / (1.0 + jnp.exp(-g_ref[...].astype(F32)))
        return gate * jnp.dot(o_ref_[...], w_ref[...], preferred_element_type=F32)

    merged = (branch(osb_ref, wsb_ref, ga_ref) + branch(odf_ref, wdf_ref, gd_ref)
              + branch(osw_ref, wsw_ref, gs_ref))
    x = x_ref[...] + jnp.dot(merged.astype(BF16), wo_ref[...], preferred_element_type=F32)
    h = _rmsnorm_rows(x, ng_ref[...]).astype(BF16)
    out = x
    for start, width in MERGE_FFN_CHUNKS:
        cols = slice(start, start + width)
        g = jnp.dot(h, wg_ref[:, cols], preferred_element_type=F32)
        u = jnp.dot(h, wu_ref[:, cols], preferred_element_type=F32)
        a = (g * (1.0 / (1.0 + jnp.exp(-g))) * u).astype(BF16)
        out = out + 0.5 * jnp.dot(a, wd_ref[cols, :], preferred_element_type=F32)
    o_ref[...] = out


def _merge_ffn(xt, att, p1, w_sb, w_diff, w_swa, w_out, gain, w_in, w_down):
    t, d = xt.shape
    d_ff = w_down.shape[0]
    assert sum(w for _, w in MERGE_FFN_CHUNKS) == d_ff
    gcol = 2 * SB_W // d
    row = lambda i: (i, 0)
    resident = lambda shape, col=0: pl.BlockSpec(shape, lambda i: (0, col), pipeline_mode=pl.Buffered(1))
    branch_out = lambda n: pl.BlockSpec((ROW_TILE, SB_W), lambda i: (i, n))
    w_in = w_in.astype(BF16)
    return pl.pallas_call(
        _merge_ffn_kernel,
        name="merge_ffn",
        grid=(t // ROW_TILE,),
        in_specs=[
            pl.BlockSpec((ROW_TILE, d), row),
            branch_out(0), branch_out(1), branch_out(2),
            pl.BlockSpec((ROW_TILE, d), lambda i: (i, gcol)),
            pl.BlockSpec((ROW_TILE, d), lambda i: (i, gcol + 1)),
            pl.BlockSpec((ROW_TILE, d), lambda i: (i, gcol + 2)),
            resident((SB_W, d)), resident((DIFF_W, d)), resident((SWA_Q_W, d)), resident((d, d)),
            pl.BlockSpec((1, d), lambda i: (0, 0)),
            resident((d, d_ff), 0), resident((d, d_ff), 1), resident((d_ff, d)),
        ],
        out_specs=pl.BlockSpec((ROW_TILE, d), row),
        out_shape=jax.ShapeDtypeStruct((t, d), F32),
        compiler_params=_compiler_params(("parallel",), 60 * 1024 * 1024),
    )(xt, att, att, att, p1, p1, p1,
      w_sb.astype(BF16), w_diff.astype(BF16), w_swa.astype(BF16), w_out.astype(BF16),
      gain.reshape(1, d), w_in, w_in, w_down.astype(BF16))


def _projection_weights(w_in, q_norm_diff, k_norm_diff, q_norm_swa, k_norm_swa):
    widths = (SB_W, SB_W, SB_W, DIFF_W, DIFF_W, DIFF_W, SWA_Q_W, SWA_KV_W, SWA_KV_W)
    offs = [0]
    for w in widths:
        offs.append(offs[-1] + w)
    qa, ka, va, qd, kd, vd, qs, ks, vs = (w_in[:, offs[n]:offs[n + 1]] for n in range(len(widths)))
    gates = w_in[:, offs[-1]:]
    qscale = SCALE * LOG2E

    def dup_heads(w):
        return jnp.concatenate([w[:, h * HEAD_DIM:(h + 1) * HEAD_DIM]
                                for h in range(SWA_KV_HEADS) for _ in range(2)], axis=1)

    w1 = jnp.concatenate([qa * qscale, ka, gates], axis=1).astype(BF16)
    w2 = jnp.concatenate([qd, kd, qs, dup_heads(ks)], axis=1).astype(BF16)
    w3 = jnp.concatenate([va, vd, vs], axis=1).astype(BF16)
    g2 = jnp.concatenate([
        jnp.tile(q_norm_diff * qscale, DIFF_W // HEAD_DIM), jnp.tile(k_norm_diff, DIFF_W // HEAD_DIM),
        jnp.tile(q_norm_swa * qscale, SWA_Q_W // HEAD_DIM), jnp.tile(k_norm_swa, 2 * SWA_KV_W // HEAD_DIM),
    ]).reshape(1, P2_W).astype(F32)
    return w1, w2, g2, w3


def _group_mean_matrix():
    g = jnp.arange(P2_CHUNK) // HEAD_DIM
    return jnp.where(g[:, None] == g[None, :], 1.0 / HEAD_DIM, 0.0).astype(BF16)


def _neg_suffix_matrix():
    idx = jnp.arange(ATT_TILE)
    return jnp.where(idx[None, :] > idx[:, None], -1.0, 0.0).astype(BF16)


def kernel(x, ffn1_norm, ffn1_w_in, ffn1_w_out, mix_norm, w_in, q_norm_diff, k_norm_diff, q_norm_swa, k_norm_swa, diff_lambda, diff_subln, swa_sinks, rel_bias, w_proj_sb, w_proj_diff, w_proj_swa, w_out, ffn2_norm, ffn2_w_in, ffn2_w_out):
    batch, seq, d = x.shape
    depth = ffn1_norm.shape[0]
    assert d == D_MODEL and seq % ATT_TILE == 0 and (batch * seq) % ROW_TILE == 0
    xt = x.reshape(batch * seq, d)
    dbias, sbias = _bias_tiles(rel_bias)
    gmat = _group_mean_matrix()
    mtri = _neg_suffix_matrix()
    for l in range(depth):
        lam_init = 0.8 - 0.6 * math.exp(-0.3 * l)
        xt = _ffn(xt, ffn1_norm[l], ffn1_w_in[l], ffn1_w_out[l])
        w1, w2, g2, w3 = _projection_weights(w_in[l], q_norm_diff[l], k_norm_diff[l],
                                             q_norm_swa[l], k_norm_swa[l])
        p1, p2, p3 = _projection(xt, mix_norm[l], w1, w2, g2, gmat, w3)
        att = _attention(swa_sinks[l], p1, p2, p3, mtri, dbias, sbias, diff_lambda[l],
                         diff_subln[l], lam_init, batch, seq)
        xt = _merge_ffn(xt, att.reshape(batch * seq, ATT_OUT_W), p1,
                        w_proj_sb[l], w_proj_diff[l], w_proj_swa[l], w_out[l],
                        ffn2_norm[l], ffn2_w_in[l], ffn2_w_out[l])
    return xt.reshape(batch, seq, d)
```

```python
import functools
import math

import jax
import jax.numpy as jnp
from jax import lax
from jax.experimental import pallas as pl
from jax.experimental.pallas import tpu as pltpu

F32 = jnp.float32
BF16 = jnp.bfloat16

D_MODEL = 1024
HEAD_DIM = 64
SB_HEADS = 8
DIFF_HEADS = 4
SWA_Q_HEADS = 8
SWA_KV_HEADS = 2
SWA_GROUP = SWA_Q_HEADS // SWA_KV_HEADS
WINDOW = 128
N_BUCKETS = 32
MAX_DISTANCE = 128
EPS = 1e-6
SCALE = HEAD_DIM ** -0.5
LOG2E = math.log2(math.e)
SB_DEAD_LOG2 = -160.0
MXU_LEAD_MIXED = 2
MXU_LEAD_LOOP = 4

SB_W = SB_HEADS * HEAD_DIM
DIFF_W = DIFF_HEADS * 2 * HEAD_DIM
SWA_Q_W = SWA_Q_HEADS * HEAD_DIM
SWA_KV_W = SWA_KV_HEADS * HEAD_DIM
ATT_OUT_W = SB_W + DIFF_W + SWA_Q_W

V7X_LANES = 128
V7X_MXU_DIM = 256
V7X_VMEM_BYTES = 64 * 1024 * 1024

ATT_TILE = V7X_MXU_DIM
ROW_TILE = 512
CAST_ROWS = 256
PAIR_W = 2 * HEAD_DIM
SUM_ROWS = 16
assert PAIR_W == V7X_LANES and 2 * WINDOW == ATT_TILE

P1_W = 2 * SB_W + 3 * D_MODEL
P2_W = 2 * DIFF_W + SWA_Q_W + 2 * SWA_KV_W
P3_W = SB_W + DIFF_W + SWA_KV_W
P1_CHUNK = 512
P2_CHUNK = 256
P3_CHUNKS = ((0, 256), (256, 256), (512, 256), (768, 256), (1024, 128))
assert sum(w for _, w in P3_CHUNKS) == P3_W
MERGE_FFN_CHUNKS = ((0, 6 * V7X_MXU_DIM), (6 * V7X_MXU_DIM, 5 * V7X_MXU_DIM))

_NT = (((1,), (1,)), ((), ()))


def _compiler_params(semantics, vmem_bytes):
    assert vmem_bytes < V7X_VMEM_BYTES
    return pltpu.CompilerParams(dimension_semantics=semantics, vmem_limit_bytes=vmem_bytes)


def _rmsnorm_rows(x, g):
    ms = jnp.mean(x * x, axis=-1, keepdims=True)
    return x * lax.rsqrt(ms + EPS) * g


def _cast_kernel(w_ref, o_ref):
    o_ref[...] = w_ref[...].astype(BF16)


def _cast_bf16(w):
    depth, rows, cols = w.shape
    assert rows % CAST_ROWS == 0
    spec = pl.BlockSpec((None, CAST_ROWS, cols), lambda l, r: (l, r, 0))
    return pl.pallas_call(
        _cast_kernel,
        name="cast_bf16",
        grid=(depth, rows // CAST_ROWS),
        in_specs=[spec],
        out_specs=spec,
        out_shape=jax.ShapeDtypeStruct(w.shape, BF16),
        compiler_params=_compiler_params(("parallel", "parallel"), 40 * 1024 * 1024),
    )(w)


def _regroup_kernel(w_ref, w1_ref, w2_ref, w3_ref):
    def cols(start, width):
        return w_ref[:, start:start + width]

    offs = {}
    acc = 0
    for name, width in (("qa", SB_W), ("ka", SB_W), ("va", SB_W), ("qd", DIFF_W), ("kd", DIFF_W),
                        ("vd", DIFF_W), ("qs", SWA_Q_W), ("ks", SWA_KV_W), ("vs", SWA_KV_W),
                        ("gates", 3 * D_MODEL)):
        offs[name] = (acc, width)
        acc += width
    w1_ref[:, 0:SB_W] = (cols(*offs["qa"]) * (SCALE * LOG2E)).astype(BF16)
    w1_ref[:, SB_W:2 * SB_W] = cols(*offs["ka"]).astype(BF16)
    w1_ref[:, 2 * SB_W:P1_W] = cols(*offs["gates"]).astype(BF16)
    w2_ref[:, 0:DIFF_W] = cols(*offs["qd"]).astype(BF16)
    w2_ref[:, DIFF_W:2 * DIFF_W] = cols(*offs["kd"]).astype(BF16)
    w2_ref[:, 2 * DIFF_W:2 * DIFF_W + SWA_Q_W] = cols(*offs["qs"]).astype(BF16)
    ks = cols(*offs["ks"])
    swapped = pltpu.roll(ks, HEAD_DIM, axis=1)
    lane = lax.broadcasted_iota(jnp.int32, ks.shape, 1)
    base = 2 * DIFF_W + SWA_Q_W
    w2_ref[:, base:base + PAIR_W] = jnp.where(lane < HEAD_DIM, ks, swapped).astype(BF16)
    w2_ref[:, base + PAIR_W:P2_W] = jnp.where(lane < HEAD_DIM, swapped, ks).astype(BF16)
    w3_ref[:, 0:SB_W] = cols(*offs["va"]).astype(BF16)
    w3_ref[:, SB_W:SB_W + DIFF_W] = cols(*offs["vd"]).astype(BF16)
    w3_ref[:, SB_W + DIFF_W:P3_W] = cols(*offs["vs"]).astype(BF16)


def _regroup_projection(w_in):
    depth, d, width = w_in.shape
    assert SWA_KV_W == PAIR_W and d % CAST_ROWS == 0
    block = lambda cols: pl.BlockSpec((None, CAST_ROWS, cols), lambda l, r: (l, r, 0))
    return pl.pallas_call(
        _regroup_kernel,
        name="regroup_projection",
        grid=(depth, d // CAST_ROWS),
        in_specs=[block(width)],
        out_specs=[block(P1_W), block(P2_W), block(P3_W)],
        out_shape=[jax.ShapeDtypeStruct((depth, d, w), BF16) for w in (P1_W, P2_W, P3_W)],
        compiler_params=_compiler_params(("parallel", "parallel"), 40 * 1024 * 1024),
    )(w_in)


def _ffn_kernel(x_ref, g_ref, wg_ref, wu_ref, wo_ref, o_ref):
    x = x_ref[...]
    h = _rmsnorm_rows(x, g_ref[...]).astype(BF16)
    g = jnp.dot(h, wg_ref[...], preferred_element_type=F32)
    u = jnp.dot(h, wu_ref[...], preferred_element_type=F32)
    a = (g * (1.0 / (1.0 + jnp.exp(-g))) * u).astype(BF16)
    o_ref[...] = x + 0.5 * jnp.dot(a, wo_ref[...], preferred_element_type=F32)


def _ffn(xt, gain, w_in, w_out, layer):
    t, d = xt.shape
    d_ff = w_out.shape[1]
    assert d_ff % V7X_MXU_DIM == 0 and t % ROW_TILE == 0
    resident = pl.Buffered(1)
    return pl.pallas_call(
        _ffn_kernel,
        name="ffn",
        grid=(t // ROW_TILE,),
        in_specs=[
            pl.BlockSpec((ROW_TILE, d), lambda i: (i, 0)),
            pl.BlockSpec((1, d), lambda i: (0, 0)),
            pl.BlockSpec((None, d, d_ff), lambda i: (layer, 0, 0), pipeline_mode=resident),
            pl.BlockSpec((None, d, d_ff), lambda i: (layer, 0, 1), pipeline_mode=resident),
            pl.BlockSpec((None, d_ff, d), lambda i: (layer, 0, 0), pipeline_mode=resident),
        ],
        out_specs=pl.BlockSpec((ROW_TILE, d), lambda i: (i, 0)),
        out_shape=jax.ShapeDtypeStruct((t, d), F32),
        compiler_params=_compiler_params(("parallel",), 56 * 1024 * 1024),
    )(xt, gain.reshape(1, d), w_in, w_in, w_out)


def _proj_kernel(x_ref, ng_ref, w1_ref, w2_ref, g2_ref, gm_ref, w3_ref,
                 p1_ref, p2_ref, p3_ref):
    h = _rmsnorm_rows(x_ref[...], ng_ref[...]).astype(BF16)
    for c in range(P1_W // P1_CHUNK):
        cols = slice(c * P1_CHUNK, (c + 1) * P1_CHUNK)
        p1_ref[:, cols] = jnp.dot(h, w1_ref[:, cols], preferred_element_type=F32).astype(BF16)
    y_all = jnp.dot(h, w2_ref[...], preferred_element_type=F32)
    for c in range(P2_W // P2_CHUNK):
        cols = slice(c * P2_CHUNK, (c + 1) * P2_CHUNK)
        y = y_all[:, cols]
        ms = jnp.dot((y * y).astype(BF16), gm_ref[...], preferred_element_type=F32)
        p2_ref[:, cols] = (y * lax.rsqrt(ms + EPS) * g2_ref[:, cols]).astype(BF16)
    for start, width in P3_CHUNKS:
        cols = slice(start, start + width)
        yt = jnp.dot(h, w3_ref[:, cols], preferred_element_type=F32).T
        for r in range(ROW_TILE // ATT_TILE):
            p3_ref[r, cols, :] = yt[:, r * ATT_TILE:(r + 1) * ATT_TILE].astype(BF16)


def _projection(xt, norm_gain, w1, w2, g2, gmat, w3, layer):
    t, d = xt.shape
    const = lambda i: (0, 0)
    weights = lambda cols: pl.BlockSpec((None, d, cols), lambda i: (layer, 0, 0),
                                        pipeline_mode=pl.Buffered(1))
    return pl.pallas_call(
        _proj_kernel,
        name="mixer_proj",
        grid=(t // ROW_TILE,),
        in_specs=[
            pl.BlockSpec((ROW_TILE, d), lambda i: (i, 0)),
            pl.BlockSpec((1, d), const),
            weights(P1_W),
            weights(P2_W),
            pl.BlockSpec((1, P2_W), const),
            pl.BlockSpec((P2_CHUNK, P2_CHUNK), const),
            weights(P3_W),
        ],
        out_specs=[
            pl.BlockSpec((ROW_TILE, P1_W), lambda i: (i, 0)),
            pl.BlockSpec((ROW_TILE, P2_W), lambda i: (i, 0)),
            pl.BlockSpec((ROW_TILE // ATT_TILE, P3_W, ATT_TILE), lambda i: (i, 0, 0)),
        ],
        out_shape=[
            jax.ShapeDtypeStruct((t, P1_W), BF16),
            jax.ShapeDtypeStruct((t, P2_W), BF16),
            jax.ShapeDtypeStruct((t // ATT_TILE, P3_W, ATT_TILE), BF16),
        ],
        compiler_params=_compiler_params(("parallel",), 52 * 1024 * 1024),
    )(xt, norm_gain.reshape(1, d), w1, w2, g2, gmat, w3)


def _t5_bucket(dist):
    n = jnp.maximum(dist, 0)
    max_exact = N_BUCKETS // 2
    nf = jnp.maximum(n, 1).astype(F32)
    large = max_exact + (jnp.log(nf / max_exact) / math.log(MAX_DISTANCE / max_exact)
                         * (N_BUCKETS - max_exact)).astype(jnp.int32)
    large = jnp.minimum(large, N_BUCKETS - 1)
    return jnp.where(n < max_exact, n, large)


def _bias_lookup(bucket, tab_ref, head):
    val = jnp.zeros(bucket.shape, F32)
    for b in range(N_BUCKETS):
        val = jnp.where(bucket == b, tab_ref[b, head], val)
    return val


def _bias_kernel(tab_ref, dbias_ref, sbias_ref):
    kk = lax.broadcasted_iota(jnp.int32, (ATT_TILE, ATT_TILE), 0)
    qq = lax.broadcasted_iota(jnp.int32, (ATT_TILE, ATT_TILE), 1)
    for v in range(2):
        dist = qq - kk + v * ATT_TILE
        bucket = _t5_bucket(dist)
        for h in range(DIFF_HEADS):
            val = (_bias_lookup(bucket, tab_ref, h) - tab_ref[N_BUCKETS - 1, h]) * LOG2E
            dbias_ref[h, v] = jnp.where(dist >= 0, val, -jnp.inf)
    kk = lax.broadcasted_iota(jnp.int32, (2 * WINDOW, WINDOW), 0)
    qq = lax.broadcasted_iota(jnp.int32, (2 * WINDOW, WINDOW), 1)
    for v in range(2):
        dist = qq - kk + (1 - v) * WINDOW
        bucket = _t5_bucket(dist)
        for h in range(SWA_Q_HEADS):
            val = _bias_lookup(bucket, tab_ref, DIFF_HEADS + h) * LOG2E
            sbias_ref[h, v] = jnp.where(dist >= 0, jnp.where(dist < WINDOW, val, -jnp.inf), -jnp.inf)


def _bias_tiles(rel_bias):
    return pl.pallas_call(
        _bias_kernel,
        name="bias_tiles",
        in_specs=[pl.BlockSpec(memory_space=pltpu.SMEM)],
        out_specs=[pl.BlockSpec(memory_space=pltpu.VMEM), pl.BlockSpec(memory_space=pltpu.VMEM)],
        out_shape=[
            jax.ShapeDtypeStruct((DIFF_HEADS, 2, ATT_TILE, ATT_TILE), F32),
            jax.ShapeDtypeStruct((SWA_Q_HEADS, 2, 2 * WINDOW, WINDOW), F32),
        ],
    )(rel_bias)


def _split_pairs(q, n_pairs):
    lane = lax.broadcasted_iota(jnp.int32, (1, PAIR_W), 1)
    low = jnp.where(lane < HEAD_DIM, 1.0, 0.0).astype(BF16)
    high = jnp.where(lane >= HEAD_DIM, 1.0, 0.0).astype(BF16)
    heads = []
    for p in range(n_pairs):
        pair = q[:, p * PAIR_W:(p + 1) * PAIR_W]
        heads.append(pair * low)
        heads.append(pair * high)
    return heads


def _neg_abs(x):
    bits = lax.bitcast_convert_type(x, jnp.uint32) | jnp.uint32(0x80000000)
    return lax.bitcast_convert_type(bits, F32)


def _lead(rounds):
    for _ in range(rounds):
        yield


def _interleave(chains):
    live = []
    started = 0
    while started < len(chains) or live:
        if started < len(chains):
            live.append(started)
            started += 1
        for n in reversed(list(live)):
            try:
                next(chains[n])
            except StopIteration:
                live.remove(n)


def _round_robin(groups):
    merged = []
    for n in range(max(len(g) for g in groups)):
        merged.extend(g[n] for g in groups if n < len(g))
    return merged


def _key_tile(ref, j):
    return ref[0, pl.ds(pl.multiple_of(j * ATT_TILE, ATT_TILE), ATT_TILE), :]


def _attn_kernel(sink_ref, qa_ref, ka_ref, qd_ref, kd_ref, qs_ref, ks_ref,
                 vta_ref, vtd_ref, vts_ref, mtri_ref, dbias_ref, sbias_ref, lam_ref, gain_ref,
                 o_ref, acc_a, z_ref, btw_ref, acc_d, sd_ref, pvd_ref, ss_ref, pvs_ref,
                 *, lam_init):
    i = pl.program_id(1)
    first = i == 0
    prev = jnp.maximum(i - 1, 0)
    ones_rows = jnp.ones((SUM_ROWS, ATT_TILE), BF16)

    qa_heads = _split_pairs(qa_ref[0], SB_HEADS // 2)
    mtri = mtri_ref[...]
    kk = lax.broadcasted_iota(jnp.int32, (ATT_TILE, ATT_TILE), 0)
    qq = lax.broadcasted_iota(jnp.int32, (ATT_TILE, ATT_TILE), 1)
    strictly_before = kk < qq

    def sb_chains(tiles, carry, lead):
        def chain(k, vt, keep, fresh, h, slot):
            k_pair = k[:, (h // 2) * PAIR_W:(h // 2 + 1) * PAIR_W]
            rows = slice(h * HEAD_DIM, (h + 1) * HEAD_DIM)
            z_ref[slot] = lax.dot_general(k_pair, qa_heads[h], _NT, preferred_element_type=F32)
            yield from _lead(lead)
            z = z_ref[slot]
            sp = jnp.maximum(z, 0.0) + jnp.log(1.0 + jnp.exp2(_neg_abs(z))) * LOG2E
            if keep is not None:
                sp = jnp.where(keep, sp, 0.0)
            sp_b = sp.astype(BF16)
            yield
            btw_ref[slot] = jnp.dot(mtri, sp_b, preferred_element_type=F32)
            yield from _lead(lead)
            between = btw_ref[slot]
            after = carry[h]
            carry[h] = after + (between[0:1, :] - sp_b[0:1, :].astype(F32))
            w = jnp.exp2((z - sp) + between)
            if keep is not None:
                w = jnp.where(keep, w, 0.0)
            w = w.astype(BF16)
            yield
            pv = jnp.dot(vt[rows], w, preferred_element_type=F32)
            if fresh:
                acc_a[rows, :] = pv
            else:
                acc_a[rows, :] += pv * jnp.exp2(after)

        chains = []
        for t, (j, keep, fresh) in enumerate(tiles):
            k = _key_tile(ka_ref, j)
            vt = vta_ref[j]
            chains.extend(chain(k, vt, keep, fresh, h, SB_HEADS * t + h) for h in range(SB_HEADS))
        return chains

    qd_comps = _split_pairs(qd_ref[0], DIFF_HEADS)

    def diff_chains(tiles, m_run, l_run, lead):
        def chain(k, vt, variant, fresh, n, slot):
            h = n // 2
            sd_ref[slot] = lax.dot_general(k[:, h * PAIR_W:(h + 1) * PAIR_W], qd_comps[n], _NT,
                                           preferred_element_type=F32)
            yield from _lead(lead)
            s = sd_ref[slot]
            if variant is not None:
                s = s + dbias_ref[h, variant]
            m_prev = m_run[n]
            m_new = jnp.maximum(m_prev, jnp.max(s, axis=0, keepdims=True))
            alpha = jnp.exp2(m_prev - m_new)
            m_run[n] = m_new
            yield
            p = jnp.exp2(s - m_new).astype(BF16)
            yield
            pvd_ref[slot] = jnp.dot(jnp.concatenate([vt[h * PAIR_W:(h + 1) * PAIR_W], ones_rows], axis=0),
                                    p, preferred_element_type=F32)
            yield from _lead(lead)
            pv = pvd_ref[slot]
            if fresh:
                l_run[n] = pv[PAIR_W:PAIR_W + 1]
                acc_d[n] = pv[:PAIR_W]
            else:
                l_run[n] = alpha * l_run[n] + pv[PAIR_W:PAIR_W + 1]
                acc_d[n] = alpha * acc_d[n] + pv[:PAIR_W]

        chains = []
        for t, (j, variant, fresh) in enumerate(tiles):
            k = _key_tile(kd_ref, j)
            vt = vtd_ref[j]
            chains.extend(chain(k, vt, variant, fresh, n, 2 * DIFF_HEADS * t + n)
                          for n in range(2 * DIFF_HEADS))
        return chains

    def swa_chains(outs, lead):
        qs = qs_ref[0]
        k_hi = _key_tile(ks_ref, i)
        vt_hi = vts_ref[i]
        lo_start = pl.multiple_of(jnp.maximum(i * ATT_TILE - WINDOW, 0), WINDOW)
        k_lo = ks_ref[0, pl.ds(lo_start, 2 * WINDOW), :]
        vt_lo = jnp.where(first, vt_hi,
                          jnp.concatenate([vts_ref[prev][:, WINDOW:], vt_hi[:, :WINDOW]], axis=1))
        lo_variant = jnp.where(first, 1, 0)
        halves = ((_split_pairs(qs[:WINDOW], SWA_Q_HEADS // 2), k_lo, vt_lo, lo_variant),
                  (_split_pairs(qs[WINDOW:], SWA_Q_HEADS // 2), k_hi, vt_hi, 0))

        def chain(q_heads, k_win, vt_win, variant, h, slot):
            kv = h // SWA_GROUP
            k_pair = k_win[:, kv * PAIR_W:(kv + 1) * PAIR_W]
            ss_ref[slot] = lax.dot_general(k_pair, q_heads[h], _NT, preferred_element_type=F32)
            yield from _lead(lead)
            sink = sink_ref[h] * LOG2E
            s = ss_ref[slot] + sbias_ref[h, variant]
            m = jnp.maximum(jnp.max(s, axis=0, keepdims=True), sink)
            e = jnp.exp2(s - m).astype(BF16)
            yield
            pvs_ref[slot] = jnp.dot(
                jnp.concatenate([vt_win[kv * HEAD_DIM:(kv + 1) * HEAD_DIM], ones_rows], axis=0),
                e, preferred_element_type=F32)
            yield from _lead(lead)
            pv = pvs_ref[slot]
            denom = pv[HEAD_DIM:HEAD_DIM + 1] + jnp.exp2(sink - m)
            outs[slot] = pv[:HEAD_DIM] / denom

        return [chain(*half, h, a * SWA_Q_HEADS + h)
                for a, half in enumerate(halves) for h in range(SWA_Q_HEADS)]

    def head_tiles(sb_tiles, diff_tiles):
        carry = [jnp.zeros((1, ATT_TILE), F32)] * SB_HEADS
        m_run = [jnp.full((1, ATT_TILE), -jnp.inf, F32)] * (2 * DIFF_HEADS)
        l_run = [jnp.zeros((1, ATT_TILE), F32)] * (2 * DIFF_HEADS)
        swa = [None] * (2 * SWA_Q_HEADS)
        _interleave(_round_robin([sb_chains(sb_tiles, carry, MXU_LEAD_MIXED),
                                  diff_chains(diff_tiles, m_run, l_run, MXU_LEAD_MIXED),
                                  swa_chains(swa, MXU_LEAD_MIXED)]))
        return tuple(carry), tuple(zip(m_run, l_run)), tuple(swa)

    carries, state, swa = lax.cond(
        i >= 1,
        lambda: head_tiles([(i, strictly_before, True), (i - 1, None, False)],
                           [(i, 0, True), (i - 1, 1, False)]),
        lambda: head_tiles([(i, strictly_before, True)], [(i, 0, True)]))

    def far_tiles(tiles, st):
        m_run = [s[0] for s in st]
        l_run = [s[1] for s in st]
        _interleave(diff_chains(tiles, m_run, l_run, MXU_LEAD_LOOP))
        return tuple(zip(m_run, l_run))

    n_far = jnp.maximum(i - 1, 0)
    state = lax.fori_loop(
        0, lax.shift_right_logical(n_far, 1),
        lambda t, st: far_tiles([(i - 2 - 2 * t, None, False), (i - 3 - 2 * t, None, False)], st), state)
    state = lax.fori_loop(0, n_far & 1, lambda t, st: far_tiles([(0, None, False)], st), state)

    def live(c):
        j, carry = c
        return jnp.logical_and(j >= 0, jnp.max(jnp.concatenate(carry, axis=0)) > SB_DEAD_LOG2)

    def earlier_tile(c):
        j, carry = c
        carry = list(carry)
        _interleave(sb_chains([(j, None, False)], carry, MXU_LEAD_LOOP))
        return j - 1, tuple(carry)

    lax.while_loop(live, earlier_tile, (i - 2, carries))

    o_ref[0, :, 0:SB_W] = acc_a[...].T.astype(BF16)

    lv = lam_ref[...]
    lam = (jnp.exp(jnp.sum(lv[0:1] * lv[1:2], axis=-1, keepdims=True))
           - jnp.exp(jnp.sum(lv[2:3] * lv[3:4], axis=-1, keepdims=True)) + lam_init)
    outs = []
    for h in range(DIFF_HEADS):
        (_, l1), (_, l2) = state[2 * h], state[2 * h + 1]
        o = acc_d[2 * h] / l1 - lam * (acc_d[2 * h + 1] / l2)
        ms = jnp.mean(o * o, axis=0, keepdims=True)
        outs.append(o * lax.rsqrt(ms + EPS) * gain_ref[...] * (1.0 - lam_init))
    o_ref[0, :, SB_W:SB_W + DIFF_W] = jnp.concatenate(outs, axis=0).T.astype(BF16)

    swa_t = jnp.concatenate([jnp.concatenate(swa[:SWA_Q_HEADS], axis=0),
                             jnp.concatenate(swa[SWA_Q_HEADS:], axis=0)], axis=1)
    o_ref[0, :, SB_W + DIFF_W:ATT_OUT_W] = swa_t.T.astype(BF16)


def _attention(sinks, p1, p2, p3, mtri, dbias, sbias, lam_vecs, sub_gain, lam_init, batch, seq):
    nq = seq // ATT_TILE
    p1 = p1.reshape(batch, seq, P1_W)
    p2 = p2.reshape(batch, seq, P2_W)
    kdup_w = 2 * SWA_KV_W
    q_tile = lambda col: pl.BlockSpec((1, ATT_TILE, SB_W), lambda b, i: (b, i, col))
    keys = lambda col: pl.BlockSpec((1, seq, SB_W), lambda b, i: (b, 0, col))
    values_t = lambda rows, blk: pl.BlockSpec((nq, rows, ATT_TILE), lambda b, i: (b, blk, 0))
    whole = lambda shape: pl.BlockSpec(shape, lambda b, i: (0,) * len(shape),
                                       pipeline_mode=pl.Buffered(1))
    assert SB_W == DIFF_W == SWA_Q_W
    return pl.pallas_call(
        functools.partial(_attn_kernel, lam_init=lam_init),
        name="attention",
        grid=(batch, nq),
        in_specs=[
            pl.BlockSpec(memory_space=pltpu.SMEM),
            q_tile(0), keys(1),
            q_tile(0), keys(1),
            q_tile(2),
            pl.BlockSpec((1, seq, kdup_w), lambda b, i: (b, 0, (2 * DIFF_W + SWA_Q_W) // kdup_w)),
            values_t(SB_W, 0), values_t(DIFF_W, 1),
            values_t(SWA_KV_W, (SB_W + DIFF_W) // SWA_KV_W),
            whole((ATT_TILE, ATT_TILE)),
            whole((DIFF_HEADS, 2, ATT_TILE, ATT_TILE)),
            whole((SWA_Q_HEADS, 2, 2 * WINDOW, WINDOW)),
            whole((4, HEAD_DIM)),
            whole((PAIR_W, 1)),
        ],
        out_specs=pl.BlockSpec((1, ATT_TILE, ATT_OUT_W), lambda b, i: (b, i, 0)),
        out_shape=jax.ShapeDtypeStruct((batch, seq, ATT_OUT_W), BF16),
        scratch_shapes=[
            pltpu.VMEM((SB_W, ATT_TILE), F32),
            pltpu.VMEM((2 * SB_HEADS, ATT_TILE, ATT_TILE), F32),
            pltpu.VMEM((2 * SB_HEADS, ATT_TILE, ATT_TILE), F32),
            pltpu.VMEM((2 * DIFF_HEADS, PAIR_W, ATT_TILE), F32),
            pltpu.VMEM((4 * DIFF_HEADS, ATT_TILE, ATT_TILE), F32),
            pltpu.VMEM((4 * DIFF_HEADS, PAIR_W + SUM_ROWS, ATT_TILE), F32),
            pltpu.VMEM((2 * SWA_Q_HEADS, 2 * WINDOW, WINDOW), F32),
            pltpu.VMEM((2 * SWA_Q_HEADS, HEAD_DIM + SUM_ROWS, WINDOW), F32),
        ],
        compiler_params=_compiler_params(("parallel", "arbitrary"), 58 * 1024 * 1024),
    )(sinks, p1, p1, p2, p2, p2, p2, p3, p3, p3, mtri, dbias, sbias, lam_vecs,
      sub_gain.reshape(PAIR_W, 1))


def _merge_ffn_kernel(x_ref, osb_ref, odf_ref, osw_ref, ga_ref, gd_ref, gs_ref,
                      wsb_ref, wdf_ref, wsw_ref, wo_ref, ng_ref, wg_ref, wu_ref, wd_ref, o_ref):
    def branch(o_ref_, w_ref, g_ref):
        gate = 1.0 / (1.0 + jnp.exp(-g_ref[...].astype(F32)))
        return gate * jnp.dot(o_ref_[...], w_ref[...], preferred_element_type=F32)

    merged = (branch(osb_ref, wsb_ref, ga_ref) + branch(odf_ref, wdf_ref, gd_ref)
              + branch(osw_ref, wsw_ref, gs_ref))
    x = x_ref[...] + jnp.dot(merged.astype(BF16), wo_ref[...], preferred_element_type=F32)
    h = _rmsnorm_rows(x, ng_ref[...]).astype(BF16)
    out = x
    for start, width in MERGE_FFN_CHUNKS:
        cols = slice(start, start + width)
        g = jnp.dot(h, wg_ref[:, cols], preferred_element_type=F32)
        u = jnp.dot(h, wu_ref[:, cols], preferred_element_type=F32)
        a = (g * (1.0 / (1.0 + jnp.exp(-g))) * u).astype(BF16)
        out = out + 0.5 * jnp.dot(a, wd_ref[cols, :], preferred_element_type=F32)
    o_ref[...] = out


def _merge_ffn(xt, att, p1, w_sb, w_diff, w_swa, w_out, gain, w_in, w_down, layer):
    t, d = xt.shape
    d_ff = w_down.shape[1]
    assert sum(w for _, w in MERGE_FFN_CHUNKS) == d_ff
    gcol = 2 * SB_W // d
    row = lambda i: (i, 0)
    resident = lambda shape, col=0: pl.BlockSpec((None,) + shape, lambda i: (layer, 0, col),
                                                 pipeline_mode=pl.Buffered(1))
    branch_out = lambda n: pl.BlockSpec((ROW_TILE, SB_W), lambda i: (i, n))
    return pl.pallas_call(
        _merge_ffn_kernel,
        name="merge_ffn",
        grid=(t // ROW_TILE,),
        in_specs=[
            pl.BlockSpec((ROW_TILE, d), row),
            branch_out(0), branch_out(1), branch_out(2),
            pl.BlockSpec((ROW_TILE, d), lambda i: (i, gcol)),
            pl.BlockSpec((ROW_TILE, d), lambda i: (i, gcol + 1)),
            pl.BlockSpec((ROW_TILE, d), lambda i: (i, gcol + 2)),
            resident((SB_W, d)), resident((DIFF_W, d)), resident((SWA_Q_W, d)), resident((d, d)),
            pl.BlockSpec((1, d), lambda i: (0, 0)),
            resident((d, d_ff), 0), resident((d, d_ff), 1), resident((d_ff, d)),
        ],
        out_specs=pl.BlockSpec((ROW_TILE, d), row),
        out_shape=jax.ShapeDtypeStruct((t, d), F32),
        compiler_params=_compiler_params(("parallel",), 60 * 1024 * 1024),
    )(xt, att, att, att, p1, p1, p1, w_sb, w_diff, w_swa, w_out,
      gain.reshape(1, d), w_in, w_in, w_down)


def _qk_norm_gains(q_norm_diff, k_norm_diff, q_norm_swa, k_norm_swa):
    qscale = SCALE * LOG2E
    return jnp.concatenate([
        jnp.tile(q_norm_diff * qscale, DIFF_W // HEAD_DIM), jnp.tile(k_norm_diff, DIFF_W // HEAD_DIM),
        jnp.tile(q_norm_swa * qscale, SWA_Q_W // HEAD_DIM), jnp.tile(k_norm_swa, 2 * SWA_KV_W // HEAD_DIM),
    ]).reshape(1, P2_W).astype(F32)


def _group_mean_matrix():
    g = jnp.arange(P2_CHUNK) // HEAD_DIM
    return jnp.where(g[:, None] == g[None, :], 1.0 / HEAD_DIM, 0.0).astype(BF16)


def _neg_suffix_matrix():
    idx = jnp.arange(ATT_TILE)
    return jnp.where(idx[None, :] > idx[:, None], -1.0, 0.0).astype(BF16)


def kernel(x, ffn1_norm, ffn1_w_in, ffn1_w_out, mix_norm, w_in, q_norm_diff, k_norm_diff, q_norm_swa, k_norm_swa, diff_lambda, diff_subln, swa_sinks, rel_bias, w_proj_sb, w_proj_diff, w_proj_swa, w_out, ffn2_norm, ffn2_w_in, ffn2_w_out):
    batch, seq, d = x.shape
    depth = ffn1_norm.shape[0]
    assert d == D_MODEL and seq % ATT_TILE == 0 and (batch * seq) % ROW_TILE == 0
    xt = x.reshape(batch * seq, d)
    dbias, sbias = _bias_tiles(rel_bias)
    gmat = _group_mean_matrix()
    mtri = _neg_suffix_matrix()
    ffn1_in, ffn1_out = _cast_bf16(ffn1_w_in), _cast_bf16(ffn1_w_out)
    ffn2_in, ffn2_out = _cast_bf16(ffn2_w_in), _cast_bf16(ffn2_w_out)
    w1, w2, w3 = _regroup_projection(w_in)
    w_sb, w_diff, w_swa, w_o = (w.astype(BF16) for w in (w_proj_sb, w_proj_diff, w_proj_swa, w_out))
    for l in range(depth):
        lam_init = 0.8 - 0.6 * math.exp(-0.3 * l)
        xt = _ffn(xt, ffn1_norm[l], ffn1_in, ffn1_out, l)
        g2 = _qk_norm_gains(q_norm_diff[l], k_norm_diff[l], q_norm_swa[l], k_norm_swa[l])
        p1, p2, p3 = _projection(xt, mix_norm[l], w1, w2, g2, gmat, w3, l)
        att = _attention(swa_sinks[l], p1, p2, p3, mtri, dbias, sbias, diff_lambda[l],
                         diff_subln[l], lam_init, batch, seq)
        xt = _merge_ffn(xt, att.reshape(batch * seq, ATT_OUT_W), p1, w_sb, w_diff, w_swa, w_o,
                        ffn2_norm[l], ffn2_in, ffn2_out, l)
    return xt.reshape(batch, seq, d)
```

```python
import functools
import math

import jax
import jax.numpy as jnp
from jax import lax
from jax.experimental import pallas as pl
from jax.experimental.pallas import tpu as pltpu

F32 = jnp.float32
BF16 = jnp.bfloat16

D_MODEL = 1024
HEAD_DIM = 64
SB_HEADS = 8
DIFF_HEADS = 4
SWA_Q_HEADS = 8
SWA_KV_HEADS = 2
SWA_GROUP = SWA_Q_HEADS // SWA_KV_HEADS
WINDOW = 128
N_BUCKETS = 32
MAX_DISTANCE = 128
EPS = 1e-6
SCALE = HEAD_DIM ** -0.5
LOG2E = math.log2(math.e)
SB_DEAD_LOG2 = -160.0
MXU_LEAD_MIXED = 2
MXU_LEAD_LOOP = 4

SB_W = SB_HEADS * HEAD_DIM
DIFF_W = DIFF_HEADS * 2 * HEAD_DIM
SWA_Q_W = SWA_Q_HEADS * HEAD_DIM
SWA_KV_W = SWA_KV_HEADS * HEAD_DIM
ATT_OUT_W = SB_W + DIFF_W + SWA_Q_W

V7X_LANES = 128
V7X_MXU_DIM = 256
V7X_VMEM_BYTES = 64 * 1024 * 1024

ATT_TILE = V7X_MXU_DIM
ROW_TILE = 512
CAST_ROWS = 256
PAIR_W = 2 * HEAD_DIM
SUM_ROWS = 16
assert PAIR_W == V7X_LANES and 2 * WINDOW == ATT_TILE

P1_W = 2 * SB_W + 3 * D_MODEL
P2_W = 2 * DIFF_W + SWA_Q_W + 2 * SWA_KV_W
P3_W = SB_W + DIFF_W + SWA_KV_W
P1_CHUNK = 512
P2_CHUNK = 256
P3_CHUNKS = ((0, 256), (256, 256), (512, 256), (768, 256), (1024, 128))
assert sum(w for _, w in P3_CHUNKS) == P3_W
MERGE_FFN_CHUNKS = ((0, 6 * V7X_MXU_DIM), (6 * V7X_MXU_DIM, 5 * V7X_MXU_DIM))

_NT = (((1,), (1,)), ((), ()))


def _compiler_params(semantics, vmem_bytes):
    assert vmem_bytes < V7X_VMEM_BYTES
    return pltpu.CompilerParams(dimension_semantics=semantics, vmem_limit_bytes=vmem_bytes)


def _rmsnorm_rows(x, g):
    ms = jnp.mean(x * x, axis=-1, keepdims=True)
    return x * lax.rsqrt(ms + EPS) * g


def _cast_kernel(w_ref, o_ref):
    o_ref[...] = w_ref[...].astype(BF16)


def _cast_bf16(w):
    depth, rows, cols = w.shape
    assert rows % CAST_ROWS == 0
    spec = pl.BlockSpec((None, CAST_ROWS, cols), lambda l, r: (l, r, 0))
    return pl.pallas_call(
        _cast_kernel,
        name="cast_bf16",
        grid=(depth, rows // CAST_ROWS),
        in_specs=[spec],
        out_specs=spec,
        out_shape=jax.ShapeDtypeStruct(w.shape, BF16),
        compiler_params=_compiler_params(("parallel", "parallel"), 40 * 1024 * 1024),
    )(w)


def _regroup_kernel(w_ref, w1_ref, w2_ref, w3_ref):
    def cols(start, width):
        return w_ref[:, start:start + width]

    offs = {}
    acc = 0
    for name, width in (("qa", SB_W), ("ka", SB_W), ("va", SB_W), ("qd", DIFF_W), ("kd", DIFF_W),
                        ("vd", DIFF_W), ("qs", SWA_Q_W), ("ks", SWA_KV_W), ("vs", SWA_KV_W),
                        ("gates", 3 * D_MODEL)):
        offs[name] = (acc, width)
        acc += width
    w1_ref[:, 0:SB_W] = (cols(*offs["qa"]) * (SCALE * LOG2E)).astype(BF16)
    w1_ref[:, SB_W:2 * SB_W] = cols(*offs["ka"]).astype(BF16)
    w1_ref[:, 2 * SB_W:P1_W] = cols(*offs["gates"]).astype(BF16)
    w2_ref[:, 0:DIFF_W] = cols(*offs["qd"]).astype(BF16)
    w2_ref[:, DIFF_W:2 * DIFF_W] = cols(*offs["kd"]).astype(BF16)
    w2_ref[:, 2 * DIFF_W:2 * DIFF_W + SWA_Q_W] = cols(*offs["qs"]).astype(BF16)
    ks = cols(*offs["ks"])
    swapped = pltpu.roll(ks, HEAD_DIM, axis=1)
    lane = lax.broadcasted_iota(jnp.int32, ks.shape, 1)
    base = 2 * DIFF_W + SWA_Q_W
    w2_ref[:, base:base + PAIR_W] = jnp.where(lane < HEAD_DIM, ks, swapped).astype(BF16)
    w2_ref[:, base + PAIR_W:P2_W] = jnp.where(lane < HEAD_DIM, swapped, ks).astype(BF16)
    w3_ref[:, 0:SB_W] = cols(*offs["va"]).astype(BF16)
    w3_ref[:, SB_W:SB_W + DIFF_W] = cols(*offs["vd"]).astype(BF16)
    w3_ref[:, SB_W + DIFF_W:P3_W] = cols(*offs["vs"]).astype(BF16)


def _regroup_projection(w_in):
    depth, d, width = w_in.shape
    assert SWA_KV_W == PAIR_W and d % CAST_ROWS == 0
    block = lambda cols: pl.BlockSpec((None, CAST_ROWS, cols), lambda l, r: (l, r, 0))
    return pl.pallas_call(
        _regroup_kernel,
        name="regroup_projection",
        grid=(depth, d // CAST_ROWS),
        in_specs=[block(width)],
        out_specs=[block(P1_W), block(P2_W), block(P3_W)],
        out_shape=[jax.ShapeDtypeStruct((depth, d, w), BF16) for w in (P1_W, P2_W, P3_W)],
        compiler_params=_compiler_params(("parallel", "parallel"), 40 * 1024 * 1024),
    )(w_in)


def _ffn_kernel(x_ref, g_ref, wg_ref, wu_ref, wo_ref, o_ref):
    x = x_ref[...]
    h = _rmsnorm_rows(x, g_ref[...]).astype(BF16)
    g = jnp.dot(h, wg_ref[...], preferred_element_type=F32)
    u = jnp.dot(h, wu_ref[...], preferred_element_type=F32)
    a = (g * (1.0 / (1.0 + jnp.exp(-g))) * u).astype(BF16)
    o_ref[...] = x + 0.5 * jnp.dot(a, wo_ref[...], preferred_element_type=F32)


def _ffn(xt, gain, w_in, w_out, layer):
    t, d = xt.shape
    d_ff = w_out.shape[1]
    assert d_ff % V7X_MXU_DIM == 0 and t % ROW_TILE == 0
    resident = pl.Buffered(1)
    return pl.pallas_call(
        _ffn_kernel,
        name="ffn",
        grid=(t // ROW_TILE,),
        in_specs=[
            pl.BlockSpec((ROW_TILE, d), lambda i: (i, 0)),
            pl.BlockSpec((1, d), lambda i: (0, 0)),
            pl.BlockSpec((None, d, d_ff), lambda i: (layer, 0, 0), pipeline_mode=resident),
            pl.BlockSpec((None, d, d_ff), lambda i: (layer, 0, 1), pipeline_mode=resident),
            pl.BlockSpec((None, d_ff, d), lambda i: (layer, 0, 0), pipeline_mode=resident),
        ],
        out_specs=pl.BlockSpec((ROW_TILE, d), lambda i: (i, 0)),
        out_shape=jax.ShapeDtypeStruct((t, d), F32),
        compiler_params=_compiler_params(("parallel",), 56 * 1024 * 1024),
    )(xt, gain.reshape(1, d), w_in, w_in, w_out)


def _proj_kernel(x_ref, ng_ref, w1_ref, w2_ref, g2_ref, gm_ref, w3_ref,
                 p1_ref, p2_ref, p3_ref):
    h = _rmsnorm_rows(x_ref[...], ng_ref[...]).astype(BF16)
    for c in range(P1_W // P1_CHUNK):
        cols = slice(c * P1_CHUNK, (c + 1) * P1_CHUNK)
        p1_ref[:, cols] = jnp.dot(h, w1_ref[:, cols], preferred_element_type=F32).astype(BF16)
    y_all = jnp.dot(h, w2_ref[...], preferred_element_type=F32)
    for c in range(P2_W // P2_CHUNK):
        cols = slice(c * P2_CHUNK, (c + 1) * P2_CHUNK)
        y = y_all[:, cols]
        ms = jnp.dot((y * y).astype(BF16), gm_ref[...], preferred_element_type=F32)
        p2_ref[:, cols] = (y * lax.rsqrt(ms + EPS) * g2_ref[:, cols]).astype(BF16)
    for start, width in P3_CHUNKS:
        cols = slice(start, start + width)
        yt = jnp.dot(h, w3_ref[:, cols], preferred_element_type=F32).T
        for r in range(ROW_TILE // ATT_TILE):
            p3_ref[r, cols, :] = yt[:, r * ATT_TILE:(r + 1) * ATT_TILE].astype(BF16)


def _projection(xt, norm_gain, w1, w2, g2, gmat, w3, layer):
    t, d = xt.shape
    const = lambda i: (0, 0)
    weights = lambda cols: pl.BlockSpec((None, d, cols), lambda i: (layer, 0, 0),
                                        pipeline_mode=pl.Buffered(1))
    return pl.pallas_call(
        _proj_kernel,
        name="mixer_proj",
        grid=(t // ROW_TILE,),
        in_specs=[
            pl.BlockSpec((ROW_TILE, d), lambda i: (i, 0)),
            pl.BlockSpec((1, d), const),
            weights(P1_W),
            weights(P2_W),
            pl.BlockSpec((1, P2_W), const),
            pl.BlockSpec((P2_CHUNK, P2_CHUNK), const),
            weights(P3_W),
        ],
        out_specs=[
            pl.BlockSpec((ROW_TILE, P1_W), lambda i: (i, 0)),
            pl.BlockSpec((ROW_TILE, P2_W), lambda i: (i, 0)),
            pl.BlockSpec((ROW_TILE // ATT_TILE, P3_W, ATT_TILE), lambda i: (i, 0, 0)),
        ],
        out_shape=[
            jax.ShapeDtypeStruct((t, P1_W), BF16),
            jax.ShapeDtypeStruct((t, P2_W), BF16),
            jax.ShapeDtypeStruct((t // ATT_TILE, P3_W, ATT_TILE), BF16),
        ],
        compiler_params=_compiler_params(("parallel",), 52 * 1024 * 1024),
    )(xt, norm_gain.reshape(1, d), w1, w2, g2, gmat, w3)


def _t5_bucket(dist):
    n = jnp.maximum(dist, 0)
    max_exact = N_BUCKETS // 2
    nf = jnp.maximum(n, 1).astype(F32)
    large = max_exact + (jnp.log(nf / max_exact) / math.log(MAX_DISTANCE / max_exact)
                         * (N_BUCKETS - max_exact)).astype(jnp.int32)
    large = jnp.minimum(large, N_BUCKETS - 1)
    return jnp.where(n < max_exact, n, large)


def _bias_lookup(bucket, tab_ref, head):
    val = jnp.zeros(bucket.shape, F32)
    for b in range(N_BUCKETS):
        val = jnp.where(bucket == b, tab_ref[b, head], val)
    return val


def _bias_kernel(tab_ref, dbias_ref, sbias_ref):
    kk = lax.broadcasted_iota(jnp.int32, (ATT_TILE, ATT_TILE), 0)
    qq = lax.broadcasted_iota(jnp.int32, (ATT_TILE, ATT_TILE), 1)
    for v in range(2):
        dist = qq - kk + v * ATT_TILE
        bucket = _t5_bucket(dist)
        for h in range(DIFF_HEADS):
            val = (_bias_lookup(bucket, tab_ref, h) - tab_ref[N_BUCKETS - 1, h]) * LOG2E
            dbias_ref[h, v] = jnp.where(dist >= 0, val, -jnp.inf)
    kk = lax.broadcasted_iota(jnp.int32, (2 * WINDOW, WINDOW), 0)
    qq = lax.broadcasted_iota(jnp.int32, (2 * WINDOW, WINDOW), 1)
    for v in range(2):
        dist = qq - kk + (1 - v) * WINDOW
        bucket = _t5_bucket(dist)
        for h in range(SWA_Q_HEADS):
            val = _bias_lookup(bucket, tab_ref, DIFF_HEADS + h) * LOG2E
            sbias_ref[h, v] = jnp.where(dist >= 0, jnp.where(dist < WINDOW, val, -jnp.inf), -jnp.inf)


def _bias_tiles(rel_bias):
    return pl.pallas_call(
        _bias_kernel,
        name="bias_tiles",
        in_specs=[pl.BlockSpec(memory_space=pltpu.SMEM)],
        out_specs=[pl.BlockSpec(memory_space=pltpu.VMEM), pl.BlockSpec(memory_space=pltpu.VMEM)],
        out_shape=[
            jax.ShapeDtypeStruct((DIFF_HEADS, 2, ATT_TILE, ATT_TILE), F32),
            jax.ShapeDtypeStruct((SWA_Q_HEADS, 2, 2 * WINDOW, WINDOW), F32),
        ],
    )(rel_bias)


def _split_pairs(q, n_pairs):
    lane = lax.broadcasted_iota(jnp.int32, (1, PAIR_W), 1)
    low = jnp.where(lane < HEAD_DIM, 1.0, 0.0).astype(BF16)
    high = jnp.where(lane >= HEAD_DIM, 1.0, 0.0).astype(BF16)
    heads = []
    for p in range(n_pairs):
        pair = q[:, p * PAIR_W:(p + 1) * PAIR_W]
        heads.append(pair * low)
        heads.append(pair * high)
    return heads


def _neg_abs(x):
    bits = lax.bitcast_convert_type(x, jnp.uint32) | jnp.uint32(0x80000000)
    return lax.bitcast_convert_type(bits, F32)


def _lead(rounds):
    for _ in range(rounds):
        yield


def _interleave(chains):
    live = []
    started = 0
    while started < len(chains) or live:
        if started < len(chains):
            live.append(started)
            started += 1
        for n in reversed(list(live)):
            try:
                next(chains[n])
            except StopIteration:
                live.remove(n)


def _round_robin(groups):
    merged = []
    for n in range(max(len(g) for g in groups)):
        merged.extend(g[n] for g in groups if n < len(g))
    return merged


def _key_tile(ref, j):
    return ref[0, pl.ds(pl.multiple_of(j * ATT_TILE, ATT_TILE), ATT_TILE), :]


def _attn_kernel(sink_ref, qa_ref, ka_ref, qd_ref, kd_ref, qs_ref, ks_ref,
                 vta_ref, vtd_ref, vts_ref, mtri_ref, dbias_ref, sbias_ref, lam_ref, gain_ref,
                 o_ref, acc_a, z_ref, btw_ref, acc_d, sd_ref, pvd_ref, ss_ref, pvs_ref,
                 *, lam_init):
    i = pl.program_id(1)
    first = i == 0
    prev = jnp.maximum(i - 1, 0)
    ones_rows = jnp.ones((SUM_ROWS, ATT_TILE), BF16)

    qa_heads = _split_pairs(qa_ref[0], SB_HEADS // 2)
    mtri = mtri_ref[...]
    kk = lax.broadcasted_iota(jnp.int32, (ATT_TILE, ATT_TILE), 0)
    qq = lax.broadcasted_iota(jnp.int32, (ATT_TILE, ATT_TILE), 1)
    strictly_before = kk < qq

    def sb_chains(tiles, carry, lead):
        def chain(k, vt, keep, fresh, h, slot):
            k_pair = k[:, (h // 2) * PAIR_W:(h // 2 + 1) * PAIR_W]
            rows = slice(h * HEAD_DIM, (h + 1) * HEAD_DIM)
            z_ref[slot] = lax.dot_general(k_pair, qa_heads[h], _NT, preferred_element_type=F32)
            yield from _lead(lead)
            z = z_ref[slot]
            sp = jnp.maximum(z, 0.0) + jnp.log(1.0 + jnp.exp2(_neg_abs(z))) * LOG2E
            if keep is not None:
                sp = jnp.where(keep, sp, 0.0)
            sp_b = sp.astype(BF16)
            yield
            btw_ref[slot] = jnp.dot(mtri, sp_b, preferred_element_type=F32)
            yield from _lead(lead)
            between = btw_ref[slot]
            after = carry[h]
            carry[h] = after + (between[0:1, :] - sp_b[0:1, :].astype(F32))
            w = jnp.exp2((z - sp) + between)
            if keep is not None:
                w = jnp.where(keep, w, 0.0)
            w = w.astype(BF16)
            yield
            pv = jnp.dot(vt[rows], w, preferred_element_type=F32)
            if fresh:
                acc_a[rows, :] = pv
            else:
                acc_a[rows, :] += pv * jnp.exp2(after)

        chains = []
        for t, (j, keep, fresh) in enumerate(tiles):
            k = _key_tile(ka_ref, j)
            vt = vta_ref[j]
            chains.extend(chain(k, vt, keep, fresh, h, SB_HEADS * t + h) for h in range(SB_HEADS))
        return chains

    qd_comps = _split_pairs(qd_ref[0], DIFF_HEADS)

    def diff_chains(tiles, m_run, l_run, lead):
        def chain(k, vt, variant, fresh, n, slot):
            h = n // 2
            sd_ref[slot] = lax.dot_general(k[:, h * PAIR_W:(h + 1) * PAIR_W], qd_comps[n], _NT,
                                           preferred_element_type=F32)
            yield from _lead(lead)
            s = sd_ref[slot]
            if variant is not None:
                s = s + dbias_ref[h, variant]
            m_prev = m_run[n]
            m_new = jnp.maximum(m_prev, jnp.max(s, axis=0, keepdims=True))
            alpha = jnp.exp2(m_prev - m_new)
            m_run[n] = m_new
            yield
            p = jnp.exp2(s - m_new).astype(BF16)
            yield
            pvd_ref[slot] = jnp.dot(jnp.concatenate([vt[h * PAIR_W:(h + 1) * PAIR_W], ones_rows], axis=0),
                                    p, preferred_element_type=F32)
            yield from _lead(lead)
            pv = pvd_ref[slot]
            if fresh:
                l_run[n] = pv[PAIR_W:PAIR_W + 1]
                acc_d[n] = pv[:PAIR_W]
            else:
                l_run[n] = alpha * l_run[n] + pv[PAIR_W:PAIR_W + 1]
                acc_d[n] = alpha * acc_d[n] + pv[:PAIR_W]

        chains = []
        for t, (j, variant, fresh) in enumerate(tiles):
            k = _key_tile(kd_ref, j)
            vt = vtd_ref[j]
            chains.extend(chain(k, vt, variant, fresh, n, 2 * DIFF_HEADS * t + n)
                          for n in range(2 * DIFF_HEADS))
        return chains

    def diff_wide_chains(j_low, m_run, l_run, lead):
        k = kd_ref[0, pl.ds(pl.multiple_of(j_low * ATT_TILE, ATT_TILE), 2 * ATT_TILE), :]
        vt = jnp.concatenate([vtd_ref[j_low], vtd_ref[j_low + 1]], axis=1)
        ones_wide = jnp.ones((SUM_ROWS, 2 * ATT_TILE), BF16)

        def chain(n):
            h = n // 2
            s = lax.dot_general(k[:, h * PAIR_W:(h + 1) * PAIR_W], qd_comps[n], _NT,
                                preferred_element_type=F32)
            sd_ref[2 * n:2 * n + 2] = s.reshape(2, ATT_TILE, ATT_TILE)
            yield from _lead(lead)
            s = sd_ref[2 * n:2 * n + 2].reshape(2 * ATT_TILE, ATT_TILE)
            m_prev = m_run[n]
            m_new = jnp.maximum(m_prev, jnp.max(s, axis=0, keepdims=True))
            alpha = jnp.exp2(m_prev - m_new)
            m_run[n] = m_new
            yield
            p = jnp.exp2(s - m_new).astype(BF16)
            yield
            pvd_ref[n] = jnp.dot(jnp.concatenate([vt[h * PAIR_W:(h + 1) * PAIR_W], ones_wide], axis=0),
                                 p, preferred_element_type=F32)
            yield from _lead(lead)
            pv = pvd_ref[n]
            l_run[n] = alpha * l_run[n] + pv[PAIR_W:PAIR_W + 1]
            acc_d[n] = alpha * acc_d[n] + pv[:PAIR_W]

        return [chain(n) for n in range(2 * DIFF_HEADS)]

    def swa_chains(outs, lead):
        qs = qs_ref[0]
        k_hi = _key_tile(ks_ref, i)
        vt_hi = vts_ref[i]
        lo_start = pl.multiple_of(jnp.maximum(i * ATT_TILE - WINDOW, 0), WINDOW)
        k_lo = ks_ref[0, pl.ds(lo_start, 2 * WINDOW), :]
        vt_lo = jnp.where(first, vt_hi,
                          jnp.concatenate([vts_ref[prev][:, WINDOW:], vt_hi[:, :WINDOW]], axis=1))
        lo_variant = jnp.where(first, 1, 0)
        halves = ((_split_pairs(qs[:WINDOW], SWA_Q_HEADS // 2), k_lo, vt_lo, lo_variant),
                  (_split_pairs(qs[WINDOW:], SWA_Q_HEADS // 2), k_hi, vt_hi, 0))

        def chain(q_heads, k_win, vt_win, variant, h, slot):
            kv = h // SWA_GROUP
            k_pair = k_win[:, kv * PAIR_W:(kv + 1) * PAIR_W]
            ss_ref[slot] = lax.dot_general(k_pair, q_heads[h], _NT, preferred_element_type=F32)
            yield from _lead(lead)
            sink = sink_ref[h] * LOG2E
            s = ss_ref[slot] + sbias_ref[h, variant]
            m = jnp.maximum(jnp.max(s, axis=0, keepdims=True), sink)
            e = jnp.exp2(s - m).astype(BF16)
            yield
            pvs_ref[slot] = jnp.dot(
                jnp.concatenate([vt_win[kv * HEAD_DIM:(kv + 1) * HEAD_DIM], ones_rows], axis=0),
                e, preferred_element_type=F32)
            yield from _lead(lead)
            pv = pvs_ref[slot]
            denom = pv[HEAD_DIM:HEAD_DIM + 1] + jnp.exp2(sink - m)
            outs[slot] = pv[:HEAD_DIM] / denom

        return [chain(*half, h, a * SWA_Q_HEADS + h)
                for a, half in enumerate(halves) for h in range(SWA_Q_HEADS)]

    def head_tiles(sb_tiles, diff_tiles):
        carry = [jnp.zeros((1, ATT_TILE), F32)] * SB_HEADS
        m_run = [jnp.full((1, ATT_TILE), -jnp.inf, F32)] * (2 * DIFF_HEADS)
        l_run = [jnp.zeros((1, ATT_TILE), F32)] * (2 * DIFF_HEADS)
        swa = [None] * (2 * SWA_Q_HEADS)
        _interleave(_round_robin([sb_chains(sb_tiles, carry, MXU_LEAD_MIXED),
                                  diff_chains(diff_tiles, m_run, l_run, MXU_LEAD_MIXED),
                                  swa_chains(swa, MXU_LEAD_MIXED)]))
        swa_t = jnp.concatenate([jnp.concatenate(swa[:SWA_Q_HEADS], axis=0),
                                 jnp.concatenate(swa[SWA_Q_HEADS:], axis=0)], axis=1)
        o_ref[0, :, SB_W + DIFF_W:ATT_OUT_W] = swa_t.T.astype(BF16)
        return tuple(carry), tuple(zip(m_run, l_run))

    carries, state = lax.cond(
        i >= 1,
        lambda: head_tiles([(i, strictly_before, True), (i - 1, None, False)],
                           [(i, 0, True), (i - 1, 1, False)]),
        lambda: head_tiles([(i, strictly_before, True)], [(i, 0, True)]))

    def far_tiles(make_chains, st):
        m_run = [s[0] for s in st]
        l_run = [s[1] for s in st]
        _interleave(make_chains(m_run, l_run))
        return tuple(zip(m_run, l_run))

    n_far = jnp.maximum(i - 1, 0)
    state = lax.fori_loop(
        0, lax.shift_right_logical(n_far, 1),
        lambda t, st: far_tiles(
            lambda m, l: diff_wide_chains(i - 3 - 2 * t, m, l, MXU_LEAD_LOOP), st), state)
    state = lax.fori_loop(
        0, n_far & 1,
        lambda t, st: far_tiles(
            lambda m, l: diff_chains([(0, None, False)], m, l, MXU_LEAD_LOOP), st), state)

    def live(c):
        j, carry = c
        return jnp.logical_and(j >= 0, jnp.max(jnp.concatenate(carry, axis=0)) > SB_DEAD_LOG2)

    def earlier_tile(c):
        j, carry = c
        carry = list(carry)
        _interleave(sb_chains([(j, None, False)], carry, MXU_LEAD_LOOP))
        return j - 1, tuple(carry)

    lax.while_loop(live, earlier_tile, (i - 2, carries))

    o_ref[0, :, 0:SB_W] = acc_a[...].T.astype(BF16)

    lv = lam_ref[...]
    lam = (jnp.exp(jnp.sum(lv[0:1] * lv[1:2], axis=-1, keepdims=True))
           - jnp.exp(jnp.sum(lv[2:3] * lv[3:4], axis=-1, keepdims=True)) + lam_init)
    outs = []
    for h in range(DIFF_HEADS):
        (_, l1), (_, l2) = state[2 * h], state[2 * h + 1]
        o = acc_d[2 * h] / l1 - lam * (acc_d[2 * h + 1] / l2)
        ms = jnp.mean(o * o, axis=0, keepdims=True)
        outs.append(o * lax.rsqrt(ms + EPS) * gain_ref[...] * (1.0 - lam_init))
    o_ref[0, :, SB_W:SB_W + DIFF_W] = jnp.concatenate(outs, axis=0).T.astype(BF16)


def _attention(sinks, p1, p2, p3, mtri, dbias, sbias, lam_vecs, sub_gain, lam_init, batch, seq):
    nq = seq // ATT_TILE
    p1 = p1.reshape(batch, seq, P1_W)
    p2 = p2.reshape(batch, seq, P2_W)
    kdup_w = 2 * SWA_KV_W
    q_tile = lambda col: pl.BlockSpec((1, ATT_TILE, SB_W), lambda b, i: (b, i, col))
    keys = lambda col: pl.BlockSpec((1, seq, SB_W), lambda b, i: (b, 0, col))
    values_t = lambda rows, blk: pl.BlockSpec((nq, rows, ATT_TILE), lambda b, i: (b, blk, 0))
    whole = lambda shape: pl.BlockSpec(shape, lambda b, i: (0,) * len(shape),
                                       pipeline_mode=pl.Buffered(1))
    assert SB_W == DIFF_W == SWA_Q_W
    return pl.pallas_call(
        functools.partial(_attn_kernel, lam_init=lam_init),
        name="attention",
        grid=(batch, nq),
        in_specs=[
            pl.BlockSpec(memory_space=pltpu.SMEM),
            q_tile(0), keys(1),
            q_tile(0), keys(1),
            q_tile(2),
            pl.BlockSpec((1, seq, kdup_w), lambda b, i: (b, 0, (2 * DIFF_W + SWA_Q_W) // kdup_w)),
            values_t(SB_W, 0), values_t(DIFF_W, 1),
            values_t(SWA_KV_W, (SB_W + DIFF_W) // SWA_KV_W),
            whole((ATT_TILE, ATT_TILE)),
            whole((DIFF_HEADS, 2, ATT_TILE, ATT_TILE)),
            whole((SWA_Q_HEADS, 2, 2 * WINDOW, WINDOW)),
            whole((4, HEAD_DIM)),
            whole((PAIR_W, 1)),
        ],
        out_specs=pl.BlockSpec((1, ATT_TILE, ATT_OUT_W), lambda b, i: (b, i, 0)),
        out_shape=jax.ShapeDtypeStruct((batch, seq, ATT_OUT_W), BF16),
        scratch_shapes=[
            pltpu.VMEM((SB_W, ATT_TILE), F32),
            pltpu.VMEM((2 * SB_HEADS, ATT_TILE, ATT_TILE), F32),
            pltpu.VMEM((2 * SB_HEADS, ATT_TILE, ATT_TILE), F32),
            pltpu.VMEM((2 * DIFF_HEADS, PAIR_W, ATT_TILE), F32),
            pltpu.VMEM((4 * DIFF_HEADS, ATT_TILE, ATT_TILE), F32),
            pltpu.VMEM((4 * DIFF_HEADS, PAIR_W + SUM_ROWS, ATT_TILE), F32),
            pltpu.VMEM((2 * SWA_Q_HEADS, 2 * WINDOW, WINDOW), F32),
            pltpu.VMEM((2 * SWA_Q_HEADS, HEAD_DIM + SUM_ROWS, WINDOW), F32),
        ],
        compiler_params=_compiler_params(("parallel", "arbitrary"), 58 * 1024 * 1024),
    )(sinks, p1, p1, p2, p2, p2, p2, p3, p3, p3, mtri, dbias, sbias, lam_vecs,
      sub_gain.reshape(PAIR_W, 1))


def _merge_ffn_kernel(x_ref, osb_ref, odf_ref, osw_ref, ga_ref, gd_ref, gs_ref,
                      wsb_ref, wdf_ref, wsw_ref, wo_ref, ng_ref, wg_ref, wu_ref, wd_ref, o_ref):
    def branch(o_ref_, w_ref, g_ref):
        gate = 1.0 / (1.0 + jnp.exp(-g_ref[...].astype(F32)))
        return gate * jnp.dot(o_ref_[...], w_ref[...], preferred_element_type=F32)

    merged = (branch(osb_ref, wsb_ref, ga_ref) + branch(odf_ref, wdf_ref, gd_ref)
              + branch(osw_ref, wsw_ref, gs_ref))
    x = x_ref[...] + jnp.dot(merged.astype(BF16), wo_ref[...], preferred_element_type=F32)
    h = _rmsnorm_rows(x, ng_ref[...]).astype(BF16)
    out = x
    for start, width in MERGE_FFN_CHUNKS:
        cols = slice(start, start + width)
        g = jnp.dot(h, wg_ref[:, cols], preferred_element_type=F32)
        u = jnp.dot(h, wu_ref[:, cols], preferred_element_type=F32)
        a = (g * (1.0 / (1.0 + jnp.exp(-g))) * u).astype(BF16)
        out = out + 0.5 * jnp.dot(a, wd_ref[cols, :], preferred_element_type=F32)
    o_ref[...] = out


def _merge_ffn(xt, att, p1, w_sb, w_diff, w_swa, w_out, gain, w_in, w_down, layer):
    t, d = xt.shape
    d_ff = w_down.shape[1]
    assert sum(w for _, w in MERGE_FFN_CHUNKS) == d_ff
    gcol = 2 * SB_W // d
    row = lambda i: (i, 0)
    resident = lambda shape, col=0: pl.BlockSpec((None,) + shape, lambda i: (layer, 0, col),
                                                 pipeline_mode=pl.Buffered(1))
    branch_out = lambda n: pl.BlockSpec((ROW_TILE, SB_W), lambda i: (i, n))
    return pl.pallas_call(
        _merge_ffn_kernel,
        name="merge_ffn",
        grid=(t // ROW_TILE,),
        in_specs=[
            pl.BlockSpec((ROW_TILE, d), row),
            branch_out(0), branch_out(1), branch_out(2),
            pl.BlockSpec((ROW_TILE, d), lambda i: (i, gcol)),
            pl.BlockSpec((ROW_TILE, d), lambda i: (i, gcol + 1)),
            pl.BlockSpec((ROW_TILE, d), lambda i: (i, gcol + 2)),
            resident((SB_W, d)), resident((DIFF_W, d)), resident((SWA_Q_W, d)), resident((d, d)),
            pl.BlockSpec((1, d), lambda i: (0, 0)),
            resident((d, d_ff), 0), resident((d, d_ff), 1), resident((d_ff, d)),
        ],
        out_specs=pl.BlockSpec((ROW_TILE, d), row),
        out_shape=jax.ShapeDtypeStruct((t, d), F32),
        compiler_params=_compiler_params(("parallel",), 60 * 1024 * 1024),
    )(xt, att, att, att, p1, p1, p1, w_sb, w_diff, w_swa, w_out,
      gain.reshape(1, d), w_in, w_in, w_down)


def _qk_norm_gains(q_norm_diff, k_norm_diff, q_norm_swa, k_norm_swa):
    qscale = SCALE * LOG2E
    return jnp.concatenate([
        jnp.tile(q_norm_diff * qscale, DIFF_W // HEAD_DIM), jnp.tile(k_norm_diff, DIFF_W // HEAD_DIM),
        jnp.tile(q_norm_swa * qscale, SWA_Q_W // HEAD_DIM), jnp.tile(k_norm_swa, 2 * SWA_KV_W // HEAD_DIM),
    ]).reshape(1, P2_W).astype(F32)


def _group_mean_matrix():
    g = jnp.arange(P2_CHUNK) // HEAD_DIM
    return jnp.where(g[:, None] == g[None, :], 1.0 / HEAD_DIM, 0.0).astype(BF16)


def _neg_suffix_matrix():
    idx = jnp.arange(ATT_TILE)
    return jnp.where(idx[None, :] > idx[:, None], -1.0, 0.0).astype(BF16)


def kernel(x, ffn1_norm, ffn1_w_in, ffn1_w_out, mix_norm, w_in, q_norm_diff, k_norm_diff, q_norm_swa, k_norm_swa, diff_lambda, diff_subln, swa_sinks, rel_bias, w_proj_sb, w_proj_diff, w_proj_swa, w_out, ffn2_norm, ffn2_w_in, ffn2_w_out):
    batch, seq, d = x.shape
    depth = ffn1_norm.shape[0]
    assert d == D_MODEL and seq % ATT_TILE == 0 and (batch * seq) % ROW_TILE == 0
    xt = x.reshape(batch * seq, d)
    dbias, sbias = _bias_tiles(rel_bias)
    gmat = _group_mean_matrix()
    mtri = _neg_suffix_matrix()
    ffn1_in, ffn1_out = _cast_bf16(ffn1_w_in), _cast_bf16(ffn1_w_out)
    ffn2_in, ffn2_out = _cast_bf16(ffn2_w_in), _cast_bf16(ffn2_w_out)
    w1, w2, w3 = _regroup_projection(w_in)
    w_sb, w_diff, w_swa, w_o = (w.astype(BF16) for w in (w_proj_sb, w_proj_diff, w_proj_swa, w_out))
    for l in range(depth):
        lam_init = 0.8 - 0.6 * math.exp(-0.3 * l)
        xt = _ffn(xt, ffn1_norm[l], ffn1_in, ffn1_out, l)
        g2 = _qk_norm_gains(q_norm_diff[l], k_norm_diff[l], q_norm_swa[l], k_norm_swa[l])
        p1, p2, p3 = _projection(xt, mix_norm[l], w1, w2, g2, gmat, w3, l)
        att = _attention(swa_sinks[l], p1, p2, p3, mtri, dbias, sbias, diff_lambda[l],
                         diff_subln[l], lam_init, batch, seq)
        xt = _merge_ffn(xt, att.reshape(batch * seq, ATT_OUT_W), p1, w_sb, w_diff, w_swa, w_o,
                        ffn2_norm[l], ffn2_in, ffn2_out, l)
    return xt.reshape(batch, seq, d)
```

```python
import functools
import math

import jax
import jax.numpy as jnp
from jax import lax
from jax.experimental import pallas as pl
from jax.experimental.pallas import tpu as pltpu

F32 = jnp.float32
BF16 = jnp.bfloat16

D_MODEL = 1024
HEAD_DIM = 64
SB_HEADS = 8
DIFF_HEADS = 4
SWA_Q_HEADS = 8
SWA_KV_HEADS = 2
SWA_GROUP = SWA_Q_HEADS // SWA_KV_HEADS
WINDOW = 128
N_BUCKETS = 32
MAX_DISTANCE = 128
EPS = 1e-6
SCALE = HEAD_DIM ** -0.5
LOG2E = math.log2(math.e)
SB_DEAD_LOG2 = -160.0
MXU_LEAD_MIXED = 2
MXU_LEAD_LOOP = 4

SB_W = SB_HEADS * HEAD_DIM
DIFF_W = DIFF_HEADS * 2 * HEAD_DIM
SWA_Q_W = SWA_Q_HEADS * HEAD_DIM
SWA_KV_W = SWA_KV_HEADS * HEAD_DIM
ATT_OUT_W = SB_W + DIFF_W + SWA_Q_W

V7X_LANES = 128
V7X_MXU_DIM = 256
V7X_VMEM_BYTES = 64 * 1024 * 1024

ATT_TILE = V7X_MXU_DIM
ROW_TILE = 512
CAST_ROWS = 256
PAIR_W = 2 * HEAD_DIM
SUM_ROWS = 16
assert PAIR_W == V7X_LANES and 2 * WINDOW == ATT_TILE

P1_W = 2 * SB_W + 3 * D_MODEL
P2_W = 2 * DIFF_W + SWA_Q_W + 2 * SWA_KV_W
P3_W = SB_W + DIFF_W + SWA_KV_W
P1_CHUNK = 512
P2_CHUNK = 256
P3_CHUNKS = ((0, 256), (256, 256), (512, 256), (768, 256), (1024, 128))
assert sum(w for _, w in P3_CHUNKS) == P3_W
MERGE_FFN_CHUNKS = ((0, 6 * V7X_MXU_DIM), (6 * V7X_MXU_DIM, 5 * V7X_MXU_DIM))

_NT = (((1,), (1,)), ((), ()))


VMEM_MIB = {"cast": 24, "ffn": 56, "mixer_proj": 48, "attention": 58, "merge_ffn": 58}


def _compiler_params(semantics, call):
    vmem_bytes = VMEM_MIB[call] * 1024 * 1024
    assert vmem_bytes < V7X_VMEM_BYTES
    return pltpu.CompilerParams(dimension_semantics=semantics, vmem_limit_bytes=vmem_bytes)


def _rmsnorm_rows(x, g):
    ms = jnp.mean(x * x, axis=-1, keepdims=True)
    return x * lax.rsqrt(ms + EPS) * g


def _cast_kernel(w_ref, o_ref):
    o_ref[...] = w_ref[...].astype(BF16)


def _cast_bf16(w):
    depth, rows, cols = w.shape
    assert rows % CAST_ROWS == 0
    spec = pl.BlockSpec((None, CAST_ROWS, cols), lambda l, r: (l, r, 0))
    return pl.pallas_call(
        _cast_kernel,
        name="cast_bf16",
        grid=(depth, rows // CAST_ROWS),
        in_specs=[spec],
        out_specs=spec,
        out_shape=jax.ShapeDtypeStruct(w.shape, BF16),
        compiler_params=_compiler_params(("parallel", "parallel"), "cast"),
    )(w)


def _regroup_kernel(w_ref, w1_ref, w2_ref, w3_ref):
    def cols(start, width):
        return w_ref[:, start:start + width]

    offs = {}
    acc = 0
    for name, width in (("qa", SB_W), ("ka", SB_W), ("va", SB_W), ("qd", DIFF_W), ("kd", DIFF_W),
                        ("vd", DIFF_W), ("qs", SWA_Q_W), ("ks", SWA_KV_W), ("vs", SWA_KV_W),
                        ("gates", 3 * D_MODEL)):
        offs[name] = (acc, width)
        acc += width
    w1_ref[:, 0:SB_W] = (cols(*offs["qa"]) * (SCALE * LOG2E)).astype(BF16)
    w1_ref[:, SB_W:2 * SB_W] = cols(*offs["ka"]).astype(BF16)
    w1_ref[:, 2 * SB_W:P1_W] = cols(*offs["gates"]).astype(BF16)
    w2_ref[:, 0:DIFF_W] = cols(*offs["qd"]).astype(BF16)
    w2_ref[:, DIFF_W:2 * DIFF_W] = cols(*offs["kd"]).astype(BF16)
    w2_ref[:, 2 * DIFF_W:2 * DIFF_W + SWA_Q_W] = cols(*offs["qs"]).astype(BF16)
    ks = cols(*offs["ks"])
    swapped = pltpu.roll(ks, HEAD_DIM, axis=1)
    lane = lax.broadcasted_iota(jnp.int32, ks.shape, 1)
    base = 2 * DIFF_W + SWA_Q_W
    w2_ref[:, base:base + PAIR_W] = jnp.where(lane < HEAD_DIM, ks, swapped).astype(BF16)
    w2_ref[:, base + PAIR_W:P2_W] = jnp.where(lane < HEAD_DIM, swapped, ks).astype(BF16)
    w3_ref[:, 0:SB_W] = cols(*offs["va"]).astype(BF16)
    w3_ref[:, SB_W:SB_W + DIFF_W] = cols(*offs["vd"]).astype(BF16)
    w3_ref[:, SB_W + DIFF_W:P3_W] = cols(*offs["vs"]).astype(BF16)


def _regroup_projection(w_in):
    depth, d, width = w_in.shape
    assert SWA_KV_W == PAIR_W and d % CAST_ROWS == 0
    block = lambda cols: pl.BlockSpec((None, CAST_ROWS, cols), lambda l, r: (l, r, 0))
    return pl.pallas_call(
        _regroup_kernel,
        name="regroup_projection",
        grid=(depth, d // CAST_ROWS),
        in_specs=[block(width)],
        out_specs=[block(P1_W), block(P2_W), block(P3_W)],
        out_shape=[jax.ShapeDtypeStruct((depth, d, w), BF16) for w in (P1_W, P2_W, P3_W)],
        compiler_params=_compiler_params(("parallel", "parallel"), "cast"),
    )(w_in)


def _ffn_kernel(x_ref, g_ref, wg_ref, wu_ref, wo_ref, o_ref):
    x = x_ref[...]
    h = _rmsnorm_rows(x, g_ref[...]).astype(BF16)
    g = jnp.dot(h, wg_ref[...], preferred_element_type=F32)
    u = jnp.dot(h, wu_ref[...], preferred_element_type=F32)
    a = (g * (1.0 / (1.0 + jnp.exp(-g))) * u).astype(BF16)
    o_ref[...] = x + 0.5 * jnp.dot(a, wo_ref[...], preferred_element_type=F32)


def _ffn(xt, gain, w_in, w_out, layer):
    t, d = xt.shape
    d_ff = w_out.shape[1]
    assert d_ff % V7X_MXU_DIM == 0 and t % ROW_TILE == 0
    resident = pl.Buffered(1)
    return pl.pallas_call(
        _ffn_kernel,
        name="ffn",
        grid=(t // ROW_TILE,),
        in_specs=[
            pl.BlockSpec((ROW_TILE, d), lambda i: (i, 0)),
            pl.BlockSpec((1, d), lambda i: (0, 0)),
            pl.BlockSpec((None, d, d_ff), lambda i: (layer, 0, 0), pipeline_mode=resident),
            pl.BlockSpec((None, d, d_ff), lambda i: (layer, 0, 1), pipeline_mode=resident),
            pl.BlockSpec((None, d_ff, d), lambda i: (layer, 0, 0), pipeline_mode=resident),
        ],
        out_specs=pl.BlockSpec((ROW_TILE, d), lambda i: (i, 0)),
        out_shape=jax.ShapeDtypeStruct((t, d), F32),
        compiler_params=_compiler_params(("parallel",), "ffn"),
    )(xt, gain.reshape(1, d), w_in, w_in, w_out)


def _proj_kernel(x_ref, ng_ref, w1_ref, w2_ref, g2_ref, gm_ref, w3_ref,
                 p1_ref, p2_ref, p3_ref):
    h = _rmsnorm_rows(x_ref[...], ng_ref[...]).astype(BF16)
    for c in range(P1_W // P1_CHUNK):
        cols = slice(c * P1_CHUNK, (c + 1) * P1_CHUNK)
        p1_ref[:, cols] = jnp.dot(h, w1_ref[:, cols], preferred_element_type=F32).astype(BF16)
    y_all = jnp.dot(h, w2_ref[...], preferred_element_type=F32)
    for c in range(P2_W // P2_CHUNK):
        cols = slice(c * P2_CHUNK, (c + 1) * P2_CHUNK)
        y = y_all[:, cols]
        ms = jnp.dot((y * y).astype(BF16), gm_ref[...], preferred_element_type=F32)
        p2_ref[:, cols] = (y * lax.rsqrt(ms + EPS) * g2_ref[:, cols]).astype(BF16)
    for start, width in P3_CHUNKS:
        cols = slice(start, start + width)
        yt = jnp.dot(h, w3_ref[:, cols], preferred_element_type=F32).T
        for r in range(ROW_TILE // ATT_TILE):
            p3_ref[r, cols, :] = yt[:, r * ATT_TILE:(r + 1) * ATT_TILE].astype(BF16)


def _projection(xt, norm_gain, w1, w2, g2, gmat, w3, layer):
    t, d = xt.shape
    const = lambda i: (0, 0)
    weights = lambda cols: pl.BlockSpec((None, d, cols), lambda i: (layer, 0, 0),
                                        pipeline_mode=pl.Buffered(1))
    return pl.pallas_call(
        _proj_kernel,
        name="mixer_proj",
        grid=(t // ROW_TILE,),
        in_specs=[
            pl.BlockSpec((ROW_TILE, d), lambda i: (i, 0)),
            pl.BlockSpec((1, d), const),
            weights(P1_W),
            weights(P2_W),
            pl.BlockSpec((1, P2_W), const),
            pl.BlockSpec((P2_CHUNK, P2_CHUNK), const),
            weights(P3_W),
        ],
        out_specs=[
            pl.BlockSpec((ROW_TILE, P1_W), lambda i: (i, 0)),
            pl.BlockSpec((ROW_TILE, P2_W), lambda i: (i, 0)),
            pl.BlockSpec((ROW_TILE // ATT_TILE, P3_W, ATT_TILE), lambda i: (i, 0, 0)),
        ],
        out_shape=[
            jax.ShapeDtypeStruct((t, P1_W), BF16),
            jax.ShapeDtypeStruct((t, P2_W), BF16),
            jax.ShapeDtypeStruct((t // ATT_TILE, P3_W, ATT_TILE), BF16),
        ],
        compiler_params=_compiler_params(("parallel",), "mixer_proj"),
    )(xt, norm_gain.reshape(1, d), w1, w2, g2, gmat, w3)


def _t5_bucket(dist):
    n = jnp.maximum(dist, 0)
    max_exact = N_BUCKETS // 2
    nf = jnp.maximum(n, 1).astype(F32)
    large = max_exact + (jnp.log(nf / max_exact) / math.log(MAX_DISTANCE / max_exact)
                         * (N_BUCKETS - max_exact)).astype(jnp.int32)
    large = jnp.minimum(large, N_BUCKETS - 1)
    return jnp.where(n < max_exact, n, large)


def _bias_lookup(bucket, tab_ref, head):
    val = jnp.zeros(bucket.shape, F32)
    for b in range(N_BUCKETS):
        val = jnp.where(bucket == b, tab_ref[b, head], val)
    return val


def _bias_kernel(tab_ref, dbias_ref, sbias_ref):
    kk = lax.broadcasted_iota(jnp.int32, (ATT_TILE, ATT_TILE), 0)
    qq = lax.broadcasted_iota(jnp.int32, (ATT_TILE, ATT_TILE), 1)
    for v in range(2):
        dist = qq - kk + v * ATT_TILE
        bucket = _t5_bucket(dist)
        for h in range(DIFF_HEADS):
            val = (_bias_lookup(bucket, tab_ref, h) - tab_ref[N_BUCKETS - 1, h]) * LOG2E
            dbias_ref[h, v] = jnp.where(dist >= 0, val, -jnp.inf)
    kk = lax.broadcasted_iota(jnp.int32, (2 * WINDOW, WINDOW), 0)
    qq = lax.broadcasted_iota(jnp.int32, (2 * WINDOW, WINDOW), 1)
    for v in range(2):
        dist = qq - kk + (1 - v) * WINDOW
        bucket = _t5_bucket(dist)
        for h in range(SWA_Q_HEADS):
            val = _bias_lookup(bucket, tab_ref, DIFF_HEADS + h) * LOG2E
            sbias_ref[h, v] = jnp.where(dist >= 0, jnp.where(dist < WINDOW, val, -jnp.inf), -jnp.inf)


def _bias_tiles(rel_bias):
    return pl.pallas_call(
        _bias_kernel,
        name="bias_tiles",
        in_specs=[pl.BlockSpec(memory_space=pltpu.SMEM)],
        out_specs=[pl.BlockSpec(memory_space=pltpu.VMEM), pl.BlockSpec(memory_space=pltpu.VMEM)],
        out_shape=[
            jax.ShapeDtypeStruct((DIFF_HEADS, 2, ATT_TILE, ATT_TILE), F32),
            jax.ShapeDtypeStruct((SWA_Q_HEADS, 2, 2 * WINDOW, WINDOW), F32),
        ],
    )(rel_bias)


def _split_pairs(q, n_pairs):
    lane = lax.broadcasted_iota(jnp.int32, (1, PAIR_W), 1)
    low = jnp.where(lane < HEAD_DIM, 1.0, 0.0).astype(BF16)
    high = jnp.where(lane >= HEAD_DIM, 1.0, 0.0).astype(BF16)
    heads = []
    for p in range(n_pairs):
        pair = q[:, p * PAIR_W:(p + 1) * PAIR_W]
        heads.append(pair * low)
        heads.append(pair * high)
    return heads


def _neg_abs(x):
    bits = lax.bitcast_convert_type(x, jnp.uint32) | jnp.uint32(0x80000000)
    return lax.bitcast_convert_type(bits, F32)


def _lead(rounds):
    for _ in range(rounds):
        yield


def _interleave(chains):
    live = []
    started = 0
    while started < len(chains) or live:
        if started < len(chains):
            live.append(started)
            started += 1
        for n in reversed(list(live)):
            try:
                next(chains[n])
            except StopIteration:
                live.remove(n)


def _round_robin(groups):
    merged = []
    for n in range(max(len(g) for g in groups)):
        merged.extend(g[n] for g in groups if n < len(g))
    return merged


def _key_tile(ref, j):
    return ref[0, pl.ds(pl.multiple_of(j * ATT_TILE, ATT_TILE), ATT_TILE), :]


def _attn_kernel(sink_ref, qa_ref, ka_ref, qd_ref, kd_ref, qs_ref, ks_ref,
                 vta_ref, vtd_ref, vts_ref, mtri_ref, dbias_ref, sbias_ref, lam_ref, gain_ref,
                 o_ref, acc_a, z_ref, btw_ref, acc_d, sd_ref, pvd_ref, ss_ref, pvs_ref,
                 *, lam_init):
    i = pl.program_id(1)
    first = i == 0
    prev = jnp.maximum(i - 1, 0)
    ones_rows = jnp.ones((SUM_ROWS, ATT_TILE), BF16)

    qa_heads = _split_pairs(qa_ref[0], SB_HEADS // 2)
    mtri = mtri_ref[...]
    kk = lax.broadcasted_iota(jnp.int32, (ATT_TILE, ATT_TILE), 0)
    qq = lax.broadcasted_iota(jnp.int32, (ATT_TILE, ATT_TILE), 1)
    strictly_before = kk < qq

    def sb_chains(tiles, carry, lead):
        def chain(k, vt, keep, fresh, h, slot):
            k_pair = k[:, (h // 2) * PAIR_W:(h // 2 + 1) * PAIR_W]
            rows = slice(h * HEAD_DIM, (h + 1) * HEAD_DIM)
            z_ref[slot] = lax.dot_general(k_pair, qa_heads[h], _NT, preferred_element_type=F32)
            yield from _lead(lead)
            z = z_ref[slot]
            sp = jnp.maximum(z, 0.0) + jnp.log(1.0 + jnp.exp2(_neg_abs(z))) * LOG2E
            if keep is not None:
                sp = jnp.where(keep, sp, 0.0)
            sp_b = sp.astype(BF16)
            yield
            btw_ref[slot] = jnp.dot(mtri, sp_b, preferred_element_type=F32)
            yield from _lead(lead)
            between = btw_ref[slot]
            after = carry[h]
            carry[h] = after + (between[0:1, :] - sp_b[0:1, :].astype(F32))
            w = jnp.exp2((z - sp) + between)
            if keep is not None:
                w = jnp.where(keep, w, 0.0)
            w = w.astype(BF16)
            yield
            pv = jnp.dot(vt[rows], w, preferred_element_type=F32)
            if fresh:
                acc_a[rows, :] = pv
            else:
                acc_a[rows, :] += pv * jnp.exp2(after)

        chains = []
        for t, (j, keep, fresh) in enumerate(tiles):
            k = _key_tile(ka_ref, j)
            vt = vta_ref[j]
            chains.extend(chain(k, vt, keep, fresh, h, SB_HEADS * t + h) for h in range(SB_HEADS))
        return chains

    qd_comps = _split_pairs(qd_ref[0], DIFF_HEADS)

    def diff_chains(tiles, m_run, l_run, lead):
        def chain(k, vt, variant, fresh, n, slot):
            h = n // 2
            sd_ref[slot] = lax.dot_general(k[:, h * PAIR_W:(h + 1) * PAIR_W], qd_comps[n], _NT,
                                           preferred_element_type=F32)
            yield from _lead(lead)
            s = sd_ref[slot]
            if variant is not None:
                s = s + dbias_ref[h, variant]
            m_prev = m_run[n]
            m_new = jnp.maximum(m_prev, jnp.max(s, axis=0, keepdims=True))
            alpha = jnp.exp2(m_prev - m_new)
            m_run[n] = m_new
            yield
            p = jnp.exp2(s - m_new).astype(BF16)
            yield
            pvd_ref[slot] = jnp.dot(jnp.concatenate([vt[h * PAIR_W:(h + 1) * PAIR_W], ones_rows], axis=0),
                                    p, preferred_element_type=F32)
            yield from _lead(lead)
            pv = pvd_ref[slot]
            if fresh:
                l_run[n] = pv[PAIR_W:PAIR_W + 1]
                acc_d[n] = pv[:PAIR_W]
            else:
                l_run[n] = alpha * l_run[n] + pv[PAIR_W:PAIR_W + 1]
                acc_d[n] = alpha * acc_d[n] + pv[:PAIR_W]

        chains = []
        for t, (j, variant, fresh) in enumerate(tiles):
            k = _key_tile(kd_ref, j)
            vt = vtd_ref[j]
            chains.extend(chain(k, vt, variant, fresh, n, 2 * DIFF_HEADS * t + n)
                          for n in range(2 * DIFF_HEADS))
        return chains

    def diff_wide_chains(j_low, m_run, l_run, lead):
        k = kd_ref[0, pl.ds(pl.multiple_of(j_low * ATT_TILE, ATT_TILE), 2 * ATT_TILE), :]
        vt = jnp.concatenate([vtd_ref[j_low], vtd_ref[j_low + 1]], axis=1)
        ones_wide = jnp.ones((SUM_ROWS, 2 * ATT_TILE), BF16)

        def chain(n):
            h = n // 2
            s = lax.dot_general(k[:, h * PAIR_W:(h + 1) * PAIR_W], qd_comps[n], _NT,
                                preferred_element_type=F32)
            sd_ref[2 * n:2 * n + 2] = s.reshape(2, ATT_TILE, ATT_TILE)
            yield from _lead(lead)
            s = sd_ref[2 * n:2 * n + 2].reshape(2 * ATT_TILE, ATT_TILE)
            m_prev = m_run[n]
            m_new = jnp.maximum(m_prev, jnp.max(s, axis=0, keepdims=True))
            alpha = jnp.exp2(m_prev - m_new)
            m_run[n] = m_new
            yield
            p = jnp.exp2(s - m_new).astype(BF16)
            yield
            pvd_ref[n] = jnp.dot(jnp.concatenate([vt[h * PAIR_W:(h + 1) * PAIR_W], ones_wide], axis=0),
                                 p, preferred_element_type=F32)
            yield from _lead(lead)
            pv = pvd_ref[n]
            l_run[n] = alpha * l_run[n] + pv[PAIR_W:PAIR_W + 1]
            acc_d[n] = alpha * acc_d[n] + pv[:PAIR_W]

        return [chain(n) for n in range(2 * DIFF_HEADS)]

    def swa_chains(outs, lead):
        qs = qs_ref[0]
        k_hi = _key_tile(ks_ref, i)
        vt_hi = vts_ref[i]
        lo_start = pl.multiple_of(jnp.maximum(i * ATT_TILE - WINDOW, 0), WINDOW)
        k_lo = ks_ref[0, pl.ds(lo_start, 2 * WINDOW), :]
        vt_lo = jnp.where(first, vt_hi,
                          jnp.concatenate([vts_ref[prev][:, WINDOW:], vt_hi[:, :WINDOW]], axis=1))
        lo_variant = jnp.where(first, 1, 0)
        halves = ((_split_pairs(qs[:WINDOW], SWA_Q_HEADS // 2), k_lo, vt_lo, lo_variant),
                  (_split_pairs(qs[WINDOW:], SWA_Q_HEADS // 2), k_hi, vt_hi, 0))

        def chain(q_heads, k_win, vt_win, variant, h, slot):
            kv = h // SWA_GROUP
            k_pair = k_win[:, kv * PAIR_W:(kv + 1) * PAIR_W]
            ss_ref[slot] = lax.dot_general(k_pair, q_heads[h], _NT, preferred_element_type=F32)
            yield from _lead(lead)
            sink = sink_ref[h] * LOG2E
            s = ss_ref[slot] + sbias_ref[h, variant]
            m = jnp.maximum(jnp.max(s, axis=0, keepdims=True), sink)
            e = jnp.exp2(s - m).astype(BF16)
            yield
            pvs_ref[slot] = jnp.dot(
                jnp.concatenate([vt_win[kv * HEAD_DIM:(kv + 1) * HEAD_DIM], ones_rows], axis=0),
                e, preferred_element_type=F32)
            yield from _lead(lead)
            pv = pvs_ref[slot]
            denom = pv[HEAD_DIM:HEAD_DIM + 1] + jnp.exp2(sink - m)
            outs[slot] = pv[:HEAD_DIM] / denom

        return [chain(*half, h, a * SWA_Q_HEADS + h)
                for a, half in enumerate(halves) for h in range(SWA_Q_HEADS)]

    def head_tiles(sb_tiles, diff_tiles):
        carry = [jnp.zeros((1, ATT_TILE), F32)] * SB_HEADS
        m_run = [jnp.full((1, ATT_TILE), -jnp.inf, F32)] * (2 * DIFF_HEADS)
        l_run = [jnp.zeros((1, ATT_TILE), F32)] * (2 * DIFF_HEADS)
        swa = [None] * (2 * SWA_Q_HEADS)
        _interleave(_round_robin([sb_chains(sb_tiles, carry, MXU_LEAD_MIXED),
                                  diff_chains(diff_tiles, m_run, l_run, MXU_LEAD_MIXED),
                                  swa_chains(swa, MXU_LEAD_MIXED)]))
        swa_t = jnp.concatenate([jnp.concatenate(swa[:SWA_Q_HEADS], axis=0),
                                 jnp.concatenate(swa[SWA_Q_HEADS:], axis=0)], axis=1)
        o_ref[0, :, SB_W + DIFF_W:ATT_OUT_W] = swa_t.T.astype(BF16)
        return tuple(carry), tuple(zip(m_run, l_run))

    sb_pair = [(i, strictly_before, True), (i - 1, None, False)]
    diff_pair = [(i, 0, True), (i - 1, 1, False)]
    head_diff_tiles = jnp.where(i == 0, 1, 3 - (i & 1))
    carries, state = lax.switch(
        head_diff_tiles - 1,
        [lambda: head_tiles(sb_pair[:1], diff_pair[:1]),
         lambda: head_tiles(sb_pair, diff_pair),
         lambda: head_tiles(sb_pair, diff_pair + [(i - 2, None, False)])])

    def far_pair(t, st):
        m_run = [s[0] for s in st]
        l_run = [s[1] for s in st]
        j_low = i - head_diff_tiles - 1 - 2 * t
        _interleave(diff_wide_chains(j_low, m_run, l_run, MXU_LEAD_LOOP))
        return tuple(zip(m_run, l_run))

    n_far = i + 1 - head_diff_tiles
    state = lax.fori_loop(0, lax.shift_right_logical(n_far, 1), far_pair, state)

    def live(c):
        j, carry = c
        return jnp.logical_and(j >= 0, jnp.max(jnp.concatenate(carry, axis=0)) > SB_DEAD_LOG2)

    def earlier_tile(c):
        j, carry = c
        carry = list(carry)
        _interleave(sb_chains([(j, None, False)], carry, MXU_LEAD_LOOP))
        return j - 1, tuple(carry)

    lax.while_loop(live, earlier_tile, (i - 2, carries))

    o_ref[0, :, 0:SB_W] = acc_a[...].T.astype(BF16)

    lv = lam_ref[...]
    lam = (jnp.exp(jnp.sum(lv[0:1] * lv[1:2], axis=-1, keepdims=True))
           - jnp.exp(jnp.sum(lv[2:3] * lv[3:4], axis=-1, keepdims=True)) + lam_init)
    outs = []
    for h in range(DIFF_HEADS):
        (_, l1), (_, l2) = state[2 * h], state[2 * h + 1]
        o = acc_d[2 * h] / l1 - lam * (acc_d[2 * h + 1] / l2)
        ms = jnp.mean(o * o, axis=0, keepdims=True)
        outs.append(o * lax.rsqrt(ms + EPS) * gain_ref[...] * (1.0 - lam_init))
    o_ref[0, :, SB_W:SB_W + DIFF_W] = jnp.concatenate(outs, axis=0).T.astype(BF16)


def _attention(sinks, p1, p2, p3, mtri, dbias, sbias, lam_vecs, sub_gain, lam_init, batch, seq):
    nq = seq // ATT_TILE
    p1 = p1.reshape(batch, seq, P1_W)
    p2 = p2.reshape(batch, seq, P2_W)
    kdup_w = 2 * SWA_KV_W
    q_tile = lambda col: pl.BlockSpec((1, ATT_TILE, SB_W), lambda b, i: (b, i, col))
    keys = lambda col: pl.BlockSpec((1, seq, SB_W), lambda b, i: (b, 0, col))
    values_t = lambda rows, blk: pl.BlockSpec((nq, rows, ATT_TILE), lambda b, i: (b, blk, 0))
    whole = lambda shape: pl.BlockSpec(shape, lambda b, i: (0,) * len(shape),
                                       pipeline_mode=pl.Buffered(1))
    assert SB_W == DIFF_W == SWA_Q_W
    return pl.pallas_call(
        functools.partial(_attn_kernel, lam_init=lam_init),
        name="attention",
        grid=(batch, nq),
        in_specs=[
            pl.BlockSpec(memory_space=pltpu.SMEM),
            q_tile(0), keys(1),
            q_tile(0), keys(1),
            q_tile(2),
            pl.BlockSpec((1, seq, kdup_w), lambda b, i: (b, 0, (2 * DIFF_W + SWA_Q_W) // kdup_w)),
            values_t(SB_W, 0), values_t(DIFF_W, 1),
            values_t(SWA_KV_W, (SB_W + DIFF_W) // SWA_KV_W),
            whole((ATT_TILE, ATT_TILE)),
            whole((DIFF_HEADS, 2, ATT_TILE, ATT_TILE)),
            whole((SWA_Q_HEADS, 2, 2 * WINDOW, WINDOW)),
            whole((4, HEAD_DIM)),
            whole((PAIR_W, 1)),
        ],
        out_specs=pl.BlockSpec((1, ATT_TILE, ATT_OUT_W), lambda b, i: (b, i, 0)),
        out_shape=jax.ShapeDtypeStruct((batch, seq, ATT_OUT_W), BF16),
        scratch_shapes=[
            pltpu.VMEM((SB_W, ATT_TILE), F32),
            pltpu.VMEM((2 * SB_HEADS, ATT_TILE, ATT_TILE), F32),
            pltpu.VMEM((2 * SB_HEADS, ATT_TILE, ATT_TILE), F32),
            pltpu.VMEM((2 * DIFF_HEADS, PAIR_W, ATT_TILE), F32),
            pltpu.VMEM((6 * DIFF_HEADS, ATT_TILE, ATT_TILE), F32),
            pltpu.VMEM((6 * DIFF_HEADS, PAIR_W + SUM_ROWS, ATT_TILE), F32),
            pltpu.VMEM((2 * SWA_Q_HEADS, 2 * WINDOW, WINDOW), F32),
            pltpu.VMEM((2 * SWA_Q_HEADS, HEAD_DIM + SUM_ROWS, WINDOW), F32),
        ],
        compiler_params=_compiler_params(("parallel", "arbitrary"), "attention"),
    )(sinks, p1, p1, p2, p2, p2, p2, p3, p3, p3, mtri, dbias, sbias, lam_vecs,
      sub_gain.reshape(PAIR_W, 1))


def _merge_ffn_kernel(x_ref, osb_ref, odf_ref, osw_ref, ga_ref, gd_ref, gs_ref,
                      wsb_ref, wdf_ref, wsw_ref, wo_ref, ng_ref, wg_ref, wu_ref, wd_ref, o_ref):
    def branch(o_ref_, w_ref, g_ref):
        gate = 1.0 / (1.0 + jnp.exp(-g_ref[...].astype(F32)))
        return gate * jnp.dot(o_ref_[...], w_ref[...], preferred_element_type=F32)

    merged = (branch(osb_ref, wsb_ref, ga_ref) + branch(odf_ref, wdf_ref, gd_ref)
              + branch(osw_ref, wsw_ref, gs_ref))
    x = x_ref[...] + jnp.dot(merged.astype(BF16), wo_ref[...], preferred_element_type=F32)
    h = _rmsnorm_rows(x, ng_ref[...]).astype(BF16)
    out = x
    for start, width in MERGE_FFN_CHUNKS:
        cols = slice(start, start + width)
        g = jnp.dot(h, wg_ref[:, cols], preferred_element_type=F32)
        u = jnp.dot(h, wu_ref[:, cols], preferred_element_type=F32)
        a = (g * (1.0 / (1.0 + jnp.exp(-g))) * u).astype(BF16)
        out = out + 0.5 * jnp.dot(a, wd_ref[cols, :], preferred_element_type=F32)
    o_ref[...] = out


def _merge_ffn(xt, att, p1, w_sb, w_diff, w_swa, w_out, gain, w_in, w_down, layer):
    t, d = xt.shape
    d_ff = w_down.shape[1]
    assert sum(w for _, w in MERGE_FFN_CHUNKS) == d_ff
    gcol = 2 * SB_W // d
    row = lambda i: (i, 0)
    resident = lambda shape, col=0: pl.BlockSpec((None,) + shape, lambda i: (layer, 0, col),
                                                 pipeline_mode=pl.Buffered(1))
    branch_out = lambda n: pl.BlockSpec((ROW_TILE, SB_W), lambda i: (i, n))
    return pl.pallas_call(
        _merge_ffn_kernel,
        name="merge_ffn",
        grid=(t // ROW_TILE,),
        in_specs=[
            pl.BlockSpec((ROW_TILE, d), row),
            branch_out(0), branch_out(1), branch_out(2),
            pl.BlockSpec((ROW_TILE, d), lambda i: (i, gcol)),
            pl.BlockSpec((ROW_TILE, d), lambda i: (i, gcol + 1)),
            pl.BlockSpec((ROW_TILE, d), lambda i: (i, gcol + 2)),
            resident((SB_W, d)), resident((DIFF_W, d)), resident((SWA_Q_W, d)), resident((d, d)),
            pl.BlockSpec((1, d), lambda i: (0, 0)),
            resident((d, d_ff), 0), resident((d, d_ff), 1), resident((d_ff, d)),
        ],
        out_specs=pl.BlockSpec((ROW_TILE, d), row),
        out_shape=jax.ShapeDtypeStruct((t, d), F32),
        compiler_params=_compiler_params(("parallel",), "merge_ffn"),
    )(xt, att, att, att, p1, p1, p1, w_sb, w_diff, w_swa, w_out,
      gain.reshape(1, d), w_in, w_in, w_down)


def _qk_norm_gains(q_norm_diff, k_norm_diff, q_norm_swa, k_norm_swa):
    qscale = SCALE * LOG2E
    return jnp.concatenate([
        jnp.tile(q_norm_diff * qscale, DIFF_W // HEAD_DIM), jnp.tile(k_norm_diff, DIFF_W // HEAD_DIM),
        jnp.tile(q_norm_swa * qscale, SWA_Q_W // HEAD_DIM), jnp.tile(k_norm_swa, 2 * SWA_KV_W // HEAD_DIM),
    ]).reshape(1, P2_W).astype(F32)


def _group_mean_matrix():
    g = jnp.arange(P2_CHUNK) // HEAD_DIM
    return jnp.where(g[:, None] == g[None, :], 1.0 / HEAD_DIM, 0.0).astype(BF16)


def _neg_suffix_matrix():
    idx = jnp.arange(ATT_TILE)
    return jnp.where(idx[None, :] > idx[:, None], -1.0, 0.0).astype(BF16)


def kernel(x, ffn1_norm, ffn1_w_in, ffn1_w_out, mix_norm, w_in, q_norm_diff, k_norm_diff, q_norm_swa, k_norm_swa, diff_lambda, diff_subln, swa_sinks, rel_bias, w_proj_sb, w_proj_diff, w_proj_swa, w_out, ffn2_norm, ffn2_w_in, ffn2_w_out):
    batch, seq, d = x.shape
    depth = ffn1_norm.shape[0]
    assert d == D_MODEL and seq % ATT_TILE == 0 and (batch * seq) % ROW_TILE == 0
    xt = x.reshape(batch * seq, d)
    dbias, sbias = _bias_tiles(rel_bias)
    gmat = _group_mean_matrix()
    mtri = _neg_suffix_matrix()
    ffn1_in, ffn1_out = _cast_bf16(ffn1_w_in), _cast_bf16(ffn1_w_out)
    ffn2_in, ffn2_out = _cast_bf16(ffn2_w_in), _cast_bf16(ffn2_w_out)
    w1, w2, w3 = _regroup_projection(w_in)
    w_sb, w_diff, w_swa, w_o = (w.astype(BF16) for w in (w_proj_sb, w_proj_diff, w_proj_swa, w_out))
    for l in range(depth):
        lam_init = 0.8 - 0.6 * math.exp(-0.3 * l)
        xt = _ffn(xt, ffn1_norm[l], ffn1_in, ffn1_out, l)
        g2 = _qk_norm_gains(q_norm_diff[l], k_norm_diff[l], q_norm_swa[l], k_norm_swa[l])
        p1, p2, p3 = _projection(xt, mix_norm[l], w1, w2, g2, gmat, w3, l)
        att = _attention(swa_sinks[l], p1, p2, p3, mtri, dbias, sbias, diff_lambda[l],
                         diff_subln[l], lam_init, batch, seq)
        xt = _merge_ffn(xt, att.reshape(batch * seq, ATT_OUT_W), p1, w_sb, w_diff, w_swa, w_o,
                        ffn2_norm[l], ffn2_in, ffn2_out, l)
    return xt.reshape(batch, seq, d)
```

```python
import functools
import math

import jax
import jax.numpy as jnp
from jax import lax
from jax.experimental import pallas as pl
from jax.experimental.pallas import tpu as pltpu

F32 = jnp.float32
BF16 = jnp.bfloat16

D_MODEL = 1024
HEAD_DIM = 64
SB_HEADS = 8
DIFF_HEADS = 4
SWA_Q_HEADS = 8
SWA_KV_HEADS = 2
SWA_GROUP = SWA_Q_HEADS // SWA_KV_HEADS
WINDOW = 128
N_BUCKETS = 32
MAX_DISTANCE = 128
EPS = 1e-6
SCALE = HEAD_DIM ** -0.5
LOG2E = math.log2(math.e)
SB_DEAD_LOG2 = -160.0
MXU_LEAD_MIXED = 2
MXU_LEAD_LOOP = 4

SB_W = SB_HEADS * HEAD_DIM
DIFF_W = DIFF_HEADS * 2 * HEAD_DIM
SWA_Q_W = SWA_Q_HEADS * HEAD_DIM
SWA_KV_W = SWA_KV_HEADS * HEAD_DIM
ATT_OUT_W = SB_W + DIFF_W + SWA_Q_W

V7X_LANES = 128
V7X_MXU_DIM = 256
V7X_VMEM_BYTES = 64 * 1024 * 1024

ATT_TILE = V7X_MXU_DIM
ROW_TILE = 512
CAST_ROWS = 256
PAIR_W = 2 * HEAD_DIM
SUM_ROWS = 16
assert PAIR_W == V7X_LANES and 2 * WINDOW == ATT_TILE

P1_W = 2 * SB_W + 3 * D_MODEL
P2_W = 2 * DIFF_W + SWA_Q_W + 2 * SWA_KV_W
P3_W = SB_W + DIFF_W + SWA_KV_W
P1_CHUNK = 512
P2_CHUNK = 256
P3_CHUNKS = ((0, 256), (256, 256), (512, 256), (768, 256), (1024, 128))
assert sum(w for _, w in P3_CHUNKS) == P3_W
MERGE_FFN_CHUNKS = ((0, 6 * V7X_MXU_DIM), (6 * V7X_MXU_DIM, 5 * V7X_MXU_DIM))
FFN_ROW_TILE = 1024
FFN_CHUNKS = ((0, 768), (768, 768), (1536, 768), (2304, 512))

_NT = (((1,), (1,)), ((), ()))


VMEM_MIB = {"cast": 24, "ffn": 56, "mixer_proj": 48, "attention": 58, "merge_ffn": 58}


def _compiler_params(semantics, call):
    vmem_bytes = VMEM_MIB[call] * 1024 * 1024
    assert vmem_bytes < V7X_VMEM_BYTES
    return pltpu.CompilerParams(dimension_semantics=semantics, vmem_limit_bytes=vmem_bytes)


def _rmsnorm_rows(x, g):
    ms = jnp.mean(x * x, axis=-1, keepdims=True)
    return x * lax.rsqrt(ms + EPS) * g


def _cast_kernel(w_ref, o_ref):
    o_ref[...] = w_ref[...].astype(BF16)


def _cast_bf16(w):
    depth, rows, cols = w.shape
    assert rows % CAST_ROWS == 0
    spec = pl.BlockSpec((None, CAST_ROWS, cols), lambda l, r: (l, r, 0))
    return pl.pallas_call(
        _cast_kernel,
        name="cast_bf16",
        grid=(depth, rows // CAST_ROWS),
        in_specs=[spec],
        out_specs=spec,
        out_shape=jax.ShapeDtypeStruct(w.shape, BF16),
        compiler_params=_compiler_params(("parallel", "parallel"), "cast"),
    )(w)


def _regroup_kernel(w_ref, w1_ref, w2_ref, w3_ref):
    def cols(start, width):
        return w_ref[:, start:start + width]

    offs = {}
    acc = 0
    for name, width in (("qa", SB_W), ("ka", SB_W), ("va", SB_W), ("qd", DIFF_W), ("kd", DIFF_W),
                        ("vd", DIFF_W), ("qs", SWA_Q_W), ("ks", SWA_KV_W), ("vs", SWA_KV_W),
                        ("gates", 3 * D_MODEL)):
        offs[name] = (acc, width)
        acc += width
    w1_ref[:, 0:SB_W] = (cols(*offs["qa"]) * (SCALE * LOG2E)).astype(BF16)
    w1_ref[:, SB_W:2 * SB_W] = cols(*offs["ka"]).astype(BF16)
    w1_ref[:, 2 * SB_W:P1_W] = cols(*offs["gates"]).astype(BF16)
    w2_ref[:, 0:DIFF_W] = cols(*offs["qd"]).astype(BF16)
    w2_ref[:, DIFF_W:2 * DIFF_W] = cols(*offs["kd"]).astype(BF16)
    w2_ref[:, 2 * DIFF_W:2 * DIFF_W + SWA_Q_W] = cols(*offs["qs"]).astype(BF16)
    ks = cols(*offs["ks"])
    swapped = pltpu.roll(ks, HEAD_DIM, axis=1)
    lane = lax.broadcasted_iota(jnp.int32, ks.shape, 1)
    base = 2 * DIFF_W + SWA_Q_W
    w2_ref[:, base:base + PAIR_W] = jnp.where(lane < HEAD_DIM, ks, swapped).astype(BF16)
    w2_ref[:, base + PAIR_W:P2_W] = jnp.where(lane < HEAD_DIM, swapped, ks).astype(BF16)
    w3_ref[:, 0:SB_W] = cols(*offs["va"]).astype(BF16)
    w3_ref[:, SB_W:SB_W + DIFF_W] = cols(*offs["vd"]).astype(BF16)
    w3_ref[:, SB_W + DIFF_W:P3_W] = cols(*offs["vs"]).astype(BF16)


def _regroup_projection(w_in):
    depth, d, width = w_in.shape
    assert SWA_KV_W == PAIR_W and d % CAST_ROWS == 0
    block = lambda cols: pl.BlockSpec((None, CAST_ROWS, cols), lambda l, r: (l, r, 0))
    return pl.pallas_call(
        _regroup_kernel,
        name="regroup_projection",
        grid=(depth, d // CAST_ROWS),
        in_specs=[block(width)],
        out_specs=[block(P1_W), block(P2_W), block(P3_W)],
        out_shape=[jax.ShapeDtypeStruct((depth, d, w), BF16) for w in (P1_W, P2_W, P3_W)],
        compiler_params=_compiler_params(("parallel", "parallel"), "cast"),
    )(w_in)


def _ffn_kernel(x_ref, g_ref, wg_ref, wu_ref, wo_ref, o_ref):
    x = x_ref[...]
    h = _rmsnorm_rows(x, g_ref[...]).astype(BF16)
    out = x
    for start, width in FFN_CHUNKS:
        cols = slice(start, start + width)
        g = jnp.dot(h, wg_ref[:, cols], preferred_element_type=F32)
        u = jnp.dot(h, wu_ref[:, cols], preferred_element_type=F32)
        a = (g * (1.0 / (1.0 + jnp.exp(-g))) * u).astype(BF16)
        out = out + 0.5 * jnp.dot(a, wo_ref[cols, :], preferred_element_type=F32)
    o_ref[...] = out


def _ffn(xt, gain, w_in, w_out, layer):
    t, d = xt.shape
    d_ff = w_out.shape[1]
    assert d_ff % V7X_MXU_DIM == 0 and t % FFN_ROW_TILE == 0
    resident = pl.Buffered(1)
    return pl.pallas_call(
        _ffn_kernel,
        name="ffn",
        grid=(t // FFN_ROW_TILE,),
        in_specs=[
            pl.BlockSpec((FFN_ROW_TILE, d), lambda i: (i, 0)),
            pl.BlockSpec((1, d), lambda i: (0, 0)),
            pl.BlockSpec((None, d, d_ff), lambda i: (layer, 0, 0), pipeline_mode=resident),
            pl.BlockSpec((None, d, d_ff), lambda i: (layer, 0, 1), pipeline_mode=resident),
            pl.BlockSpec((None, d_ff, d), lambda i: (layer, 0, 0), pipeline_mode=resident),
        ],
        out_specs=pl.BlockSpec((FFN_ROW_TILE, d), lambda i: (i, 0)),
        out_shape=jax.ShapeDtypeStruct((t, d), F32),
        compiler_params=_compiler_params(("parallel",), "ffn"),
    )(xt, gain.reshape(1, d), w_in, w_in, w_out)


def _proj_kernel(x_ref, ng_ref, w1_ref, w2_ref, g2_ref, gm_ref, w3_ref,
                 p1_ref, p2_ref, p3_ref):
    h = _rmsnorm_rows(x_ref[...], ng_ref[...]).astype(BF16)
    for c in range(P1_W // P1_CHUNK):
        cols = slice(c * P1_CHUNK, (c + 1) * P1_CHUNK)
        p1_ref[:, cols] = jnp.dot(h, w1_ref[:, cols], preferred_element_type=F32).astype(BF16)
    y_all = jnp.dot(h, w2_ref[...], preferred_element_type=F32)
    for c in range(P2_W // P2_CHUNK):
        cols = slice(c * P2_CHUNK, (c + 1) * P2_CHUNK)
        y = y_all[:, cols]
        ms = jnp.dot((y * y).astype(BF16), gm_ref[...], preferred_element_type=F32)
        p2_ref[:, cols] = (y * lax.rsqrt(ms + EPS) * g2_ref[:, cols]).astype(BF16)
    for start, width in P3_CHUNKS:
        cols = slice(start, start + width)
        yt = jnp.dot(h, w3_ref[:, cols], preferred_element_type=F32).T
        for r in range(ROW_TILE // ATT_TILE):
            p3_ref[r, cols, :] = yt[:, r * ATT_TILE:(r + 1) * ATT_TILE].astype(BF16)


def _projection(xt, norm_gain, w1, w2, g2, gmat, w3, layer):
    t, d = xt.shape
    const = lambda i: (0, 0)
    weights = lambda cols: pl.BlockSpec((None, d, cols), lambda i: (layer, 0, 0),
                                        pipeline_mode=pl.Buffered(1))
    return pl.pallas_call(
        _proj_kernel,
        name="mixer_proj",
        grid=(t // ROW_TILE,),
        in_specs=[
            pl.BlockSpec((ROW_TILE, d), lambda i: (i, 0)),
            pl.BlockSpec((1, d), const),
            weights(P1_W),
            weights(P2_W),
            pl.BlockSpec((1, P2_W), const),
            pl.BlockSpec((P2_CHUNK, P2_CHUNK), const),
            weights(P3_W),
        ],
        out_specs=[
            pl.BlockSpec((ROW_TILE, P1_W), lambda i: (i, 0)),
            pl.BlockSpec((ROW_TILE, P2_W), lambda i: (i, 0)),
            pl.BlockSpec((ROW_TILE // ATT_TILE, P3_W, ATT_TILE), lambda i: (i, 0, 0)),
        ],
        out_shape=[
            jax.ShapeDtypeStruct((t, P1_W), BF16),
            jax.ShapeDtypeStruct((t, P2_W), BF16),
            jax.ShapeDtypeStruct((t // ATT_TILE, P3_W, ATT_TILE), BF16),
        ],
        compiler_params=_compiler_params(("parallel",), "mixer_proj"),
    )(xt, norm_gain.reshape(1, d), w1, w2, g2, gmat, w3)


def _t5_bucket(dist):
    n = jnp.maximum(dist, 0)
    max_exact = N_BUCKETS // 2
    nf = jnp.maximum(n, 1).astype(F32)
    large = max_exact + (jnp.log(nf / max_exact) / math.log(MAX_DISTANCE / max_exact)
                         * (N_BUCKETS - max_exact)).astype(jnp.int32)
    large = jnp.minimum(large, N_BUCKETS - 1)
    return jnp.where(n < max_exact, n, large)


def _bias_lookup(bucket, tab_ref, head):
    val = jnp.zeros(bucket.shape, F32)
    for b in range(N_BUCKETS):
        val = jnp.where(bucket == b, tab_ref[b, head], val)
    return val


def _bias_kernel(tab_ref, dbias_ref, sbias_ref):
    kk = lax.broadcasted_iota(jnp.int32, (ATT_TILE, ATT_TILE), 0)
    qq = lax.broadcasted_iota(jnp.int32, (ATT_TILE, ATT_TILE), 1)
    for v in range(2):
        dist = qq - kk + v * ATT_TILE
        bucket = _t5_bucket(dist)
        for h in range(DIFF_HEADS):
            val = (_bias_lookup(bucket, tab_ref, h) - tab_ref[N_BUCKETS - 1, h]) * LOG2E
            dbias_ref[h, v] = jnp.where(dist >= 0, val, -jnp.inf)
    kk = lax.broadcasted_iota(jnp.int32, (2 * WINDOW, WINDOW), 0)
    qq = lax.broadcasted_iota(jnp.int32, (2 * WINDOW, WINDOW), 1)
    for v in range(2):
        dist = qq - kk + (1 - v) * WINDOW
        bucket = _t5_bucket(dist)
        for h in range(SWA_Q_HEADS):
            val = _bias_lookup(bucket, tab_ref, DIFF_HEADS + h) * LOG2E
            sbias_ref[h, v] = jnp.where(dist >= 0, jnp.where(dist < WINDOW, val, -jnp.inf), -jnp.inf)


def _bias_tiles(rel_bias):
    return pl.pallas_call(
        _bias_kernel,
        name="bias_tiles",
        in_specs=[pl.BlockSpec(memory_space=pltpu.SMEM)],
        out_specs=[pl.BlockSpec(memory_space=pltpu.VMEM), pl.BlockSpec(memory_space=pltpu.VMEM)],
        out_shape=[
            jax.ShapeDtypeStruct((DIFF_HEADS, 2, ATT_TILE, ATT_TILE), F32),
            jax.ShapeDtypeStruct((SWA_Q_HEADS, 2, 2 * WINDOW, WINDOW), F32),
        ],
    )(rel_bias)


def _split_pairs(q, n_pairs):
    lane = lax.broadcasted_iota(jnp.int32, (1, PAIR_W), 1)
    low = jnp.where(lane < HEAD_DIM, 1.0, 0.0).astype(BF16)
    high = jnp.where(lane >= HEAD_DIM, 1.0, 0.0).astype(BF16)
    heads = []
    for p in range(n_pairs):
        pair = q[:, p * PAIR_W:(p + 1) * PAIR_W]
        heads.append(pair * low)
        heads.append(pair * high)
    return heads


def _neg_abs(x):
    bits = lax.bitcast_convert_type(x, jnp.uint32) | jnp.uint32(0x80000000)
    return lax.bitcast_convert_type(bits, F32)


def _lead(rounds):
    for _ in range(rounds):
        yield


def _interleave(chains):
    live = []
    started = 0
    while started < len(chains) or live:
        if started < len(chains):
            live.append(started)
            started += 1
        for n in reversed(list(live)):
            try:
                next(chains[n])
            except StopIteration:
                live.remove(n)


def _round_robin(groups):
    merged = []
    for n in range(max(len(g) for g in groups)):
        merged.extend(g[n] for g in groups if n < len(g))
    return merged


def _key_tile(ref, j):
    return ref[0, pl.ds(pl.multiple_of(j * ATT_TILE, ATT_TILE), ATT_TILE), :]


def _attn_kernel(sink_ref, qa_ref, ka_ref, qd_ref, kd_ref, qs_ref, ks_ref,
                 vta_ref, vtd_ref, vts_ref, mtri_ref, dbias_ref, sbias_ref, lam_ref, gain_ref,
                 o_ref, acc_a, z_ref, btw_ref, acc_d, sd_ref, pvd_ref, ss_ref, pvs_ref,
                 *, lam_init):
    i = pl.program_id(1)
    first = i == 0
    prev = jnp.maximum(i - 1, 0)
    ones_rows = jnp.ones((SUM_ROWS, ATT_TILE), BF16)

    qa_heads = _split_pairs(qa_ref[0], SB_HEADS // 2)
    mtri = mtri_ref[...]
    kk = lax.broadcasted_iota(jnp.int32, (ATT_TILE, ATT_TILE), 0)
    qq = lax.broadcasted_iota(jnp.int32, (ATT_TILE, ATT_TILE), 1)
    strictly_before = kk < qq

    def sb_chains(tiles, carry, lead):
        def chain(k, vt, keep, fresh, h, slot):
            k_pair = k[:, (h // 2) * PAIR_W:(h // 2 + 1) * PAIR_W]
            rows = slice(h * HEAD_DIM, (h + 1) * HEAD_DIM)
            z_ref[slot] = lax.dot_general(k_pair, qa_heads[h], _NT, preferred_element_type=F32)
            yield from _lead(lead)
            z = z_ref[slot]
            sp = jnp.maximum(z, 0.0) + jnp.log(1.0 + jnp.exp2(_neg_abs(z))) * LOG2E
            if keep is not None:
                sp = jnp.where(keep, sp, 0.0)
            sp_b = sp.astype(BF16)
            yield
            btw_ref[slot] = jnp.dot(mtri, sp_b, preferred_element_type=F32)
            yield from _lead(lead)
            between = btw_ref[slot]
            after = carry[h]
            carry[h] = after + (between[0:1, :] - sp_b[0:1, :].astype(F32))
            w = jnp.exp2((z - sp) + between)
            if keep is not None:
                w = jnp.where(keep, w, 0.0)
            w = w.astype(BF16)
            yield
            pv = jnp.dot(vt[rows], w, preferred_element_type=F32)
            if fresh:
                acc_a[rows, :] = pv
            else:
                acc_a[rows, :] += pv * jnp.exp2(after)

        chains = []
        for t, (j, keep, fresh) in enumerate(tiles):
            k = _key_tile(ka_ref, j)
            vt = vta_ref[j]
            chains.extend(chain(k, vt, keep, fresh, h, SB_HEADS * t + h) for h in range(SB_HEADS))
        return chains

    qd_comps = _split_pairs(qd_ref[0], DIFF_HEADS)

    def diff_chains(tiles, m_run, l_run, lead):
        def chain(k, vt, variant, fresh, n, slot):
            h = n // 2
            sd_ref[slot] = lax.dot_general(k[:, h * PAIR_W:(h + 1) * PAIR_W], qd_comps[n], _NT,
                                           preferred_element_type=F32)
            yield from _lead(lead)
            s = sd_ref[slot]
            if variant is not None:
                s = s + dbias_ref[h, variant]
            m_prev = m_run[n]
            m_new = jnp.maximum(m_prev, jnp.max(s, axis=0, keepdims=True))
            alpha = jnp.exp2(m_prev - m_new)
            m_run[n] = m_new
            yield
            p = jnp.exp2(s - m_new).astype(BF16)
            yield
            pvd_ref[slot] = jnp.dot(jnp.concatenate([vt[h * PAIR_W:(h + 1) * PAIR_W], ones_rows], axis=0),
                                    p, preferred_element_type=F32)
            yield from _lead(lead)
            pv = pvd_ref[slot]
            if fresh:
                l_run[n] = pv[PAIR_W:PAIR_W + 1]
                acc_d[n] = pv[:PAIR_W]
            else:
                l_run[n] = alpha * l_run[n] + pv[PAIR_W:PAIR_W + 1]
                acc_d[n] = alpha * acc_d[n] + pv[:PAIR_W]

        chains = []
        for t, (j, variant, fresh) in enumerate(tiles):
            k = _key_tile(kd_ref, j)
            vt = vtd_ref[j]
            chains.extend(chain(k, vt, variant, fresh, n, 2 * DIFF_HEADS * t + n)
                          for n in range(2 * DIFF_HEADS))
        return chains

    def diff_wide_chains(j_low, m_run, l_run, lead):
        k = kd_ref[0, pl.ds(pl.multiple_of(j_low * ATT_TILE, ATT_TILE), 2 * ATT_TILE), :]
        vt = jnp.concatenate([vtd_ref[j_low], vtd_ref[j_low + 1]], axis=1)
        ones_wide = jnp.ones((SUM_ROWS, 2 * ATT_TILE), BF16)

        def chain(n):
            h = n // 2
            s = lax.dot_general(k[:, h * PAIR_W:(h + 1) * PAIR_W], qd_comps[n], _NT,
                                preferred_element_type=F32)
            sd_ref[2 * n:2 * n + 2] = s.reshape(2, ATT_TILE, ATT_TILE)
            yield from _lead(lead)
            s = sd_ref[2 * n:2 * n + 2].reshape(2 * ATT_TILE, ATT_TILE)
            m_prev = m_run[n]
            m_new = jnp.maximum(m_prev, jnp.max(s, axis=0, keepdims=True))
            alpha = jnp.exp2(m_prev - m_new)
            m_run[n] = m_new
            yield
            p = jnp.exp2(s - m_new).astype(BF16)
            yield
            pvd_ref[n] = jnp.dot(jnp.concatenate([vt[h * PAIR_W:(h + 1) * PAIR_W], ones_wide], axis=0),
                                 p, preferred_element_type=F32)
            yield from _lead(lead)
            pv = pvd_ref[n]
            l_run[n] = alpha * l_run[n] + pv[PAIR_W:PAIR_W + 1]
            acc_d[n] = alpha * acc_d[n] + pv[:PAIR_W]

        return [chain(n) for n in range(2 * DIFF_HEADS)]

    def swa_chains(outs, lead):
        qs = qs_ref[0]
        k_hi = _key_tile(ks_ref, i)
        vt_hi = vts_ref[i]
        lo_start = pl.multiple_of(jnp.maximum(i * ATT_TILE - WINDOW, 0), WINDOW)
        k_lo = ks_ref[0, pl.ds(lo_start, 2 * WINDOW), :]
        vt_lo = jnp.where(first, vt_hi,
                          jnp.concatenate([vts_ref[prev][:, WINDOW:], vt_hi[:, :WINDOW]], axis=1))
        lo_variant = jnp.where(first, 1, 0)
        halves = ((_split_pairs(qs[:WINDOW], SWA_Q_HEADS // 2), k_lo, vt_lo, lo_variant),
                  (_split_pairs(qs[WINDOW:], SWA_Q_HEADS // 2), k_hi, vt_hi, 0))

        def chain(q_heads, k_win, vt_win, variant, h, slot):
            kv = h // SWA_GROUP
            k_pair = k_win[:, kv * PAIR_W:(kv + 1) * PAIR_W]
            ss_ref[slot] = lax.dot_general(k_pair, q_heads[h], _NT, preferred_element_type=F32)
            yield from _lead(lead)
            sink = sink_ref[h] * LOG2E
            s = ss_ref[slot] + sbias_ref[h, variant]
            m = jnp.maximum(jnp.max(s, axis=0, keepdims=True), sink)
            e = jnp.exp2(s - m).astype(BF16)
            yield
            pvs_ref[slot] = jnp.dot(
                jnp.concatenate([vt_win[kv * HEAD_DIM:(kv + 1) * HEAD_DIM], ones_rows], axis=0),
                e, preferred_element_type=F32)
            yield from _lead(lead)
            pv = pvs_ref[slot]
            denom = pv[HEAD_DIM:HEAD_DIM + 1] + jnp.exp2(sink - m)
            outs[slot] = pv[:HEAD_DIM] / denom

        return [chain(*half, h, a * SWA_Q_HEADS + h)
                for a, half in enumerate(halves) for h in range(SWA_Q_HEADS)]

    def head_tiles(sb_tiles, diff_tiles):
        carry = [jnp.zeros((1, ATT_TILE), F32)] * SB_HEADS
        m_run = [jnp.full((1, ATT_TILE), -jnp.inf, F32)] * (2 * DIFF_HEADS)
        l_run = [jnp.zeros((1, ATT_TILE), F32)] * (2 * DIFF_HEADS)
        swa = [None] * (2 * SWA_Q_HEADS)
        _interleave(_round_robin([sb_chains(sb_tiles, carry, MXU_LEAD_MIXED),
                                  diff_chains(diff_tiles, m_run, l_run, MXU_LEAD_MIXED),
                                  swa_chains(swa, MXU_LEAD_MIXED)]))
        swa_t = jnp.concatenate([jnp.concatenate(swa[:SWA_Q_HEADS], axis=0),
                                 jnp.concatenate(swa[SWA_Q_HEADS:], axis=0)], axis=1)
        o_ref[0, :, SB_W + DIFF_W:ATT_OUT_W] = swa_t.T.astype(BF16)
        return tuple(carry), tuple(zip(m_run, l_run))

    sb_pair = [(i, strictly_before, True), (i - 1, None, False)]
    diff_pair = [(i, 0, True), (i - 1, 1, False)]
    head_diff_tiles = jnp.where(i == 0, 1, 3 - (i & 1))
    carries, state = lax.switch(
        head_diff_tiles - 1,
        [lambda: head_tiles(sb_pair[:1], diff_pair[:1]),
         lambda: head_tiles(sb_pair, diff_pair),
         lambda: head_tiles(sb_pair, diff_pair + [(i - 2, None, False)])])

    def far_pair(t, st):
        m_run = [s[0] for s in st]
        l_run = [s[1] for s in st]
        j_low = i - head_diff_tiles - 1 - 2 * t
        _interleave(diff_wide_chains(j_low, m_run, l_run, MXU_LEAD_LOOP))
        return tuple(zip(m_run, l_run))

    n_far = i + 1 - head_diff_tiles
    state = lax.fori_loop(0, lax.shift_right_logical(n_far, 1), far_pair, state)

    def live(c):
        j, carry = c
        return jnp.logical_and(j >= 0, jnp.max(jnp.concatenate(carry, axis=0)) > SB_DEAD_LOG2)

    def earlier_tile(c):
        j, carry = c
        carry = list(carry)
        _interleave(sb_chains([(j, None, False)], carry, MXU_LEAD_LOOP))
        return j - 1, tuple(carry)

    lax.while_loop(live, earlier_tile, (i - 2, carries))

    o_ref[0, :, 0:SB_W] = acc_a[...].T.astype(BF16)

    lv = lam_ref[...]
    lam = (jnp.exp(jnp.sum(lv[0:1] * lv[1:2], axis=-1, keepdims=True))
           - jnp.exp(jnp.sum(lv[2:3] * lv[3:4], axis=-1, keepdims=True)) + lam_init)
    outs = []
    for h in range(DIFF_HEADS):
        (_, l1), (_, l2) = state[2 * h], state[2 * h + 1]
        o = acc_d[2 * h] / l1 - lam * (acc_d[2 * h + 1] / l2)
        ms = jnp.mean(o * o, axis=0, keepdims=True)
        outs.append(o * lax.rsqrt(ms + EPS) * gain_ref[...] * (1.0 - lam_init))
    o_ref[0, :, SB_W:SB_W + DIFF_W] = jnp.concatenate(outs, axis=0).T.astype(BF16)


def _attention(sinks, p1, p2, p3, mtri, dbias, sbias, lam_vecs, sub_gain, lam_init, batch, seq):
    nq = seq // ATT_TILE
    p1 = p1.reshape(batch, seq, P1_W)
    p2 = p2.reshape(batch, seq, P2_W)
    kdup_w = 2 * SWA_KV_W
    q_tile = lambda col: pl.BlockSpec((1, ATT_TILE, SB_W), lambda b, i: (b, i, col))
    keys = lambda col: pl.BlockSpec((1, seq, SB_W), lambda b, i: (b, 0, col))
    values_t = lambda rows, blk: pl.BlockSpec((nq, rows, ATT_TILE), lambda b, i: (b, blk, 0))
    whole = lambda shape: pl.BlockSpec(shape, lambda b, i: (0,) * len(shape),
                                       pipeline_mode=pl.Buffered(1))
    assert SB_W == DIFF_W == SWA_Q_W
    return pl.pallas_call(
        functools.partial(_attn_kernel, lam_init=lam_init),
        name="attention",
        grid=(batch, nq),
        in_specs=[
            pl.BlockSpec(memory_space=pltpu.SMEM),
            q_tile(0), keys(1),
            q_tile(0), keys(1),
            q_tile(2),
            pl.BlockSpec((1, seq, kdup_w), lambda b, i: (b, 0, (2 * DIFF_W + SWA_Q_W) // kdup_w)),
            values_t(SB_W, 0), values_t(DIFF_W, 1),
            values_t(SWA_KV_W, (SB_W + DIFF_W) // SWA_KV_W),
            whole((ATT_TILE, ATT_TILE)),
            whole((DIFF_HEADS, 2, ATT_TILE, ATT_TILE)),
            whole((SWA_Q_HEADS, 2, 2 * WINDOW, WINDOW)),
            whole((4, HEAD_DIM)),
            whole((PAIR_W, 1)),
        ],
        out_specs=pl.BlockSpec((1, ATT_TILE, ATT_OUT_W), lambda b, i: (b, i, 0)),
        out_shape=jax.ShapeDtypeStruct((batch, seq, ATT_OUT_W), BF16),
        scratch_shapes=[
            pltpu.VMEM((SB_W, ATT_TILE), F32),
            pltpu.VMEM((2 * SB_HEADS, ATT_TILE, ATT_TILE), F32),
            pltpu.VMEM((2 * SB_HEADS, ATT_TILE, ATT_TILE), F32),
            pltpu.VMEM((2 * DIFF_HEADS, PAIR_W, ATT_TILE), F32),
            pltpu.VMEM((6 * DIFF_HEADS, ATT_TILE, ATT_TILE), F32),
            pltpu.VMEM((6 * DIFF_HEADS, PAIR_W + SUM_ROWS, ATT_TILE), F32),
            pltpu.VMEM((2 * SWA_Q_HEADS, 2 * WINDOW, WINDOW), F32),
            pltpu.VMEM((2 * SWA_Q_HEADS, HEAD_DIM + SUM_ROWS, WINDOW), F32),
        ],
        compiler_params=_compiler_params(("parallel", "arbitrary"), "attention"),
    )(sinks, p1, p1, p2, p2, p2, p2, p3, p3, p3, mtri, dbias, sbias, lam_vecs,
      sub_gain.reshape(PAIR_W, 1))


def _merge_ffn_kernel(x_ref, osb_ref, odf_ref, osw_ref, ga_ref, gd_ref, gs_ref,
                      wsb_ref, wdf_ref, wsw_ref, wo_ref, ng_ref, wg_ref, wu_ref, wd_ref, o_ref):
    def branch(o_ref_, w_ref, g_ref):
        gate = 1.0 / (1.0 + jnp.exp(-g_ref[...].astype(F32)))
        return gate * jnp.dot(o_ref_[...], w_ref[...], preferred_element_type=F32)

    merged = (branch(osb_ref, wsb_ref, ga_ref) + branch(odf_ref, wdf_ref, gd_ref)
              + branch(osw_ref, wsw_ref, gs_ref))
    x = x_ref[...] + jnp.dot(merged.astype(BF16), wo_ref[...], preferred_element_type=F32)
    h = _rmsnorm_rows(x, ng_ref[...]).astype(BF16)
    out = x
    for start, width in MERGE_FFN_CHUNKS:
        cols = slice(start, start + width)
        g = jnp.dot(h, wg_ref[:, cols], preferred_element_type=F32)
        u = jnp.dot(h, wu_ref[:, cols], preferred_element_type=F32)
        a = (g * (1.0 / (1.0 + jnp.exp(-g))) * u).astype(BF16)
        out = out + 0.5 * jnp.dot(a, wd_ref[cols, :], preferred_element_type=F32)
    o_ref[...] = out


def _merge_ffn(xt, att, p1, w_sb, w_diff, w_swa, w_out, gain, w_in, w_down, layer):
    t, d = xt.shape
    d_ff = w_down.shape[1]
    assert sum(w for _, w in MERGE_FFN_CHUNKS) == d_ff
    gcol = 2 * SB_W // d
    row = lambda i: (i, 0)
    resident = lambda shape, col=0: pl.BlockSpec((None,) + shape, lambda i: (layer, 0, col),
                                                 pipeline_mode=pl.Buffered(1))
    branch_out = lambda n: pl.BlockSpec((ROW_TILE, SB_W), lambda i: (i, n))
    return pl.pallas_call(
        _merge_ffn_kernel,
        name="merge_ffn",
        grid=(t // ROW_TILE,),
        in_specs=[
            pl.BlockSpec((ROW_TILE, d), row),
            branch_out(0), branch_out(1), branch_out(2),
            pl.BlockSpec((ROW_TILE, d), lambda i: (i, gcol)),
            pl.BlockSpec((ROW_TILE, d), lambda i: (i, gcol + 1)),
            pl.BlockSpec((ROW_TILE, d), lambda i: (i, gcol + 2)),
            resident((SB_W, d)), resident((DIFF_W, d)), resident((SWA_Q_W, d)), resident((d, d)),
            pl.BlockSpec((1, d), lambda i: (0, 0)),
            resident((d, d_ff), 0), resident((d, d_ff), 1), resident((d_ff, d)),
        ],
        out_specs=pl.BlockSpec((ROW_TILE, d), row),
        out_shape=jax.ShapeDtypeStruct((t, d), F32),
        compiler_params=_compiler_params(("parallel",), "merge_ffn"),
    )(xt, att, att, att, p1, p1, p1, w_sb, w_diff, w_swa, w_out,
      gain.reshape(1, d), w_in, w_in, w_down)


def _qk_norm_gains(q_norm_diff, k_norm_diff, q_norm_swa, k_norm_swa):
    qscale = SCALE * LOG2E
    return jnp.concatenate([
        jnp.tile(q_norm_diff * qscale, DIFF_W // HEAD_DIM), jnp.tile(k_norm_diff, DIFF_W // HEAD_DIM),
        jnp.tile(q_norm_swa * qscale, SWA_Q_W // HEAD_DIM), jnp.tile(k_norm_swa, 2 * SWA_KV_W // HEAD_DIM),
    ]).reshape(1, P2_W).astype(F32)


def _group_mean_matrix():
    g = jnp.arange(P2_CHUNK) // HEAD_DIM
    return jnp.where(g[:, None] == g[None, :], 1.0 / HEAD_DIM, 0.0).astype(BF16)


def _neg_suffix_matrix():
    idx = jnp.arange(ATT_TILE)
    return jnp.where(idx[None, :] > idx[:, None], -1.0, 0.0).astype(BF16)


def kernel(x, ffn1_norm, ffn1_w_in, ffn1_w_out, mix_norm, w_in, q_norm_diff, k_norm_diff, q_norm_swa, k_norm_swa, diff_lambda, diff_subln, swa_sinks, rel_bias, w_proj_sb, w_proj_diff, w_proj_swa, w_out, ffn2_norm, ffn2_w_in, ffn2_w_out):
    batch, seq, d = x.shape
    depth = ffn1_norm.shape[0]
    assert d == D_MODEL and seq % ATT_TILE == 0 and (batch * seq) % ROW_TILE == 0
    xt = x.reshape(batch * seq, d)
    dbias, sbias = _bias_tiles(rel_bias)
    gmat = _group_mean_matrix()
    mtri = _neg_suffix_matrix()
    ffn1_in, ffn1_out = _cast_bf16(ffn1_w_in), _cast_bf16(ffn1_w_out)
    ffn2_in, ffn2_out = _cast_bf16(ffn2_w_in), _cast_bf16(ffn2_w_out)
    w1, w2, w3 = _regroup_projection(w_in)
    w_sb, w_diff, w_swa, w_o = (w.astype(BF16) for w in (w_proj_sb, w_proj_diff, w_proj_swa, w_out))
    for l in range(depth):
        lam_init = 0.8 - 0.6 * math.exp(-0.3 * l)
        xt = _ffn(xt, ffn1_norm[l], ffn1_in, ffn1_out, l)
        g2 = _qk_norm_gains(q_norm_diff[l], k_norm_diff[l], q_norm_swa[l], k_norm_swa[l])
        p1, p2, p3 = _projection(xt, mix_norm[l], w1, w2, g2, gmat, w3, l)
        att = _attention(swa_sinks[l], p1, p2, p3, mtri, dbias, sbias, diff_lambda[l],
                         diff_subln[l], lam_init, batch, seq)
        xt = _merge_ffn(xt, att.reshape(batch * seq, ATT_OUT_W), p1, w_sb, w_diff, w_swa, w_o,
                        ffn2_norm[l], ffn2_in, ffn2_out, l)
    return xt.reshape(batch, seq, d)
```

```python
import functools
import math

import jax
import jax.numpy as jnp
from jax import lax
from jax.experimental import pallas as pl
from jax.experimental.pallas import tpu as pltpu

F32 = jnp.float32
BF16 = jnp.bfloat16

D_MODEL = 1024
HEAD_DIM = 64
SB_HEADS = 8
DIFF_HEADS = 4
SWA_Q_HEADS = 8
SWA_KV_HEADS = 2
SWA_GROUP = SWA_Q_HEADS // SWA_KV_HEADS
WINDOW = 128
N_BUCKETS = 32
MAX_DISTANCE = 128
EPS = 1e-6
SCALE = HEAD_DIM ** -0.5
LOG2E = math.log2(math.e)
SB_DEAD_LOG2 = -160.0
MXU_LEAD_MIXED = 2
MXU_LEAD_LOOP = 4

SB_W = SB_HEADS * HEAD_DIM
DIFF_W = DIFF_HEADS * 2 * HEAD_DIM
SWA_Q_W = SWA_Q_HEADS * HEAD_DIM
SWA_KV_W = SWA_KV_HEADS * HEAD_DIM
ATT_OUT_W = SB_W + DIFF_W + SWA_Q_W

V7X_LANES = 128
V7X_MXU_DIM = 256
V7X_VMEM_BYTES = 64 * 1024 * 1024

ATT_TILE = V7X_MXU_DIM
ROW_TILE = 512
CAST_ROWS = 256
PAIR_W = 2 * HEAD_DIM
SUM_ROWS = 16
assert PAIR_W == V7X_LANES and 2 * WINDOW == ATT_TILE

P1_W = 2 * SB_W + 3 * D_MODEL
P2_W = 2 * DIFF_W + SWA_Q_W + 2 * SWA_KV_W
P3_W = SB_W + DIFF_W + SWA_KV_W
P1_CHUNK = 512
P2_CHUNK = 256
P3_CHUNKS = ((0, 256), (256, 256), (512, 256), (768, 256), (1024, 128))
assert sum(w for _, w in P3_CHUNKS) == P3_W
MERGE_FFN_CHUNKS = ((0, 6 * V7X_MXU_DIM), (6 * V7X_MXU_DIM, 5 * V7X_MXU_DIM))
FFN_ROW_TILE = 1024
FFN_CHUNKS = ((0, 768), (768, 768), (1536, 768), (2304, 512))

_NT = (((1,), (1,)), ((), ()))


VMEM_MIB = {"cast": 24, "ffn": 56, "mixer_proj": 48, "attention": 58, "merge_ffn": 58}


def _compiler_params(semantics, call):
    vmem_bytes = VMEM_MIB[call] * 1024 * 1024
    assert vmem_bytes < V7X_VMEM_BYTES
    return pltpu.CompilerParams(dimension_semantics=semantics, vmem_limit_bytes=vmem_bytes)


def _rmsnorm_rows(x, g):
    ms = jnp.mean(x * x, axis=-1, keepdims=True)
    return x * lax.rsqrt(ms + EPS) * g


def _cast_kernel(w_ref, o_ref):
    o_ref[...] = w_ref[...].astype(BF16)


def _cast_bf16(w):
    depth, rows, cols = w.shape
    assert rows % CAST_ROWS == 0
    spec = pl.BlockSpec((None, CAST_ROWS, cols), lambda l, r: (l, r, 0))
    return pl.pallas_call(
        _cast_kernel,
        name="cast_bf16",
        grid=(depth, rows // CAST_ROWS),
        in_specs=[spec],
        out_specs=spec,
        out_shape=jax.ShapeDtypeStruct(w.shape, BF16),
        compiler_params=_compiler_params(("parallel", "parallel"), "cast"),
    )(w)


def _regroup_kernel(w_ref, w1_ref, w2_ref, w3_ref):
    def cols(start, width):
        return w_ref[:, start:start + width]

    offs = {}
    acc = 0
    for name, width in (("qa", SB_W), ("ka", SB_W), ("va", SB_W), ("qd", DIFF_W), ("kd", DIFF_W),
                        ("vd", DIFF_W), ("qs", SWA_Q_W), ("ks", SWA_KV_W), ("vs", SWA_KV_W),
                        ("gates", 3 * D_MODEL)):
        offs[name] = (acc, width)
        acc += width
    w1_ref[:, 0:SB_W] = (cols(*offs["qa"]) * (SCALE * LOG2E)).astype(BF16)
    w1_ref[:, SB_W:2 * SB_W] = cols(*offs["ka"]).astype(BF16)
    w1_ref[:, 2 * SB_W:P1_W] = cols(*offs["gates"]).astype(BF16)
    w2_ref[:, 0:DIFF_W] = cols(*offs["qd"]).astype(BF16)
    w2_ref[:, DIFF_W:2 * DIFF_W] = cols(*offs["kd"]).astype(BF16)
    w2_ref[:, 2 * DIFF_W:2 * DIFF_W + SWA_Q_W] = cols(*offs["qs"]).astype(BF16)
    ks = cols(*offs["ks"])
    swapped = pltpu.roll(ks, HEAD_DIM, axis=1)
    lane = lax.broadcasted_iota(jnp.int32, ks.shape, 1)
    base = 2 * DIFF_W + SWA_Q_W
    w2_ref[:, base:base + PAIR_W] = jnp.where(lane < HEAD_DIM, ks, swapped).astype(BF16)
    w2_ref[:, base + PAIR_W:P2_W] = jnp.where(lane < HEAD_DIM, swapped, ks).astype(BF16)
    w3_ref[:, 0:SB_W] = cols(*offs["va"]).astype(BF16)
    w3_ref[:, SB_W:SB_W + DIFF_W] = cols(*offs["vd"]).astype(BF16)
    w3_ref[:, SB_W + DIFF_W:P3_W] = cols(*offs["vs"]).astype(BF16)


def _regroup_projection(w_in):
    depth, d, width = w_in.shape
    assert SWA_KV_W == PAIR_W and d % CAST_ROWS == 0
    block = lambda cols: pl.BlockSpec((None, CAST_ROWS, cols), lambda l, r: (l, r, 0))
    return pl.pallas_call(
        _regroup_kernel,
        name="regroup_projection",
        grid=(depth, d // CAST_ROWS),
        in_specs=[block(width)],
        out_specs=[block(P1_W), block(P2_W), block(P3_W)],
        out_shape=[jax.ShapeDtypeStruct((depth, d, w), BF16) for w in (P1_W, P2_W, P3_W)],
        compiler_params=_compiler_params(("parallel", "parallel"), "cast"),
    )(w_in)


def _ffn_kernel(x_ref, g_ref, wg_ref, wu_ref, wo_ref, o_ref):
    x = x_ref[...]
    h = _rmsnorm_rows(x, g_ref[...]).astype(BF16)
    out = x
    for start, width in FFN_CHUNKS:
        cols = slice(start, start + width)
        g = jnp.dot(h, wg_ref[:, cols], preferred_element_type=F32)
        u = jnp.dot(h, wu_ref[:, cols], preferred_element_type=F32)
        a = (g * (1.0 / (1.0 + jnp.exp(-g))) * u).astype(BF16)
        out = out + 0.5 * jnp.dot(a, wo_ref[cols, :], preferred_element_type=F32)
    o_ref[...] = out


def _ffn(xt, gain, w_in, w_out, layer):
    t, d = xt.shape
    d_ff = w_out.shape[1]
    assert d_ff % V7X_MXU_DIM == 0 and t % FFN_ROW_TILE == 0
    resident = pl.Buffered(1)
    return pl.pallas_call(
        _ffn_kernel,
        name="ffn",
        grid=(t // FFN_ROW_TILE,),
        in_specs=[
            pl.BlockSpec((FFN_ROW_TILE, d), lambda i: (i, 0)),
            pl.BlockSpec((1, d), lambda i: (0, 0)),
            pl.BlockSpec((None, d, d_ff), lambda i: (layer, 0, 0), pipeline_mode=resident),
            pl.BlockSpec((None, d, d_ff), lambda i: (layer, 0, 1), pipeline_mode=resident),
            pl.BlockSpec((None, d_ff, d), lambda i: (layer, 0, 0), pipeline_mode=resident),
        ],
        out_specs=pl.BlockSpec((FFN_ROW_TILE, d), lambda i: (i, 0)),
        out_shape=jax.ShapeDtypeStruct((t, d), F32),
        compiler_params=_compiler_params(("parallel",), "ffn"),
    )(xt, gain.reshape(1, d), w_in, w_in, w_out)


def _proj_kernel(x_ref, ng_ref, w1_ref, w2_ref, g2_ref, gm_ref, w3_ref,
                 p1_ref, p2_ref, p3_ref):
    h = _rmsnorm_rows(x_ref[...], ng_ref[...]).astype(BF16)
    for c in range(P1_W // P1_CHUNK):
        cols = slice(c * P1_CHUNK, (c + 1) * P1_CHUNK)
        p1_ref[:, cols] = jnp.dot(h, w1_ref[:, cols], preferred_element_type=F32).astype(BF16)
    y_all = jnp.dot(h, w2_ref[...], preferred_element_type=F32)
    for c in range(P2_W // P2_CHUNK):
        cols = slice(c * P2_CHUNK, (c + 1) * P2_CHUNK)
        y = y_all[:, cols]
        ms = jnp.dot((y * y).astype(BF16), gm_ref[...], preferred_element_type=F32)
        p2_ref[:, cols] = (y * lax.rsqrt(ms + EPS) * g2_ref[:, cols]).astype(BF16)
    for start, width in P3_CHUNKS:
        cols = slice(start, start + width)
        yt = jnp.dot(h, w3_ref[:, cols], preferred_element_type=F32).T
        for r in range(ROW_TILE // ATT_TILE):
            p3_ref[r, cols, :] = yt[:, r * ATT_TILE:(r + 1) * ATT_TILE].astype(BF16)


def _projection(xt, norm_gain, w1, w2, g2, gmat, w3, layer):
    t, d = xt.shape
    const = lambda i: (0, 0)
    weights = lambda cols: pl.BlockSpec((None, d, cols), lambda i: (layer, 0, 0),
                                        pipeline_mode=pl.Buffered(1))
    return pl.pallas_call(
        _proj_kernel,
        name="mixer_proj",
        grid=(t // ROW_TILE,),
        in_specs=[
            pl.BlockSpec((ROW_TILE, d), lambda i: (i, 0)),
            pl.BlockSpec((1, d), const),
            weights(P1_W),
            weights(P2_W),
            pl.BlockSpec((1, P2_W), const),
            pl.BlockSpec((P2_CHUNK, P2_CHUNK), const),
            weights(P3_W),
        ],
        out_specs=[
            pl.BlockSpec((ROW_TILE, P1_W), lambda i: (i, 0)),
            pl.BlockSpec((ROW_TILE, P2_W), lambda i: (i, 0)),
            pl.BlockSpec((ROW_TILE // ATT_TILE, P3_W, ATT_TILE), lambda i: (i, 0, 0)),
        ],
        out_shape=[
            jax.ShapeDtypeStruct((t, P1_W), BF16),
            jax.ShapeDtypeStruct((t, P2_W), BF16),
            jax.ShapeDtypeStruct((t // ATT_TILE, P3_W, ATT_TILE), BF16),
        ],
        compiler_params=_compiler_params(("parallel",), "mixer_proj"),
    )(xt, norm_gain.reshape(1, d), w1, w2, g2, gmat, w3)


def _t5_bucket(dist):
    n = jnp.maximum(dist, 0)
    max_exact = N_BUCKETS // 2
    nf = jnp.maximum(n, 1).astype(F32)
    large = max_exact + (jnp.log(nf / max_exact) / math.log(MAX_DISTANCE / max_exact)
                         * (N_BUCKETS - max_exact)).astype(jnp.int32)
    large = jnp.minimum(large, N_BUCKETS - 1)
    return jnp.where(n < max_exact, n, large)


def _bias_lookup(bucket, tab_ref, head):
    val = jnp.zeros(bucket.shape, F32)
    for b in range(N_BUCKETS):
        val = jnp.where(bucket == b, tab_ref[b, head], val)
    return val


def _bias_kernel(tab_ref, dbias_ref, sbias_ref):
    kk = lax.broadcasted_iota(jnp.int32, (ATT_TILE, ATT_TILE), 0)
    qq = lax.broadcasted_iota(jnp.int32, (ATT_TILE, ATT_TILE), 1)
    for v in range(2):
        dist = qq - kk + v * ATT_TILE
        bucket = _t5_bucket(dist)
        for h in range(DIFF_HEADS):
            val = (_bias_lookup(bucket, tab_ref, h) - tab_ref[N_BUCKETS - 1, h]) * LOG2E
            dbias_ref[h, v] = jnp.where(dist >= 0, val, -jnp.inf)
    kk = lax.broadcasted_iota(jnp.int32, (2 * WINDOW, WINDOW), 0)
    qq = lax.broadcasted_iota(jnp.int32, (2 * WINDOW, WINDOW), 1)
    for v in range(2):
        dist = qq - kk + (1 - v) * WINDOW
        bucket = _t5_bucket(dist)
        for h in range(SWA_Q_HEADS):
            val = _bias_lookup(bucket, tab_ref, DIFF_HEADS + h) * LOG2E
            sbias_ref[h, v] = jnp.where(dist >= 0, jnp.where(dist < WINDOW, val, -jnp.inf), -jnp.inf)


def _bias_tiles(rel_bias):
    return pl.pallas_call(
        _bias_kernel,
        name="bias_tiles",
        in_specs=[pl.BlockSpec(memory_space=pltpu.SMEM)],
        out_specs=[pl.BlockSpec(memory_space=pltpu.VMEM), pl.BlockSpec(memory_space=pltpu.VMEM)],
        out_shape=[
            jax.ShapeDtypeStruct((DIFF_HEADS, 2, ATT_TILE, ATT_TILE), F32),
            jax.ShapeDtypeStruct((SWA_Q_HEADS, 2, 2 * WINDOW, WINDOW), F32),
        ],
    )(rel_bias)


def _split_pairs(q, n_pairs):
    lane = lax.broadcasted_iota(jnp.int32, (1, PAIR_W), 1)
    low = jnp.where(lane < HEAD_DIM, 1.0, 0.0).astype(BF16)
    high = jnp.where(lane >= HEAD_DIM, 1.0, 0.0).astype(BF16)
    heads = []
    for p in range(n_pairs):
        pair = q[:, p * PAIR_W:(p + 1) * PAIR_W]
        heads.append(pair * low)
        heads.append(pair * high)
    return heads


def _neg_abs(x):
    bits = lax.bitcast_convert_type(x, jnp.uint32) | jnp.uint32(0x80000000)
    return lax.bitcast_convert_type(bits, F32)


def _lead(rounds):
    for _ in range(rounds):
        yield


def _interleave(chains):
    live = []
    started = 0
    while started < len(chains) or live:
        if started < len(chains):
            live.append(started)
            started += 1
        for n in reversed(list(live)):
            try:
                next(chains[n])
            except StopIteration:
                live.remove(n)


def _round_robin(groups):
    merged = []
    for n in range(max(len(g) for g in groups)):
        merged.extend(g[n] for g in groups if n < len(g))
    return merged


def _key_tile(ref, j):
    return ref[0, pl.ds(pl.multiple_of(j * ATT_TILE, ATT_TILE), ATT_TILE), :]


def _attn_kernel(sink_ref, qa_ref, ka_ref, qd_ref, kd_ref, qs_ref, ks_ref,
                 vta_ref, vtd_ref, vts_ref, mtri_ref, dbias_ref, sbias_ref, lam_ref, gain_ref,
                 o_ref, acc_a, z_ref, btw_ref, acc_d, sd_ref, pvd_ref, ss_ref, pvs_ref,
                 *, lam_init):
    i = pl.program_id(1)
    first = i == 0
    prev = jnp.maximum(i - 1, 0)
    ones_rows = jnp.ones((SUM_ROWS, ATT_TILE), BF16)

    qa_heads = _split_pairs(qa_ref[0], SB_HEADS // 2)
    mtri = mtri_ref[...]
    kk = lax.broadcasted_iota(jnp.int32, (ATT_TILE, ATT_TILE), 0)
    qq = lax.broadcasted_iota(jnp.int32, (ATT_TILE, ATT_TILE), 1)
    strictly_before = kk < qq

    def sb_chains(tiles, carry, lead):
        def chain(k, vt, keep, fresh, h, slot):
            k_pair = k[:, (h // 2) * PAIR_W:(h // 2 + 1) * PAIR_W]
            rows = slice(h * HEAD_DIM, (h + 1) * HEAD_DIM)
            z_ref[slot] = lax.dot_general(k_pair, qa_heads[h], _NT, preferred_element_type=F32)
            yield from _lead(lead)
            z = z_ref[slot]
            sp = jnp.maximum(z, 0.0) + jnp.log(1.0 + jnp.exp2(_neg_abs(z))) * LOG2E
            if keep is not None:
                sp = jnp.where(keep, sp, 0.0)
            sp_b = sp.astype(BF16)
            yield
            btw_ref[slot] = jnp.dot(mtri, sp_b, preferred_element_type=F32)
            yield from _lead(lead)
            between = btw_ref[slot]
            after = carry[h]
            carry[h] = after + (between[0:1, :] - sp_b[0:1, :].astype(F32))
            w = jnp.exp2((z - sp) + between)
            if keep is not None:
                w = jnp.where(keep, w, 0.0)
            w = w.astype(BF16)
            yield
            pv = jnp.dot(vt[rows], w, preferred_element_type=F32)
            if fresh:
                acc_a[rows, :] = pv
            else:
                acc_a[rows, :] += pv * jnp.exp2(after)

        chains = []
        for t, (j, keep, fresh) in enumerate(tiles):
            k = _key_tile(ka_ref, j)
            vt = vta_ref[j]
            chains.extend(chain(k, vt, keep, fresh, h, SB_HEADS * t + h) for h in range(SB_HEADS))
        return chains

    qd_comps = _split_pairs(qd_ref[0], DIFF_HEADS)

    def diff_chains(tiles, m_run, l_run, lead):
        def chain(k, vt, variant, fresh, n, slot):
            h = n // 2
            sd_ref[slot] = lax.dot_general(k[:, h * PAIR_W:(h + 1) * PAIR_W], qd_comps[n], _NT,
                                           preferred_element_type=F32)
            yield from _lead(lead)
            s = sd_ref[slot]
            if variant is not None:
                s = s + dbias_ref[h, variant]
            m_prev = m_run[n]
            m_new = jnp.maximum(m_prev, jnp.max(s, axis=0, keepdims=True))
            alpha = jnp.exp2(m_prev - m_new)
            m_run[n] = m_new
            yield
            p = jnp.exp2(s - m_new).astype(BF16)
            yield
            pvd_ref[slot] = jnp.dot(jnp.concatenate([vt[h * PAIR_W:(h + 1) * PAIR_W], ones_rows], axis=0),
                                    p, preferred_element_type=F32)
            yield from _lead(lead)
            pv = pvd_ref[slot]
            if fresh:
                l_run[n] = pv[PAIR_W:PAIR_W + 1]
                acc_d[n] = pv[:PAIR_W]
            else:
                l_run[n] = alpha * l_run[n] + pv[PAIR_W:PAIR_W + 1]
                acc_d[n] = alpha * acc_d[n] + pv[:PAIR_W]

        chains = []
        for t, (j, variant, fresh) in enumerate(tiles):
            k = _key_tile(kd_ref, j)
            vt = vtd_ref[j]
            chains.extend(chain(k, vt, variant, fresh, n, 2 * DIFF_HEADS * t + n)
                          for n in range(2 * DIFF_HEADS))
        return chains

    def diff_wide_chains(j_low, m_run, l_run, lead):
        k = kd_ref[0, pl.ds(pl.multiple_of(j_low * ATT_TILE, ATT_TILE), 2 * ATT_TILE), :]
        vt = jnp.concatenate([vtd_ref[j_low], vtd_ref[j_low + 1]], axis=1)
        ones_wide = jnp.ones((SUM_ROWS, 2 * ATT_TILE), BF16)

        def chain(n):
            h = n // 2
            s = lax.dot_general(k[:, h * PAIR_W:(h + 1) * PAIR_W], qd_comps[n], _NT,
                                preferred_element_type=F32)
            sd_ref[2 * n:2 * n + 2] = s.reshape(2, ATT_TILE, ATT_TILE)
            yield from _lead(lead)
            s = sd_ref[2 * n:2 * n + 2].reshape(2 * ATT_TILE, ATT_TILE)
            m_prev = m_run[n]
            m_new = jnp.maximum(m_prev, jnp.max(s, axis=0, keepdims=True))
            alpha = jnp.exp2(m_prev - m_new)
            m_run[n] = m_new
            yield
            p = jnp.exp2(s - m_new).astype(BF16)
            yield
            pvd_ref[n] = jnp.dot(jnp.concatenate([vt[h * PAIR_W:(h + 1) * PAIR_W], ones_wide], axis=0),
                                 p, preferred_element_type=F32)
            yield from _lead(lead)
            pv = pvd_ref[n]
            l_run[n] = alpha * l_run[n] + pv[PAIR_W:PAIR_W + 1]
            acc_d[n] = alpha * acc_d[n] + pv[:PAIR_W]

        return [chain(n) for n in range(2 * DIFF_HEADS)]

    def swa_chains(outs, lead):
        qs = qs_ref[0]
        k_hi = _key_tile(ks_ref, i)
        vt_hi = vts_ref[i]
        lo_start = pl.multiple_of(jnp.maximum(i * ATT_TILE - WINDOW, 0), WINDOW)
        k_lo = ks_ref[0, pl.ds(lo_start, 2 * WINDOW), :]
        vt_lo = jnp.where(first, vt_hi,
                          jnp.concatenate([vts_ref[prev][:, WINDOW:], vt_hi[:, :WINDOW]], axis=1))
        lo_variant = jnp.where(first, 1, 0)
        halves = ((_split_pairs(qs[:WINDOW], SWA_Q_HEADS // 2), k_lo, vt_lo, lo_variant),
                  (_split_pairs(qs[WINDOW:], SWA_Q_HEADS // 2), k_hi, vt_hi, 0))

        def chain(q_heads, k_win, vt_win, variant, h, slot):
            kv = h // SWA_GROUP
            k_pair = k_win[:, kv * PAIR_W:(kv + 1) * PAIR_W]
            ss_ref[slot] = lax.dot_general(k_pair, q_heads[h], _NT, preferred_element_type=F32)
            yield from _lead(lead)
            sink = sink_ref[h] * LOG2E
            s = ss_ref[slot] + sbias_ref[h, variant]
            m = jnp.maximum(jnp.max(s, axis=0, keepdims=True), sink)
            e = jnp.exp2(s - m).astype(BF16)
            yield
            pvs_ref[slot] = jnp.dot(
                jnp.concatenate([vt_win[kv * HEAD_DIM:(kv + 1) * HEAD_DIM], ones_rows], axis=0),
                e, preferred_element_type=F32)
            yield from _lead(lead)
            pv = pvs_ref[slot]
            denom = pv[HEAD_DIM:HEAD_DIM + 1] + jnp.exp2(sink - m)
            outs[slot] = pv[:HEAD_DIM] / denom

        return [chain(*half, h, a * SWA_Q_HEADS + h)
                for a, half in enumerate(halves) for h in range(SWA_Q_HEADS)]

    def any_weight_left(carry):
        return jnp.max(jnp.concatenate(carry, axis=0)) > SB_DEAD_LOG2

    def head_tiles(sb_tiles, diff_tiles):
        carry = [jnp.zeros((1, ATT_TILE), F32)] * SB_HEADS
        m_run = [jnp.full((1, ATT_TILE), -jnp.inf, F32)] * (2 * DIFF_HEADS)
        l_run = [jnp.zeros((1, ATT_TILE), F32)] * (2 * DIFF_HEADS)
        swa = [None] * (2 * SWA_Q_HEADS)
        _interleave(_round_robin([sb_chains(sb_tiles, carry, MXU_LEAD_MIXED),
                                  diff_chains(diff_tiles, m_run, l_run, MXU_LEAD_MIXED),
                                  swa_chains(swa, MXU_LEAD_MIXED)]))
        swa_t = jnp.concatenate([jnp.concatenate(swa[:SWA_Q_HEADS], axis=0),
                                 jnp.concatenate(swa[SWA_Q_HEADS:], axis=0)], axis=1)
        o_ref[0, :, SB_W + DIFF_W:ATT_OUT_W] = swa_t.T.astype(BF16)
        return tuple(carry), tuple(zip(m_run, l_run)), any_weight_left(carry)

    sb_pair = [(i, strictly_before, True), (i - 1, None, False)]
    diff_pair = [(i, 0, True), (i - 1, 1, False)]
    head_diff_tiles = jnp.where(i == 0, 1, 3 - (i & 1))
    carries, state, sb_live = lax.switch(
        head_diff_tiles - 1,
        [lambda: head_tiles(sb_pair[:1], diff_pair[:1]),
         lambda: head_tiles(sb_pair, diff_pair),
         lambda: head_tiles(sb_pair, diff_pair + [(i - 2, None, False)])])

    def far_pair(t, st):
        m_run = [s[0] for s in st]
        l_run = [s[1] for s in st]
        j_low = i - head_diff_tiles - 1 - 2 * t
        _interleave(diff_wide_chains(j_low, m_run, l_run, MXU_LEAD_LOOP))
        return tuple(zip(m_run, l_run))

    n_far = i + 1 - head_diff_tiles
    state = lax.fori_loop(0, lax.shift_right_logical(n_far, 1), far_pair, state)

    def earlier_tile(c):
        j, carry, _ = c
        carry = list(carry)
        _interleave(sb_chains([(j, None, False)], carry, MXU_LEAD_LOOP))
        return j - 1, tuple(carry), any_weight_left(carry)

    lax.while_loop(lambda c: jnp.logical_and(c[0] >= 0, c[2]), earlier_tile,
                   (i - 2, carries, sb_live))

    o_ref[0, :, 0:SB_W] = acc_a[...].T.astype(BF16)

    lv = lam_ref[...]
    lam = (jnp.exp(jnp.sum(lv[0:1] * lv[1:2], axis=-1, keepdims=True))
           - jnp.exp(jnp.sum(lv[2:3] * lv[3:4], axis=-1, keepdims=True)) + lam_init)
    outs = []
    for h in range(DIFF_HEADS):
        (_, l1), (_, l2) = state[2 * h], state[2 * h + 1]
        o = acc_d[2 * h] / l1 - lam * (acc_d[2 * h + 1] / l2)
        ms = jnp.mean(o * o, axis=0, keepdims=True)
        outs.append(o * lax.rsqrt(ms + EPS) * gain_ref[...] * (1.0 - lam_init))
    o_ref[0, :, SB_W:SB_W + DIFF_W] = jnp.concatenate(outs, axis=0).T.astype(BF16)


def _attention(sinks, p1, p2, p3, mtri, dbias, sbias, lam_vecs, sub_gain, lam_init, batch, seq):
    nq = seq // ATT_TILE
    p1 = p1.reshape(batch, seq, P1_W)
    p2 = p2.reshape(batch, seq, P2_W)
    kdup_w = 2 * SWA_KV_W
    q_tile = lambda col: pl.BlockSpec((1, ATT_TILE, SB_W), lambda b, i: (b, i, col))
    keys = lambda col: pl.BlockSpec((1, seq, SB_W), lambda b, i: (b, 0, col))
    values_t = lambda rows, blk: pl.BlockSpec((nq, rows, ATT_TILE), lambda b, i: (b, blk, 0))
    whole = lambda shape: pl.BlockSpec(shape, lambda b, i: (0,) * len(shape),
                                       pipeline_mode=pl.Buffered(1))
    assert SB_W == DIFF_W == SWA_Q_W
    return pl.pallas_call(
        functools.partial(_attn_kernel, lam_init=lam_init),
        name="attention",
        grid=(batch, nq),
        in_specs=[
            pl.BlockSpec(memory_space=pltpu.SMEM),
            q_tile(0), keys(1),
            q_tile(0), keys(1),
            q_tile(2),
            pl.BlockSpec((1, seq, kdup_w), lambda b, i: (b, 0, (2 * DIFF_W + SWA_Q_W) // kdup_w)),
            values_t(SB_W, 0), values_t(DIFF_W, 1),
            values_t(SWA_KV_W, (SB_W + DIFF_W) // SWA_KV_W),
            whole((ATT_TILE, ATT_TILE)),
            whole((DIFF_HEADS, 2, ATT_TILE, ATT_TILE)),
            whole((SWA_Q_HEADS, 2, 2 * WINDOW, WINDOW)),
            whole((4, HEAD_DIM)),
            whole((PAIR_W, 1)),
        ],
        out_specs=pl.BlockSpec((1, ATT_TILE, ATT_OUT_W), lambda b, i: (b, i, 0)),
        out_shape=jax.ShapeDtypeStruct((batch, seq, ATT_OUT_W), BF16),
        scratch_shapes=[
            pltpu.VMEM((SB_W, ATT_TILE), F32),
            pltpu.VMEM((2 * SB_HEADS, ATT_TILE, ATT_TILE), F32),
            pltpu.VMEM((2 * SB_HEADS, ATT_TILE, ATT_TILE), F32),
            pltpu.VMEM((2 * DIFF_HEADS, PAIR_W, ATT_TILE), F32),
            pltpu.VMEM((6 * DIFF_HEADS, ATT_TILE, ATT_TILE), F32),
            pltpu.VMEM((6 * DIFF_HEADS, PAIR_W + SUM_ROWS, ATT_TILE), F32),
            pltpu.VMEM((2 * SWA_Q_HEADS, 2 * WINDOW, WINDOW), F32),
            pltpu.VMEM((2 * SWA_Q_HEADS, HEAD_DIM + SUM_ROWS, WINDOW), F32),
        ],
        compiler_params=_compiler_params(("parallel", "arbitrary"), "attention"),
    )(sinks, p1, p1, p2, p2, p2, p2, p3, p3, p3, mtri, dbias, sbias, lam_vecs,
      sub_gain.reshape(PAIR_W, 1))


def _merge_ffn_kernel(x_ref, osb_ref, odf_ref, osw_ref, ga_ref, gd_ref, gs_ref,
                      wsb_ref, wdf_ref, wsw_ref, wo_ref, ng_ref, wg_ref, wu_ref, wd_ref, o_ref):
    def branch(o_ref_, w_ref, g_ref):
        gate = 1.0 / (1.0 + jnp.exp(-g_ref[...].astype(F32)))
        return gate * jnp.dot(o_ref_[...], w_ref[...], preferred_element_type=F32)

    merged = (branch(osb_ref, wsb_ref, ga_ref) + branch(odf_ref, wdf_ref, gd_ref)
              + branch(osw_ref, wsw_ref, gs_ref))
    x = x_ref[...] + jnp.dot(merged.astype(BF16), wo_ref[...], preferred_element_type=F32)
    h = _rmsnorm_rows(x, ng_ref[...]).astype(BF16)
    out = x
    for start, width in MERGE_FFN_CHUNKS:
        cols = slice(start, start + width)
        g = jnp.dot(h, wg_ref[:, cols], preferred_element_type=F32)
        u = jnp.dot(h, wu_ref[:, cols], preferred_element_type=F32)
        a = (g * (1.0 / (1.0 + jnp.exp(-g))) * u).astype(BF16)
        out = out + 0.5 * jnp.dot(a, wd_ref[cols, :], preferred_element_type=F32)
    o_ref[...] = out


def _merge_ffn(xt, att, p1, w_sb, w_diff, w_swa, w_out, gain, w_in, w_down, layer):
    t, d = xt.shape
    d_ff = w_down.shape[1]
    assert sum(w for _, w in MERGE_FFN_CHUNKS) == d_ff
    gcol = 2 * SB_W // d
    row = lambda i: (i, 0)
    resident = lambda shape, col=0: pl.BlockSpec((None,) + shape, lambda i: (layer, 0, col),
                                                 pipeline_mode=pl.Buffered(1))
    branch_out = lambda n: pl.BlockSpec((ROW_TILE, SB_W), lambda i: (i, n))
    return pl.pallas_call(
        _merge_ffn_kernel,
        name="merge_ffn",
        grid=(t // ROW_TILE,),
        in_specs=[
            pl.BlockSpec((ROW_TILE, d), row),
            branch_out(0), branch_out(1), branch_out(2),
            pl.BlockSpec((ROW_TILE, d), lambda i: (i, gcol)),
            pl.BlockSpec((ROW_TILE, d), lambda i: (i, gcol + 1)),
            pl.BlockSpec((ROW_TILE, d), lambda i: (i, gcol + 2)),
            resident((SB_W, d)), resident((DIFF_W, d)), resident((SWA_Q_W, d)), resident((d, d)),
            pl.BlockSpec((1, d), lambda i: (0, 0)),
            resident((d, d_ff), 0), resident((d, d_ff), 1), resident((d_ff, d)),
        ],
        out_specs=pl.BlockSpec((ROW_TILE, d), row),
        out_shape=jax.ShapeDtypeStruct((t, d), F32),
        compiler_params=_compiler_params(("parallel",), "merge_ffn"),
    )(xt, att, att, att, p1, p1, p1, w_sb, w_diff, w_swa, w_out,
      gain.reshape(1, d), w_in, w_in, w_down)


def _qk_norm_gains(q_norm_diff, k_norm_diff, q_norm_swa, k_norm_swa):
    qscale = SCALE * LOG2E
    return jnp.concatenate([
        jnp.tile(q_norm_diff * qscale, DIFF_W // HEAD_DIM), jnp.tile(k_norm_diff, DIFF_W // HEAD_DIM),
        jnp.tile(q_norm_swa * qscale, SWA_Q_W // HEAD_DIM), jnp.tile(k_norm_swa, 2 * SWA_KV_W // HEAD_DIM),
    ]).reshape(1, P2_W).astype(F32)


def _group_mean_matrix():
    g = jnp.arange(P2_CHUNK) // HEAD_DIM
    return jnp.where(g[:, None] == g[None, :], 1.0 / HEAD_DIM, 0.0).astype(BF16)


def _neg_suffix_matrix():
    idx = jnp.arange(ATT_TILE)
    return jnp.where(idx[None, :] > idx[:, None], -1.0, 0.0).astype(BF16)


def kernel(x, ffn1_norm, ffn1_w_in, ffn1_w_out, mix_norm, w_in, q_norm_diff, k_norm_diff, q_norm_swa, k_norm_swa, diff_lambda, diff_subln, swa_sinks, rel_bias, w_proj_sb, w_proj_diff, w_proj_swa, w_out, ffn2_norm, ffn2_w_in, ffn2_w_out):
    batch, seq, d = x.shape
    depth = ffn1_norm.shape[0]
    assert d == D_MODEL and seq % ATT_TILE == 0 and (batch * seq) % ROW_TILE == 0
    xt = x.reshape(batch * seq, d)
    dbias, sbias = _bias_tiles(rel_bias)
    gmat = _group_mean_matrix()
    mtri = _neg_suffix_matrix()
    ffn1_in, ffn1_out = _cast_bf16(ffn1_w_in), _cast_bf16(ffn1_w_out)
    ffn2_in, ffn2_out = _cast_bf16(ffn2_w_in), _cast_bf16(ffn2_w_out)
    w1, w2, w3 = _regroup_projection(w_in)
    w_sb, w_diff, w_swa, w_o = (w.astype(BF16) for w in (w_proj_sb, w_proj_diff, w_proj_swa, w_out))
    for l in range(depth):
        lam_init = 0.8 - 0.6 * math.exp(-0.3 * l)
        xt = _ffn(xt, ffn1_norm[l], ffn1_in, ffn1_out, l)
        g2 = _qk_norm_gains(q_norm_diff[l], k_norm_diff[l], q_norm_swa[l], k_norm_swa[l])
        p1, p2, p3 = _projection(xt, mix_norm[l], w1, w2, g2, gmat, w3, l)
        att = _attention(swa_sinks[l], p1, p2, p3, mtri, dbias, sbias, diff_lambda[l],
                         diff_subln[l], lam_init, batch, seq)
        xt = _merge_ffn(xt, att.reshape(batch * seq, ATT_OUT_W), p1, w_sb, w_diff, w_swa, w_o,
                        ffn2_norm[l], ffn2_in, ffn2_out, l)
    return xt.reshape(batch, seq, d)
```

```python
import functools
import math

import jax
import jax.numpy as jnp
from jax import lax
from jax.experimental import pallas as pl
from jax.experimental.pallas import tpu as pltpu

F32 = jnp.float32
BF16 = jnp.bfloat16

D_MODEL = 1024
HEAD_DIM = 64
SB_HEADS = 8
DIFF_HEADS = 4
SWA_Q_HEADS = 8
SWA_KV_HEADS = 2
SWA_GROUP = SWA_Q_HEADS // SWA_KV_HEADS
WINDOW = 128
N_BUCKETS = 32
MAX_DISTANCE = 128
EPS = 1e-6
SCALE = HEAD_DIM ** -0.5
LOG2E = math.log2(math.e)
SB_DEAD_LOG2 = -160.0
MXU_LEAD_MIXED = 2
MXU_LEAD_LOOP = 4

SB_W = SB_HEADS * HEAD_DIM
DIFF_W = DIFF_HEADS * 2 * HEAD_DIM
SWA_Q_W = SWA_Q_HEADS * HEAD_DIM
SWA_KV_W = SWA_KV_HEADS * HEAD_DIM
ATT_OUT_W = SB_W + DIFF_W + SWA_Q_W

V7X_LANES = 128
V7X_MXU_DIM = 256
V7X_VMEM_BYTES = 64 * 1024 * 1024

ATT_TILE = V7X_MXU_DIM
ROW_TILE = 512
CAST_ROWS = 256
PAIR_W = 2 * HEAD_DIM
SUM_ROWS = 16
assert PAIR_W == V7X_LANES and 2 * WINDOW == ATT_TILE

P1_W = 2 * SB_W + 3 * D_MODEL
P2_W = 2 * DIFF_W + SWA_Q_W + 2 * SWA_KV_W
P3_W = SB_W + DIFF_W + SWA_KV_W
P1_CHUNK = 512
P2_CHUNK = 256
P3_CHUNKS = ((0, 256), (256, 256), (512, 256), (768, 256), (1024, 128))
assert sum(w for _, w in P3_CHUNKS) == P3_W
MERGE_FFN_CHUNKS = ((0, 6 * V7X_MXU_DIM), (6 * V7X_MXU_DIM, 5 * V7X_MXU_DIM))
FFN_ROW_TILE = 1024
FFN_CHUNKS = ((0, 768), (768, 768), (1536, 768), (2304, 512))

_NT = (((1,), (1,)), ((), ()))


VMEM_MIB = {"cast": 24, "ffn": 56, "mixer_proj": 48, "attention": 58, "merge_ffn": 58}


def _compiler_params(semantics, call):
    vmem_bytes = VMEM_MIB[call] * 1024 * 1024
    assert vmem_bytes < V7X_VMEM_BYTES
    return pltpu.CompilerParams(dimension_semantics=semantics, vmem_limit_bytes=vmem_bytes)


def _rmsnorm_rows(x, g):
    ms = jnp.mean(x * x, axis=-1, keepdims=True)
    return x * lax.rsqrt(ms + EPS) * g


def _cast_kernel(w_ref, o_ref):
    o_ref[...] = w_ref[...].astype(BF16)


def _cast_bf16(w):
    depth, rows, cols = w.shape
    assert rows % CAST_ROWS == 0
    spec = pl.BlockSpec((None, CAST_ROWS, cols), lambda l, r: (l, r, 0))
    return pl.pallas_call(
        _cast_kernel,
        name="cast_bf16",
        grid=(depth, rows // CAST_ROWS),
        in_specs=[spec],
        out_specs=spec,
        out_shape=jax.ShapeDtypeStruct(w.shape, BF16),
        compiler_params=_compiler_params(("parallel", "parallel"), "cast"),
    )(w)


def _regroup_kernel(w_ref, w1_ref, w2_ref, w3_ref):
    def cols(start, width):
        return w_ref[:, start:start + width]

    offs = {}
    acc = 0
    for name, width in (("qa", SB_W), ("ka", SB_W), ("va", SB_W), ("qd", DIFF_W), ("kd", DIFF_W),
                        ("vd", DIFF_W), ("qs", SWA_Q_W), ("ks", SWA_KV_W), ("vs", SWA_KV_W),
                        ("gates", 3 * D_MODEL)):
        offs[name] = (acc, width)
        acc += width
    w1_ref[:, 0:SB_W] = (cols(*offs["qa"]) * (SCALE * LOG2E)).astype(BF16)
    w1_ref[:, SB_W:2 * SB_W] = cols(*offs["ka"]).astype(BF16)
    w1_ref[:, 2 * SB_W:P1_W] = cols(*offs["gates"]).astype(BF16)
    w2_ref[:, 0:DIFF_W] = cols(*offs["qd"]).astype(BF16)
    w2_ref[:, DIFF_W:2 * DIFF_W] = cols(*offs["kd"]).astype(BF16)
    w2_ref[:, 2 * DIFF_W:2 * DIFF_W + SWA_Q_W] = cols(*offs["qs"]).astype(BF16)
    ks = cols(*offs["ks"])
    swapped = pltpu.roll(ks, HEAD_DIM, axis=1)
    lane = lax.broadcasted_iota(jnp.int32, ks.shape, 1)
    base = 2 * DIFF_W + SWA_Q_W
    w2_ref[:, base:base + PAIR_W] = jnp.where(lane < HEAD_DIM, ks, swapped).astype(BF16)
    w2_ref[:, base + PAIR_W:P2_W] = jnp.where(lane < HEAD_DIM, swapped, ks).astype(BF16)
    w3_ref[:, 0:SB_W] = cols(*offs["va"]).astype(BF16)
    w3_ref[:, SB_W:SB_W + DIFF_W] = cols(*offs["vd"]).astype(BF16)
    w3_ref[:, SB_W + DIFF_W:P3_W] = cols(*offs["vs"]).astype(BF16)


def _regroup_projection(w_in):
    depth, d, width = w_in.shape
    assert SWA_KV_W == PAIR_W and d % CAST_ROWS == 0
    block = lambda cols: pl.BlockSpec((None, CAST_ROWS, cols), lambda l, r: (l, r, 0))
    return pl.pallas_call(
        _regroup_kernel,
        name="regroup_projection",
        grid=(depth, d // CAST_ROWS),
        in_specs=[block(width)],
        out_specs=[block(P1_W), block(P2_W), block(P3_W)],
        out_shape=[jax.ShapeDtypeStruct((depth, d, w), BF16) for w in (P1_W, P2_W, P3_W)],
        compiler_params=_compiler_params(("parallel", "parallel"), "cast"),
    )(w_in)


def _ffn_kernel(x_ref, g_ref, wg_ref, wu_ref, wo_ref, o_ref):
    x = x_ref[...]
    h = _rmsnorm_rows(x, g_ref[...]).astype(BF16)
    out = x
    for start, width in FFN_CHUNKS:
        cols = slice(start, start + width)
        g = jnp.dot(h, wg_ref[:, cols], preferred_element_type=F32)
        u = jnp.dot(h, wu_ref[:, cols], preferred_element_type=F32)
        a = (g * (1.0 / (1.0 + jnp.exp(-g))) * u).astype(BF16)
        out = out + 0.5 * jnp.dot(a, wo_ref[cols, :], preferred_element_type=F32)
    o_ref[...] = out


def _ffn(xt, gain, w_in, w_out, layer):
    t, d = xt.shape
    d_ff = w_out.shape[1]
    assert d_ff % V7X_MXU_DIM == 0 and t % FFN_ROW_TILE == 0
    resident = pl.Buffered(1)
    return pl.pallas_call(
        _ffn_kernel,
        name="ffn",
        grid=(t // FFN_ROW_TILE,),
        in_specs=[
            pl.BlockSpec((FFN_ROW_TILE, d), lambda i: (i, 0)),
            pl.BlockSpec((1, d), lambda i: (0, 0)),
            pl.BlockSpec((None, d, d_ff), lambda i: (layer, 0, 0), pipeline_mode=resident),
            pl.BlockSpec((None, d, d_ff), lambda i: (layer, 0, 1), pipeline_mode=resident),
            pl.BlockSpec((None, d_ff, d), lambda i: (layer, 0, 0), pipeline_mode=resident),
        ],
        out_specs=pl.BlockSpec((FFN_ROW_TILE, d), lambda i: (i, 0)),
        out_shape=jax.ShapeDtypeStruct((t, d), F32),
        compiler_params=_compiler_params(("parallel",), "ffn"),
    )(xt, gain.reshape(1, d), w_in, w_in, w_out)


def _proj_kernel(x_ref, ng_ref, w1_ref, w2_ref, g2_ref, gm_ref, w3_ref,
                 p1_ref, p2_ref, p3_ref):
    h = _rmsnorm_rows(x_ref[...], ng_ref[...]).astype(BF16)
    for c in range(P1_W // P1_CHUNK):
        cols = slice(c * P1_CHUNK, (c + 1) * P1_CHUNK)
        p1_ref[:, cols] = jnp.dot(h, w1_ref[:, cols], preferred_element_type=F32).astype(BF16)
    y_all = jnp.dot(h, w2_ref[...], preferred_element_type=F32)
    for c in range(P2_W // P2_CHUNK):
        cols = slice(c * P2_CHUNK, (c + 1) * P2_CHUNK)
        y = y_all[:, cols]
        ms = jnp.dot((y * y).astype(BF16), gm_ref[...], preferred_element_type=F32)
        p2_ref[:, cols] = (y * lax.rsqrt(ms + EPS) * g2_ref[:, cols]).astype(BF16)
    for start, width in P3_CHUNKS:
        cols = slice(start, start + width)
        yt = jnp.dot(h, w3_ref[:, cols], preferred_element_type=F32).T
        for r in range(ROW_TILE // ATT_TILE):
            p3_ref[r, cols, :] = yt[:, r * ATT_TILE:(r + 1) * ATT_TILE].astype(BF16)


def _projection(xt, norm_gain, w1, w2, g2, gmat, w3, layer):
    t, d = xt.shape
    const = lambda i: (0, 0)
    weights = lambda cols: pl.BlockSpec((None, d, cols), lambda i: (layer, 0, 0),
                                        pipeline_mode=pl.Buffered(1))
    return pl.pallas_call(
        _proj_kernel,
        name="mixer_proj",
        grid=(t // ROW_TILE,),
        in_specs=[
            pl.BlockSpec((ROW_TILE, d), lambda i: (i, 0)),
            pl.BlockSpec((1, d), const),
            weights(P1_W),
            weights(P2_W),
            pl.BlockSpec((1, P2_W), const),
            pl.BlockSpec((P2_CHUNK, P2_CHUNK), const),
            weights(P3_W),
        ],
        out_specs=[
            pl.BlockSpec((ROW_TILE, P1_W), lambda i: (i, 0)),
            pl.BlockSpec((ROW_TILE, P2_W), lambda i: (i, 0)),
            pl.BlockSpec((ROW_TILE // ATT_TILE, P3_W, ATT_TILE), lambda i: (i, 0, 0)),
        ],
        out_shape=[
            jax.ShapeDtypeStruct((t, P1_W), BF16),
            jax.ShapeDtypeStruct((t, P2_W), BF16),
            jax.ShapeDtypeStruct((t // ATT_TILE, P3_W, ATT_TILE), BF16),
        ],
        compiler_params=_compiler_params(("parallel",), "mixer_proj"),
    )(xt, norm_gain.reshape(1, d), w1, w2, g2, gmat, w3)


def _t5_bucket(dist):
    n = jnp.maximum(dist, 0)
    max_exact = N_BUCKETS // 2
    nf = jnp.maximum(n, 1).astype(F32)
    large = max_exact + (jnp.log(nf / max_exact) / math.log(MAX_DISTANCE / max_exact)
                         * (N_BUCKETS - max_exact)).astype(jnp.int32)
    large = jnp.minimum(large, N_BUCKETS - 1)
    return jnp.where(n < max_exact, n, large)


def _bias_lookup(bucket, tab_ref, head):
    val = jnp.zeros(bucket.shape, F32)
    for b in range(N_BUCKETS):
        val = jnp.where(bucket == b, tab_ref[b, head], val)
    return val


def _bias_kernel(tab_ref, dbias_ref, sbias_ref):
    kk = lax.broadcasted_iota(jnp.int32, (ATT_TILE, ATT_TILE), 0)
    qq = lax.broadcasted_iota(jnp.int32, (ATT_TILE, ATT_TILE), 1)
    for v in range(2):
        dist = qq - kk + v * ATT_TILE
        bucket = _t5_bucket(dist)
        for h in range(DIFF_HEADS):
            val = (_bias_lookup(bucket, tab_ref, h) - tab_ref[N_BUCKETS - 1, h]) * LOG2E
            dbias_ref[h, v] = jnp.where(dist >= 0, val, -jnp.inf)
    kk = lax.broadcasted_iota(jnp.int32, (2 * WINDOW, WINDOW), 0)
    qq = lax.broadcasted_iota(jnp.int32, (2 * WINDOW, WINDOW), 1)
    for v in range(2):
        dist = qq - kk + (1 - v) * WINDOW
        bucket = _t5_bucket(dist)
        for h in range(SWA_Q_HEADS):
            val = _bias_lookup(bucket, tab_ref, DIFF_HEADS + h) * LOG2E
            sbias_ref[h, v] = jnp.where(dist >= 0, jnp.where(dist < WINDOW, val, -jnp.inf), -jnp.inf)


def _bias_tiles(rel_bias):
    return pl.pallas_call(
        _bias_kernel,
        name="bias_tiles",
        in_specs=[pl.BlockSpec(memory_space=pltpu.SMEM)],
        out_specs=[pl.BlockSpec(memory_space=pltpu.VMEM), pl.BlockSpec(memory_space=pltpu.VMEM)],
        out_shape=[
            jax.ShapeDtypeStruct((DIFF_HEADS, 2, ATT_TILE, ATT_TILE), F32),
            jax.ShapeDtypeStruct((SWA_Q_HEADS, 2, 2 * WINDOW, WINDOW), F32),
        ],
    )(rel_bias)


def _split_pairs(q, n_pairs):
    lane = lax.broadcasted_iota(jnp.int32, (1, PAIR_W), 1)
    low = jnp.where(lane < HEAD_DIM, 1.0, 0.0).astype(BF16)
    high = jnp.where(lane >= HEAD_DIM, 1.0, 0.0).astype(BF16)
    heads = []
    for p in range(n_pairs):
        pair = q[:, p * PAIR_W:(p + 1) * PAIR_W]
        heads.append(pair * low)
        heads.append(pair * high)
    return heads


def _neg_abs(x):
    bits = lax.bitcast_convert_type(x, jnp.uint32) | jnp.uint32(0x80000000)
    return lax.bitcast_convert_type(bits, F32)


def _lead(rounds):
    for _ in range(rounds):
        yield


def _interleave(chains):
    live = []
    started = 0
    while started < len(chains) or live:
        if started < len(chains):
            live.append(started)
            started += 1
        for n in reversed(list(live)):
            try:
                next(chains[n])
            except StopIteration:
                live.remove(n)


def _round_robin(groups):
    merged = []
    for n in range(max(len(g) for g in groups)):
        merged.extend(g[n] for g in groups if n < len(g))
    return merged


def _key_tile(ref, j):
    return ref[0, pl.ds(pl.multiple_of(j * ATT_TILE, ATT_TILE), ATT_TILE), :]


def _attn_kernel(sink_ref, qa_ref, ka_ref, qd_ref, kd_ref, qs_ref, ks_ref,
                 vta_ref, vtd_ref, vts_ref, mtri_ref, dbias_ref, sbias_ref, lam_ref, gain_ref,
                 o_ref, acc_a, z_ref, btw_ref, acc_d, sd_ref, ss_ref, pvs_ref,
                 *, lam_init):
    i = pl.program_id(1)
    first = i == 0
    prev = jnp.maximum(i - 1, 0)
    ones_rows = jnp.ones((SUM_ROWS, ATT_TILE), BF16)

    qa_heads = _split_pairs(qa_ref[0], SB_HEADS // 2)
    mtri = mtri_ref[...]
    kk = lax.broadcasted_iota(jnp.int32, (ATT_TILE, ATT_TILE), 0)
    qq = lax.broadcasted_iota(jnp.int32, (ATT_TILE, ATT_TILE), 1)
    strictly_before = kk < qq

    def sb_chains(tiles, carry, lead):
        def chain(k, vt, keep, fresh, h, slot):
            k_pair = k[:, (h // 2) * PAIR_W:(h // 2 + 1) * PAIR_W]
            rows = slice(h * HEAD_DIM, (h + 1) * HEAD_DIM)
            z_ref[slot] = lax.dot_general(k_pair, qa_heads[h], _NT, preferred_element_type=F32)
            yield from _lead(lead)
            z = z_ref[slot]
            sp = jnp.maximum(z, 0.0) + jnp.log(1.0 + jnp.exp2(_neg_abs(z))) * LOG2E
            if keep is not None:
                sp = jnp.where(keep, sp, 0.0)
            sp_b = sp.astype(BF16)
            yield
            btw_ref[slot] = jnp.dot(mtri, sp_b, preferred_element_type=F32)
            yield from _lead(lead)
            between = btw_ref[slot]
            after = carry[h]
            carry[h] = after + (between[0:1, :] - sp_b[0:1, :].astype(F32))
            w = jnp.exp2((z - sp) + between)
            if keep is not None:
                w = jnp.where(keep, w, 0.0)
            w = w.astype(BF16)
            yield
            pv = jnp.dot(vt[rows], w, preferred_element_type=F32)
            if fresh:
                acc_a[rows, :] = pv
            else:
                acc_a[rows, :] += pv * jnp.exp2(after)

        chains = []
        for t, (j, keep, fresh) in enumerate(tiles):
            k = _key_tile(ka_ref, j)
            vt = vta_ref[j]
            chains.extend(chain(k, vt, keep, fresh, h, SB_HEADS * t + h) for h in range(SB_HEADS))
        return chains

    qd_comps = _split_pairs(qd_ref[0], DIFF_HEADS)

    def diff_chains(tiles, m_run, l_run, lead):
        def chain(k, vt, variant, fresh, n, slot):
            h = n // 2
            sd_ref[slot] = lax.dot_general(k[:, h * PAIR_W:(h + 1) * PAIR_W], qd_comps[n], _NT,
                                           preferred_element_type=F32)
            yield from _lead(lead)
            s = sd_ref[slot]
            if variant is not None:
                s = s + dbias_ref[h, variant]
            m_prev = m_run[n]
            m_new = jnp.maximum(m_prev, jnp.max(s, axis=0, keepdims=True))
            alpha = jnp.exp2(m_prev - m_new)
            m_run[n] = m_new
            yield
            p = jnp.exp2(s - m_new).astype(BF16)
            yield
            pv = jnp.dot(jnp.concatenate([vt[h * PAIR_W:(h + 1) * PAIR_W], ones_rows], axis=0),
                         p, preferred_element_type=F32)
            yield from _lead(lead)
            if fresh:
                l_run[n] = pv[PAIR_W:PAIR_W + 1]
                acc_d[n] = pv[:PAIR_W]
            else:
                l_run[n] = alpha * l_run[n] + pv[PAIR_W:PAIR_W + 1]
                acc_d[n] = alpha * acc_d[n] + pv[:PAIR_W]

        chains = []
        for t, (j, variant, fresh) in enumerate(tiles):
            k = _key_tile(kd_ref, j)
            vt = vtd_ref[j]
            chains.extend(chain(k, vt, variant, fresh, n, 2 * DIFF_HEADS * t + n)
                          for n in range(2 * DIFF_HEADS))
        return chains

    def diff_wide_chains(j_low, m_run, l_run, lead):
        k = kd_ref[0, pl.ds(pl.multiple_of(j_low * ATT_TILE, ATT_TILE), 2 * ATT_TILE), :]
        vt = jnp.concatenate([vtd_ref[j_low], vtd_ref[j_low + 1]], axis=1)
        ones_wide = jnp.ones((SUM_ROWS, 2 * ATT_TILE), BF16)

        def chain(n):
            h = n // 2
            s = lax.dot_general(k[:, h * PAIR_W:(h + 1) * PAIR_W], qd_comps[n], _NT,
                                preferred_element_type=F32)
            sd_ref[2 * n:2 * n + 2] = s.reshape(2, ATT_TILE, ATT_TILE)
            yield from _lead(lead)
            s = sd_ref[2 * n:2 * n + 2].reshape(2 * ATT_TILE, ATT_TILE)
            m_prev = m_run[n]
            m_new = jnp.maximum(m_prev, jnp.max(s, axis=0, keepdims=True))
            alpha = jnp.exp2(m_prev - m_new)
            m_run[n] = m_new
            yield
            p = jnp.exp2(s - m_new).astype(BF16)
            yield
            pv = jnp.dot(jnp.concatenate([vt[h * PAIR_W:(h + 1) * PAIR_W], ones_wide], axis=0),
                         p, preferred_element_type=F32)
            yield from _lead(lead)
            l_run[n] = alpha * l_run[n] + pv[PAIR_W:PAIR_W + 1]
            acc_d[n] = alpha * acc_d[n] + pv[:PAIR_W]

        return [chain(n) for n in range(2 * DIFF_HEADS)]

    def swa_chains(outs, lead):
        qs = qs_ref[0]
        k_hi = _key_tile(ks_ref, i)
        vt_hi = vts_ref[i]
        lo_start = pl.multiple_of(jnp.maximum(i * ATT_TILE - WINDOW, 0), WINDOW)
        k_lo = ks_ref[0, pl.ds(lo_start, 2 * WINDOW), :]
        vt_lo = jnp.where(first, vt_hi,
                          jnp.concatenate([vts_ref[prev][:, WINDOW:], vt_hi[:, :WINDOW]], axis=1))
        lo_variant = jnp.where(first, 1, 0)
        halves = ((_split_pairs(qs[:WINDOW], SWA_Q_HEADS // 2), k_lo, vt_lo, lo_variant),
                  (_split_pairs(qs[WINDOW:], SWA_Q_HEADS // 2), k_hi, vt_hi, 0))

        def chain(q_heads, k_win, vt_win, variant, h, slot):
            kv = h // SWA_GROUP
            k_pair = k_win[:, kv * PAIR_W:(kv + 1) * PAIR_W]
            ss_ref[slot] = lax.dot_general(k_pair, q_heads[h], _NT, preferred_element_type=F32)
            yield from _lead(lead)
            sink = sink_ref[h] * LOG2E
            s = ss_ref[slot] + sbias_ref[h, variant]
            m = jnp.maximum(jnp.max(s, axis=0, keepdims=True), sink)
            e = jnp.exp2(s - m).astype(BF16)
            yield
            pvs_ref[slot] = jnp.dot(
                jnp.concatenate([vt_win[kv * HEAD_DIM:(kv + 1) * HEAD_DIM], ones_rows], axis=0),
                e, preferred_element_type=F32)
            yield from _lead(lead)
            pv = pvs_ref[slot]
            denom = pv[HEAD_DIM:HEAD_DIM + 1] + jnp.exp2(sink - m)
            outs[slot] = pv[:HEAD_DIM] / denom

        return [chain(*half, h, a * SWA_Q_HEADS + h)
                for a, half in enumerate(halves) for h in range(SWA_Q_HEADS)]

    def any_weight_left(carry):
        return jnp.max(jnp.concatenate(carry, axis=0)) > SB_DEAD_LOG2

    def head_tiles(sb_tiles, diff_tiles):
        carry = [jnp.zeros((1, ATT_TILE), F32)] * SB_HEADS
        m_run = [jnp.full((1, ATT_TILE), -jnp.inf, F32)] * (2 * DIFF_HEADS)
        l_run = [jnp.zeros((1, ATT_TILE), F32)] * (2 * DIFF_HEADS)
        swa = [None] * (2 * SWA_Q_HEADS)
        _interleave(_round_robin([sb_chains(sb_tiles, carry, MXU_LEAD_MIXED),
                                  diff_chains(diff_tiles, m_run, l_run, MXU_LEAD_MIXED),
                                  swa_chains(swa, MXU_LEAD_MIXED)]))
        swa_t = jnp.concatenate([jnp.concatenate(swa[:SWA_Q_HEADS], axis=0),
                                 jnp.concatenate(swa[SWA_Q_HEADS:], axis=0)], axis=1)
        o_ref[0, :, SB_W + DIFF_W:ATT_OUT_W] = swa_t.T.astype(BF16)
        return tuple(carry), tuple(zip(m_run, l_run)), any_weight_left(carry)

    sb_pair = [(i, strictly_before, True), (i - 1, None, False)]
    diff_pair = [(i, 0, True), (i - 1, 1, False)]
    head_diff_tiles = jnp.where(i == 0, 1, 3 - (i & 1))
    carries, state, sb_live = lax.switch(
        head_diff_tiles - 1,
        [lambda: head_tiles(sb_pair[:1], diff_pair[:1]),
         lambda: head_tiles(sb_pair, diff_pair),
         lambda: head_tiles(sb_pair, diff_pair + [(i - 2, None, False)])])

    def far_pair(t, st):
        m_run = [s[0] for s in st]
        l_run = [s[1] for s in st]
        j_low = i - head_diff_tiles - 1 - 2 * t
        _interleave(diff_wide_chains(j_low, m_run, l_run, MXU_LEAD_LOOP))
        return tuple(zip(m_run, l_run))

    n_far = i + 1 - head_diff_tiles
    state = lax.fori_loop(0, lax.shift_right_logical(n_far, 1), far_pair, state)

    def earlier_tile(c):
        j, carry, _ = c
        carry = list(carry)
        _interleave(sb_chains([(j, None, False)], carry, MXU_LEAD_LOOP))
        return j - 1, tuple(carry), any_weight_left(carry)

    lax.while_loop(lambda c: jnp.logical_and(c[0] >= 0, c[2]), earlier_tile,
                   (i - 2, carries, sb_live))

    o_ref[0, :, 0:SB_W] = acc_a[...].T.astype(BF16)

    lv = lam_ref[...]
    lam = (jnp.exp(jnp.sum(lv[0:1] * lv[1:2], axis=-1, keepdims=True))
           - jnp.exp(jnp.sum(lv[2:3] * lv[3:4], axis=-1, keepdims=True)) + lam_init)
    outs = []
    for h in range(DIFF_HEADS):
        (_, l1), (_, l2) = state[2 * h], state[2 * h + 1]
        o = acc_d[2 * h] / l1 - lam * (acc_d[2 * h + 1] / l2)
        ms = jnp.mean(o * o, axis=0, keepdims=True)
        outs.append(o * lax.rsqrt(ms + EPS) * gain_ref[...] * (1.0 - lam_init))
    o_ref[0, :, SB_W:SB_W + DIFF_W] = jnp.concatenate(outs, axis=0).T.astype(BF16)


def _attention(sinks, p1, p2, p3, mtri, dbias, sbias, lam_vecs, sub_gain, lam_init, batch, seq):
    nq = seq // ATT_TILE
    p1 = p1.reshape(batch, seq, P1_W)
    p2 = p2.reshape(batch, seq, P2_W)
    kdup_w = 2 * SWA_KV_W
    q_tile = lambda col: pl.BlockSpec((1, ATT_TILE, SB_W), lambda b, i: (b, i, col))
    keys = lambda col: pl.BlockSpec((1, seq, SB_W), lambda b, i: (b, 0, col))
    values_t = lambda rows, blk: pl.BlockSpec((nq, rows, ATT_TILE), lambda b, i: (b, blk, 0))
    whole = lambda shape: pl.BlockSpec(shape, lambda b, i: (0,) * len(shape),
                                       pipeline_mode=pl.Buffered(1))
    assert SB_W == DIFF_W == SWA_Q_W
    return pl.pallas_call(
        functools.partial(_attn_kernel, lam_init=lam_init),
        name="attention",
        grid=(batch, nq),
        in_specs=[
            pl.BlockSpec(memory_space=pltpu.SMEM),
            q_tile(0), keys(1),
            q_tile(0), keys(1),
            q_tile(2),
            pl.BlockSpec((1, seq, kdup_w), lambda b, i: (b, 0, (2 * DIFF_W + SWA_Q_W) // kdup_w)),
            values_t(SB_W, 0), values_t(DIFF_W, 1),
            values_t(SWA_KV_W, (SB_W + DIFF_W) // SWA_KV_W),
            whole((ATT_TILE, ATT_TILE)),
            whole((DIFF_HEADS, 2, ATT_TILE, ATT_TILE)),
            whole((SWA_Q_HEADS, 2, 2 * WINDOW, WINDOW)),
            whole((4, HEAD_DIM)),
            whole((PAIR_W, 1)),
        ],
        out_specs=pl.BlockSpec((1, ATT_TILE, ATT_OUT_W), lambda b, i: (b, i, 0)),
        out_shape=jax.ShapeDtypeStruct((batch, seq, ATT_OUT_W), BF16),
        scratch_shapes=[
            pltpu.VMEM((SB_W, ATT_TILE), F32),
            pltpu.VMEM((2 * SB_HEADS, ATT_TILE, ATT_TILE), F32),
            pltpu.VMEM((2 * SB_HEADS, ATT_TILE, ATT_TILE), F32),
            pltpu.VMEM((2 * DIFF_HEADS, PAIR_W, ATT_TILE), F32),
            pltpu.VMEM((6 * DIFF_HEADS, ATT_TILE, ATT_TILE), F32),
            pltpu.VMEM((2 * SWA_Q_HEADS, 2 * WINDOW, WINDOW), F32),
            pltpu.VMEM((2 * SWA_Q_HEADS, HEAD_DIM + SUM_ROWS, WINDOW), F32),
        ],
        compiler_params=_compiler_params(("parallel", "arbitrary"), "attention"),
    )(sinks, p1, p1, p2, p2, p2, p2, p3, p3, p3, mtri, dbias, sbias, lam_vecs,
      sub_gain.reshape(PAIR_W, 1))


def _merge_ffn_kernel(x_ref, osb_ref, odf_ref, osw_ref, ga_ref, gd_ref, gs_ref,
                      wsb_ref, wdf_ref, wsw_ref, wo_ref, ng_ref, wg_ref, wu_ref, wd_ref, o_ref):
    def branch(o_ref_, w_ref, g_ref):
        gate = 1.0 / (1.0 + jnp.exp(-g_ref[...].astype(F32)))
        return gate * jnp.dot(o_ref_[...], w_ref[...], preferred_element_type=F32)

    merged = (branch(osb_ref, wsb_ref, ga_ref) + branch(odf_ref, wdf_ref, gd_ref)
              + branch(osw_ref, wsw_ref, gs_ref))
    x = x_ref[...] + jnp.dot(merged.astype(BF16), wo_ref[...], preferred_element_type=F32)
    h = _rmsnorm_rows(x, ng_ref[...]).astype(BF16)
    out = x
    for start, width in MERGE_FFN_CHUNKS:
        cols = slice(start, start + width)
        g = jnp.dot(h, wg_ref[:, cols], preferred_element_type=F32)
        u = jnp.dot(h, wu_ref[:, cols], preferred_element_type=F32)
        a = (g * (1.0 / (1.0 + jnp.exp(-g))) * u).astype(BF16)
        out = out + 0.5 * jnp.dot(a, wd_ref[cols, :], preferred_element_type=F32)
    o_ref[...] = out


def _merge_ffn(xt, att, p1, w_sb, w_diff, w_swa, w_out, gain, w_in, w_down, layer):
    t, d = xt.shape
    d_ff = w_down.shape[1]
    assert sum(w for _, w in MERGE_FFN_CHUNKS) == d_ff
    gcol = 2 * SB_W // d
    row = lambda i: (i, 0)
    resident = lambda shape, col=0: pl.BlockSpec((None,) + shape, lambda i: (layer, 0, col),
                                                 pipeline_mode=pl.Buffered(1))
    branch_out = lambda n: pl.BlockSpec((ROW_TILE, SB_W), lambda i: (i, n))
    return pl.pallas_call(
        _merge_ffn_kernel,
        name="merge_ffn",
        grid=(t // ROW_TILE,),
        in_specs=[
            pl.BlockSpec((ROW_TILE, d), row),
            branch_out(0), branch_out(1), branch_out(2),
            pl.BlockSpec((ROW_TILE, d), lambda i: (i, gcol)),
            pl.BlockSpec((ROW_TILE, d), lambda i: (i, gcol + 1)),
            pl.BlockSpec((ROW_TILE, d), lambda i: (i, gcol + 2)),
            resident((SB_W, d)), resident((DIFF_W, d)), resident((SWA_Q_W, d)), resident((d, d)),
            pl.BlockSpec((1, d), lambda i: (0, 0)),
            resident((d, d_ff), 0), resident((d, d_ff), 1), resident((d_ff, d)),
        ],
        out_specs=pl.BlockSpec((ROW_TILE, d), row),
        out_shape=jax.ShapeDtypeStruct((t, d), F32),
        compiler_params=_compiler_params(("parallel",), "merge_ffn"),
    )(xt, att, att, att, p1, p1, p1, w_sb, w_diff, w_swa, w_out,
      gain.reshape(1, d), w_in, w_in, w_down)


def _qk_norm_gains(q_norm_diff, k_norm_diff, q_norm_swa, k_norm_swa):
    qscale = SCALE * LOG2E
    return jnp.concatenate([
        jnp.tile(q_norm_diff * qscale, DIFF_W // HEAD_DIM), jnp.tile(k_norm_diff, DIFF_W // HEAD_DIM),
        jnp.tile(q_norm_swa * qscale, SWA_Q_W // HEAD_DIM), jnp.tile(k_norm_swa, 2 * SWA_KV_W // HEAD_DIM),
    ]).reshape(1, P2_W).astype(F32)


def _group_mean_matrix():
    g = jnp.arange(P2_CHUNK) // HEAD_DIM
    return jnp.where(g[:, None] == g[None, :], 1.0 / HEAD_DIM, 0.0).astype(BF16)


def _neg_suffix_matrix():
    idx = jnp.arange(ATT_TILE)
    return jnp.where(idx[None, :] > idx[:, None], -1.0, 0.0).astype(BF16)


def kernel(x, ffn1_norm, ffn1_w_in, ffn1_w_out, mix_norm, w_in, q_norm_diff, k_norm_diff, q_norm_swa, k_norm_swa, diff_lambda, diff_subln, swa_sinks, rel_bias, w_proj_sb, w_proj_diff, w_proj_swa, w_out, ffn2_norm, ffn2_w_in, ffn2_w_out):
    batch, seq, d = x.shape
    depth = ffn1_norm.shape[0]
    assert d == D_MODEL and seq % ATT_TILE == 0 and (batch * seq) % ROW_TILE == 0
    xt = x.reshape(batch * seq, d)
    dbias, sbias = _bias_tiles(rel_bias)
    gmat = _group_mean_matrix()
    mtri = _neg_suffix_matrix()
    ffn1_in, ffn1_out = _cast_bf16(ffn1_w_in), _cast_bf16(ffn1_w_out)
    ffn2_in, ffn2_out = _cast_bf16(ffn2_w_in), _cast_bf16(ffn2_w_out)
    w1, w2, w3 = _regroup_projection(w_in)
    w_sb, w_diff, w_swa, w_o = (w.astype(BF16) for w in (w_proj_sb, w_proj_diff, w_proj_swa, w_out))
    for l in range(depth):
        lam_init = 0.8 - 0.6 * math.exp(-0.3 * l)
        xt = _ffn(xt, ffn1_norm[l], ffn1_in, ffn1_out, l)
        g2 = _qk_norm_gains(q_norm_diff[l], k_norm_diff[l], q_norm_swa[l], k_norm_swa[l])
        p1, p2, p3 = _projection(xt, mix_norm[l], w1, w2, g2, gmat, w3, l)
        att = _attention(swa_sinks[l], p1, p2, p3, mtri, dbias, sbias, diff_lambda[l],
                         diff_subln[l], lam_init, batch, seq)
        xt = _merge_ffn(xt, att.reshape(batch * seq, ATT_OUT_W), p1, w_sb, w_diff, w_swa, w_o,
                        ffn2_norm[l], ffn2_in, ffn2_out, l)
    return xt.reshape(batch, seq, d)
```
